```python
import jax, jax.numpy as jnp
from jax import lax
import numpy as np

D_MODEL = 1024
BATCH = 4
SEQ = 8192
DEPTH = 2

GRID_W = 64
CTX_LEN = 256
N_EVEN = (DEPTH + 1) // 2
N_ODD = DEPTH // 2
N_MOD = 6

POOL_WINDOWS = (2, 4, 8, 16)
POOL_GROUPS = len(POOL_WINDOWS)
D_POOL = D_MODEL // 2
POOL_GW = D_POOL // POOL_GROUPS
D_CONV = D_MODEL // 2
CONV_W = 31
AB_IN = D_POOL + 2 * D_CONV
AB_MIX = D_POOL + D_CONV
MLSTM_HEADS = 4
MLSTM_DH = D_MODEL // 8
D_MLSTM = MLSTM_HEADS * MLSTM_DH
MLSTM_CHUNK = 128
N_GATE = 4 * MLSTM_HEADS
ATT_QH = 8
ATT_KVH = 2
ATT_DH = D_MODEL // 16
ATT_GROUP = ATT_QH // ATT_KVH
ATT_BLOCK = 128
ROPE_THETA = 10000.0
ROPE_F = ATT_DH // 4
CD_SPLITS = (D_MLSTM, D_MLSTM, D_MLSTM, D_MLSTM, N_GATE, ATT_QH * ATT_DH, ATT_KVH * ATT_DH, ATT_KVH * ATT_DH)
CD_IN = sum(CD_SPLITS)
CD_CUTS = tuple(sum(CD_SPLITS[:i + 1]) for i in range(len(CD_SPLITS) - 1))
CD_MIX = D_MLSTM + ATT_QH * ATT_DH
PEER_HEADS = 8
PEER_NKEYS = 128
PEER_N = PEER_NKEYS * PEER_NKEYS
PEER_TOPK = 16
PEER_DKEY = 256
PEER_BLOCK = 128
DN_ALPHA = (2 * DEPTH) ** 0.25
DN_BETA = (8 * DEPTH) ** -0.25
LN_EPS = 1e-5
RMS_EPS = 1e-6

kernel_name = "hybrid_pool_conv_mlstm_gqa_peer_trunk"

F32 = jnp.float32


def layer_norm(x, g, b):
    xf = x.astype(F32)
    mu = jnp.mean(xf, -1, keepdims=True)
    var = jnp.mean(jnp.square(xf - mu), -1, keepdims=True)
    return ((xf - mu) * lax.rsqrt(var + LN_EPS) * g + b).astype(x.dtype)


def rms_norm(x, g):
    xf = x.astype(F32)
    return (xf * lax.rsqrt(jnp.mean(xf * xf, -1, keepdims=True) + RMS_EPS) * g).astype(x.dtype)


def multiscale_pool(xa, w_pool, ls_pool):
    T = xa.shape[1]
    xf = xa.astype(F32)
    cs = jnp.concatenate([jnp.zeros_like(xf[:, :1]), jnp.cumsum(xf, axis=1)], axis=1)
    t = jnp.arange(T)
    outs = []
    for g, w in enumerate(POOL_WINDOWS):
        lo = jnp.clip(t - w // 2, 0, T)
        hi = jnp.clip(t + w // 2, 0, T)
        sl = slice(g * POOL_GW, (g + 1) * POOL_GW)
        csg = cs[:, :, sl]
        mean = (csg[:, hi] - csg[:, lo]) / (hi - lo).astype(F32)[None, :, None]
        outs.append((mean - xf[:, :, sl]).astype(xa.dtype) @ w_pool[g])
    return jnp.concatenate(outs, axis=-1) * ls_pool


def conformer_conv(xb, conv_w, conv_b, ln_g, ln_b):
    a, gt = jnp.split(xb, 2, axis=-1)
    u = a * jax.nn.sigmoid(gt)
    y = lax.conv_general_dilated(u, conv_w[:, None, :], window_strides=(1,),
                                 padding=[(CONV_W // 2, CONV_W // 2)],
                                 dimension_numbers=('NWC', 'WIO', 'NWC'),
                                 feature_group_count=D_CONV) + conv_b
    return jax.nn.silu(layer_norm(y, ln_g, ln_b))


def mixer_ab(h, w_in, w_pool, ls_pool, conv_w, conv_b, cln_g, cln_b, w_out):
    z = h @ w_in
    ya = multiscale_pool(z[..., :D_POOL], w_pool, ls_pool)
    yb = conformer_conv(z[..., D_POOL:], conv_w, conv_b, cln_g, cln_b)
    return jnp.concatenate([ya, yb], axis=-1) @ w_out


def mlstm_chunkwise(q, k, v, ig, fg, state):
    B_, H, T, _ = q.shape
    L = MLSTM_CHUNK
    nc = T // L

    def chunks(a):
        return jnp.moveaxis(a.reshape(B_, H, nc, L, *a.shape[3:]), 2, 0)

    causal = jnp.tril(jnp.ones((L, L), dtype=bool))

    def step(carry, inp):
        C, n, m = carry
        qj, kj, vj, ij, fj = inp
        b = jnp.cumsum(jax.nn.log_sigmoid(fj), axis=-1)
        dmat = jnp.where(causal, b[..., :, None] - b[..., None, :] + ij[..., None, :], -jnp.inf)
        inter = b + m[..., None]
        m_t = jnp.maximum(inter, jnp.max(dmat, -1))
        w_inter = jnp.exp(inter - m_t)
        s = jnp.einsum('bhid,bhjd->bhij', qj, kj) * jnp.exp(dmat - m_t[..., None])
        num = jnp.einsum('bhij,bhjv->bhiv', s, vj) + w_inter[..., None] * jnp.einsum('bhid,bhdv->bhiv', qj, C)
        den = jnp.sum(s, -1) + w_inter * jnp.einsum('bhid,bhd->bhi', qj, n)
        h = num / jnp.maximum(jnp.abs(den), jnp.exp(-m_t))[..., None]
        b_last = b[..., -1]
        g_log = b_last[..., None] - b + ij
        m_new = jnp.maximum(b_last + m, jnp.max(g_log, -1))
        wk = jnp.exp(g_log - m_new[..., None])
        decay = jnp.exp(b_last + m - m_new)
        C_new = decay[..., None, None] * C + jnp.einsum('bhj,bhjd,bhjv->bhdv', wk, kj, vj)
        n_new = decay[..., None] * n + jnp.einsum('bhj,bhjd->bhd', wk, kj)
        return (C_new, n_new, m_new), h

    state, hs = lax.scan(step, state, tuple(chunks(a) for a in (q, k, v, ig, fg)))
    return jnp.moveaxis(hs, 0, 2).reshape(B_, H, T, -1), state


def mlstm_direction(q, k, v, ig, fg, state, reverse):
    if reverse:
        q, k, v, ig, fg = (jnp.flip(a, axis=2) for a in (q, k, v, ig, fg))
    h, state = mlstm_chunkwise(q, k, v, ig, fg, state)
    if reverse:
        h = jnp.flip(h, axis=2)
    return h, state


def mlstm_inputs(zq, zk, zv, zg, gate_b):
    B_, T, _ = zq.shape

    def heads(a):
        return a.reshape(B_, T, MLSTM_HEADS, MLSTM_DH).transpose(0, 2, 1, 3).astype(F32)

    g = (zg.astype(F32).reshape(B_, T, 4, MLSTM_HEADS) + gate_b.astype(F32)).transpose(2, 0, 3, 1)
    return heads(zq), heads(zk) * (MLSTM_DH ** -0.5), heads(zv), g


def mlstm_out(h, o_pre, norm_g):
    B_, H, T, dv = h.shape
    mu = jnp.mean(h, -1, keepdims=True)
    var = jnp.mean(jnp.square(h - mu), -1, keepdims=True)
    hn = ((h - mu) * lax.rsqrt(var + LN_EPS)).transpose(0, 2, 1, 3).reshape(B_, T, H * dv) * norm_g
    return (hn * jax.nn.sigmoid(o_pre.astype(F32))).astype(o_pre.dtype)


def mlstm_zero_state(b_):
    return (jnp.zeros((b_, MLSTM_HEADS, MLSTM_DH, MLSTM_DH), F32),
            jnp.zeros((b_, MLSTM_HEADS, MLSTM_DH), F32),
            jnp.zeros((b_, MLSTM_HEADS), F32))


def axial_rope_tables(row, col):
    freqs = ROPE_THETA ** (-jnp.arange(ROPE_F, dtype=F32) / ROPE_F)
    ar = row.astype(F32)[:, None] * freqs
    ac = col.astype(F32)[:, None] * freqs
    return (jnp.cos(ar), jnp.sin(ar), jnp.cos(ac), jnp.sin(ac))


def rope_1d(x, cos, sin):
    x1, x2 = jnp.split(x, 2, axis=-1)
    cos, sin = cos[:, None, :], sin[:, None, :]
    return jnp.concatenate([x1 * cos - x2 * sin, x1 * sin + x2 * cos], axis=-1)


def apply_axial_rope(x, tabs):
    cr, sr, cc, sc = tabs
    xr, xc = jnp.split(x, 2, axis=-1)
    return jnp.concatenate([rope_1d(xr, cr, sr), rope_1d(xc, cc, sc)], axis=-1).astype(x.dtype)


def attn_inputs(zq, zk, zv, qn_g, kn_g, rope):
    B_, T, _ = zq.shape
    q = rms_norm(zq.reshape(B_, T, ATT_QH, ATT_DH), qn_g)
    k = rms_norm(zk.reshape(B_, T, ATT_KVH, ATT_DH), kn_g)
    v = zv.reshape(B_, T, ATT_KVH, ATT_DH)
    if rope is not None:
        q, k = apply_axial_rope(q, rope), apply_axial_rope(k, rope)
    return q, k, v


def blocked_attention(q, k, v):
    B_, Tq, _, _ = q.shape
    nb = Tq // ATT_BLOCK
    qb = q.reshape(B_, nb, ATT_BLOCK, ATT_KVH, ATT_GROUP, ATT_DH).transpose(1, 0, 2, 3, 4, 5)
    scale = ATT_DH ** -0.5

    def block(qi):
        s = jnp.einsum('bqhgd,bkhd->bhgqk', qi, k).astype(F32) * scale
        p = jax.nn.softmax(s, axis=-1).astype(v.dtype)
        return jnp.einsum('bhgqk,bkhd->bqhgd', p, v)

    o = lax.map(block, qb)
    return o.transpose(1, 0, 2, 3, 4, 5).reshape(B_, Tq, ATT_QH * ATT_DH)


def mixer_cd(h_l, h_c, w_in, gate_b, norm_g, qn_g, kn_g, w_out, rope, ctx_out):
    qm_l, km_l, vm_l, om_l, gt_l, qa_l, ka_l, va_l = jnp.split(h_l @ w_in, CD_CUTS, axis=-1)
    qm_c, km_c, vm_c, om_c, gt_c, qa_c, ka_c, va_c = jnp.split(h_c @ w_in, CD_CUTS, axis=-1)
    Qc, Kc, Vc, Gc = mlstm_inputs(qm_c, km_c, vm_c, gt_c, gate_b)
    Ql, Kl, Vl, Gl = mlstm_inputs(qm_l, km_l, vm_l, gt_l, gate_b)
    zero = mlstm_zero_state(h_c.shape[0])
    hcf, st_f = mlstm_direction(Qc, Kc, Vc, Gc[0], Gc[1], zero, False)
    hcb, st_b = mlstm_direction(Qc, Kc, Vc, Gc[2], Gc[3], zero, True)
    hlf, _ = mlstm_direction(Ql, Kl, Vl, Gl[0], Gl[1], st_f, False)
    hlb, _ = mlstm_direction(Ql, Kl, Vl, Gl[2], Gl[3], st_b, True)
    q_l, k_l, v_l = attn_inputs(qa_l, ka_l, va_l, qn_g, kn_g, rope)
    q_c, k_c, v_c = attn_inputs(qa_c, ka_c, va_c, qn_g, kn_g, None)
    y_la = blocked_attention(q_l, jnp.concatenate([k_c, k_l], 1), jnp.concatenate([v_c, v_l], 1))
    y_l = jnp.concatenate([mlstm_out(hlf + hlb, om_l, norm_g), y_la], axis=-1) @ w_out
    if ctx_out:
        y_ca = blocked_attention(q_c, k_c, v_c)
        y_c = jnp.concatenate([mlstm_out(hcf + hcb, om_c, norm_g), y_ca], axis=-1) @ w_out
    else:
        y_c = None
    return y_l, y_c


def peer_ffn(h, w_q, sub_keys, u_tab, v_tab):
    shape = h.shape
    tok = h.reshape(-1, PEER_BLOCK, D_MODEL)
    half = PEER_DKEY // 2
    kk = PEER_TOPK * PEER_TOPK

    def block(hb):
        q = (hb @ w_q).reshape(PEER_BLOCK, PEER_HEADS, 2, half)
        s = jnp.einsum('nhpd,pkd->nhpk', q, sub_keys).astype(F32)
        sv, si = lax.top_k(s, PEER_TOPK)
        cand = (sv[:, :, 0, :, None] + sv[:, :, 1, None, :]).reshape(PEER_BLOCK, PEER_HEADS, kk)
        cidx = (si[:, :, 0, :, None] * PEER_NKEYS + si[:, :, 1, None, :]).reshape(PEER_BLOCK, PEER_HEADS, kk)
        top_s, pos = lax.top_k(cand, PEER_TOPK)
        eidx = jnp.take_along_axis(cidx, pos, axis=-1)
        gate = jax.nn.softmax(top_s, axis=-1)
        act = jax.nn.gelu(jnp.einsum('nd,nhkd->nhk', hb, u_tab[eidx]).astype(F32), approximate=False)
        w = (gate * act).astype(hb.dtype)
        return jnp.einsum('nhk,nhkd->nd', w, v_tab[eidx])

    return lax.map(block, tok).reshape(shape)


def setup_inputs(seed: int = 0) -> dict:
    key = jax.random.key(seed)
    ks = jax.random.split(key, 32)

    def nrm(k, shape, s):
        return jax.random.normal(k, shape, F32) * s

    gate_i = nrm(ks[20], (N_ODD, 2, MLSTM_HEADS), 0.1)
    gate_f = 3.0 + 3.0 * jax.random.uniform(ks[21], (N_ODD, 2, MLSTM_HEADS), F32)
    mlstm_gate_b = jnp.stack([gate_i[:, 0], gate_f[:, 0], gate_i[:, 1], gate_f[:, 1]], axis=1)
    return {
        "x": nrm(ks[0], (BATCH, SEQ, D_MODEL), 1.0),
        "c": nrm(ks[1], (BATCH, D_MODEL), 1.0),
        "ctx": nrm(ks[2], (BATCH, CTX_LEN, D_MODEL), 1.0),
        "c_ctx": nrm(ks[3], (D_MODEL,), 1.0),
        "mod_w": nrm(ks[4], (DEPTH, D_MODEL, N_MOD * D_MODEL), 0.5 * D_MODEL ** -0.5),
        "mod_b": nrm(ks[5], (DEPTH, N_MOD * D_MODEL), 0.02),
        "ln_g": 1.0 + nrm(ks[6], (DEPTH, 2, D_MODEL), 0.02),
        "ln_b": nrm(ks[7], (DEPTH, 2, D_MODEL), 0.02),
        "ab_w_in": nrm(ks[8], (N_EVEN, D_MODEL, AB_IN), D_MODEL ** -0.5),
        "pool_w": nrm(ks[9], (N_EVEN, POOL_GROUPS, POOL_GW, POOL_GW), POOL_GW ** -0.5),
        "pool_ls": 1.0 + nrm(ks[10], (N_EVEN, D_POOL), 0.1),
        "conv_w": nrm(ks[11], (N_EVEN, CONV_W, D_CONV), CONV_W ** -0.5),
        "conv_b": nrm(ks[12], (N_EVEN, D_CONV), 0.02),
        "conv_ln_g": 1.0 + nrm(ks[13], (N_EVEN, D_CONV), 0.02),
        "conv_ln_b": nrm(ks[14], (N_EVEN, D_CONV), 0.02),
        "ab_w_out": nrm(ks[15], (N_EVEN, AB_MIX, D_MODEL), DN_BETA * AB_MIX ** -0.5),
        "cd_w_in": nrm(ks[16], (N_ODD, D_MODEL, CD_IN), D_MODEL ** -0.5),
        "mlstm_gate_b": mlstm_gate_b,
        "mlstm_norm_g": 1.0 + nrm(ks[17], (N_ODD, D_MLSTM), 0.02),
        "q_norm_g": 1.0 + nrm(ks[18], (N_ODD, ATT_DH), 0.02),
        "k_norm_g": 1.0 + nrm(ks[19], (N_ODD, ATT_DH), 0.02),
        "cd_w_out": nrm(ks[22], (N_ODD, CD_MIX, D_MODEL), DN_BETA * CD_MIX ** -0.5),
        "peer_w_q": nrm(ks[23], (DEPTH, D_MODEL, PEER_HEADS * PEER_DKEY), D_MODEL ** -0.5),
        "peer_keys": nrm(ks[24], (DEPTH, 2, PEER_NKEYS, PEER_DKEY // 2), (PEER_DKEY // 2) ** -0.5),
        "peer_u": nrm(ks[25], (DEPTH, PEER_N, D_MODEL), D_MODEL ** -0.5),
        "peer_v": nrm(ks[26], (DEPTH, PEER_N, D_MODEL), DN_BETA),
    }


def reference(x, c, ctx, c_ctx, mod_w, mod_b, ln_g, ln_b,
              ab_w_in, pool_w, pool_ls, conv_w, conv_b, conv_ln_g, conv_ln_b, ab_w_out,
              cd_w_in, mlstm_gate_b, mlstm_norm_g, q_norm_g, k_norm_g, cd_w_out,
              peer_w_q, peer_keys, peer_u, peer_v):
    n_lat = x.shape[1]
    ROWS = n_lat // GRID_W
    row = jnp.repeat(jnp.arange(ROWS), GRID_W)
    col = jnp.tile(jnp.arange(GRID_W), ROWS)
    rope = axial_rope_tables(row, col)
    x_l, x_c = x, ctx
    for l in range(DEPTH):
        ctx_out = l < DEPTH - 1
        j = l // 2
        mod_l = (jax.nn.silu(c) @ mod_w[l] + mod_b[l])[:, None, :]
        mod_c = (jax.nn.silu(c_ctx) @ mod_w[l] + mod_b[l])[None, None, :]
        sh1, sc1, g1, sh2, sc2, g2 = jnp.split(mod_l, N_MOD, axis=-1)
        csh1, csc1, cg1, csh2, csc2, cg2 = jnp.split(mod_c, N_MOD, axis=-1)
        h_l = x_l * (1 + sc1) + sh1
        if l % 2 == 0:
            ab = (ab_w_in[j], pool_w[j], pool_ls[j], conv_w[j], conv_b[j], conv_ln_g[j], conv_ln_b[j], ab_w_out[j])
            y_l = mixer_ab(h_l, *ab)
            y_c = mixer_ab(x_c * (1 + csc1) + csh1, *ab) if ctx_out else None
        else:
            y_l, y_c = mixer_cd(h_l, x_c * (1 + csc1) + csh1, cd_w_in[j], mlstm_gate_b[j], mlstm_norm_g[j],
                                q_norm_g[j], k_norm_g[j], cd_w_out[j], rope, ctx_out)
        x_l = layer_norm(DN_ALPHA * x_l + g1 * y_l, ln_g[l, 0], ln_b[l, 0])
        f_l = peer_ffn(x_l * (1 + sc2) + sh2, peer_w_q[l], peer_keys[l], peer_u[l], peer_v[l])
        x_l = layer_norm(DN_ALPHA * x_l + g2 * f_l, ln_g[l, 1], ln_b[l, 1])
        if ctx_out:
            x_c = layer_norm(DN_ALPHA * x_c + cg1 * y_c, ln_g[l, 0], ln_b[l, 0])
            f_c = peer_ffn(x_c * (1 + csc2) + csh2, peer_w_q[l], peer_keys[l], peer_u[l], peer_v[l])
            x_c = layer_norm(DN_ALPHA * x_c + cg2 * f_c, ln_g[l, 1], ln_b[l, 1])
    return x_l
```

```python
import functools
import math

import jax
import jax.numpy as jnp
import numpy as np
from jax import lax
from jax.experimental import pallas as pl
from jax.experimental.pallas import tpu as pltpu

F32 = jnp.float32
BF16 = jnp.bfloat16

LANES = 128
SUBLANES = 8
VMEM_LIMIT_BYTES = 56 * 1024 * 1024

GRID_W = 64
N_MOD = 6
POOL_WINDOWS = (2, 4, 8, 16)
CONV_W = 31
HALO = 16
MLSTM_HEADS = 4
MLSTM_CHUNK = 128
ATT_QH = 8
ATT_KVH = 2
ROPE_THETA = 10000.0
PEER_HEADS = 8
PEER_NKEYS = 128
PEER_TOPK = 16
LN_EPS = 1e-5
RMS_EPS = 1e-6
NEG_INF = float("-inf")

SH1, SC1, G1, SH2, SC2, G2 = range(N_MOD)


def _cparams(*sem):
    return pltpu.CompilerParams(dimension_semantics=sem, vmem_limit_bytes=VMEM_LIMIT_BYTES)


def _bdot(a, b):
    return jnp.dot(a.astype(BF16), b.astype(BF16), preferred_element_type=F32)


def _bdot_nt(a, b):
    return lax.dot_general(a.astype(BF16), b.astype(BF16), (((1,), (1,)), ((), ())),
                           preferred_element_type=F32)


def _layer_norm_rows(v, g, b):
    mu = jnp.mean(v, axis=-1, keepdims=True)
    d = v - mu
    var = jnp.mean(d * d, axis=-1, keepdims=True)
    return d * lax.rsqrt(var + LN_EPS) * g + b


def _mod_kernel(c_ref, w_ref, b_ref, o_ref):
    c = c_ref[...]
    a = c * jax.nn.sigmoid(c)
    o_ref[...] = jnp.dot(a, w_ref[...], preferred_element_type=F32,
                         precision=lax.Precision.HIGHEST) + b_ref[...]


def _modulation(cc, mod_w, mod_b):
    depth, d, n = mod_w.shape
    rows = cc.shape[0]
    bn = d
    out = pl.pallas_call(
        _mod_kernel,
        grid=(depth, n // bn),
        in_specs=[
            pl.BlockSpec((rows, d), lambda l, j: (0, 0)),
            pl.BlockSpec((None, d, bn), lambda l, j: (l, 0, j)),
            pl.BlockSpec((None, 1, bn), lambda l, j: (l, 0, j)),
        ],
        out_specs=pl.BlockSpec((None, rows, bn), lambda l, j: (l, 0, j)),
        out_shape=jax.ShapeDtypeStruct((depth, rows, n), F32),
        compiler_params=_cparams("parallel", "parallel"),
        name="modulation",
    )(cc, mod_w, mod_b.reshape(depth, 1, n))
    return out.reshape(depth, rows, N_MOD, d)


def _modmm_kernel(x_ref, mod_ref, w_ref, *o_refs, splits):
    h = x_ref[...] * (1.0 + mod_ref[SC1:SC1 + 1, :]) + mod_ref[SH1:SH1 + 1, :]
    z = _bdot(h, w_ref[...])
    off = 0
    for o_ref, n in zip(o_refs, splits):
        o_ref[...] = z[:, off:off + n].astype(o_ref.dtype)
        off += n


def _modulated_matmul(x, mod_l, w, splits, out_dtypes, tm, in_tile, stream_of_tile, n_tiles):
    d = x.shape[1]
    n = w.shape[1]
    assert sum(splits) == n
    return pl.pallas_call(
        functools.partial(_modmm_kernel, splits=splits),
        grid=(n_tiles,),
        in_specs=[
            pl.BlockSpec((tm, d), lambda t: (in_tile(t), 0)),
            pl.BlockSpec((None, N_MOD, d), lambda t: (stream_of_tile(t), 0, 0)),
            pl.BlockSpec((d, n), lambda t: (0, 0)),
        ],
        out_specs=[pl.BlockSpec((tm, s), lambda t: (t, 0)) for s in splits],
        out_shape=[jax.ShapeDtypeStruct((n_tiles * tm, s), dt) for s, dt in zip(splits, out_dtypes)],
        compiler_params=_cparams("parallel"),
        name="modulated_matmul",
    )(x, mod_l, w.astype(BF16))


def _ab_tail_kernel(zp_ref, zm_ref, zn_ref, x_ref, mod_ref, pw_ref, pls_ref, cw_ref, cb_ref,
                    cg_ref, cbb_ref, wo_ref, lg_ref, lb_ref, o_ref, zpool, ubuf, ymix,
                    *, tm, d_pool, d_conv, n_lat_tiles, lat_tiles_per_seq, ctx_tiles_per_seq,
                    dn_alpha, row_chunk):
    t = pl.program_id(0)
    is_lat = t < n_lat_tiles
    per_seq = jnp.where(is_lat, lat_tiles_per_seq, ctx_tiles_per_seq)
    pos_tile = jnp.where(is_lat, t, t - n_lat_tiles) % per_seq
    first = pos_tile == 0
    last = pos_tile == per_seq - 1
    seq_len = per_seq * tm

    def glu(z):
        return z[:, d_pool:d_pool + d_conv] * jax.nn.sigmoid(z[:, d_pool + d_conv:])

    zp = jnp.where(first, 0.0, zp_ref[...])
    zn = jnp.where(last, 0.0, zn_ref[...])
    zm = zm_ref[...]
    zpool[0:HALO, :] = zp[:, :d_pool]
    zpool[HALO:HALO + tm, :] = zm[:, :d_pool]
    zpool[HALO + tm:, :] = zn[:, :d_pool]
    ubuf[0:HALO, :] = glu(zp)
    ubuf[HALO:HALO + tm, :] = glu(zm)
    ubuf[HALO + tm:, :] = glu(zn)

    gw = d_pool // len(POOL_WINDOWS)
    tpos = pos_tile * tm + lax.broadcasted_iota(jnp.int32, (tm, gw), 0)
    for g, w in enumerate(POOL_WINDOWS):
        cols = slice(g * gw, (g + 1) * gw)
        acc = zpool[HALO - w // 2:HALO - w // 2 + tm, cols]
        for s in range(1, w):
            acc = acc + zpool[HALO - w // 2 + s:HALO - w // 2 + s + tm, cols]
        cnt = jnp.minimum(tpos + w // 2, seq_len) - jnp.maximum(tpos - w // 2, 0)
        diff = acc / cnt.astype(F32) - zpool[HALO:HALO + tm, cols]
        ymix[:, cols] = (_bdot(diff, pw_ref[g]) * pls_ref[:, cols]).astype(BF16)

    half = CONV_W // 2
    for r in range(0, tm, row_chunk):
        base = HALO - half + r
        acc = ubuf[base:base + row_chunk, :] * cw_ref[0:1, :]
        for k in range(1, CONV_W):
            acc = acc + ubuf[base + k:base + k + row_chunk, :] * cw_ref[k:k + 1, :]
        yn = _layer_norm_rows(acc + cb_ref[...], cg_ref[...], cbb_ref[...])
        ymix[r:r + row_chunk, d_pool:] = (yn * jax.nn.sigmoid(yn)).astype(BF16)

    y = jnp.dot(ymix[...], wo_ref[...], preferred_element_type=F32)
    v = dn_alpha * x_ref[...] + mod_ref[G1:G1 + 1, :] * y
    o_ref[...] = _layer_norm_rows(v, lg_ref[...], lb_ref[...])


def _ab_tail(z, x, mod_l, pool_w, pool_ls, conv_w, conv_b, cln_g, cln_b, w_out, ln_g, ln_b,
             tm, n_lat_tiles, lat_tiles_per_seq, ctx_tiles_per_seq, stream_of_tile, dn_alpha):
    ntok, d = x.shape
    d_pool = pool_ls.shape[0]
    d_conv = conv_b.shape[0]
    n_tiles = ntok // tm
    hb = tm // HALO
    n_hblocks = ntok // HALO
    row = lambda a: a.reshape(1, -1)
    kern = functools.partial(
        _ab_tail_kernel, tm=tm, d_pool=d_pool, d_conv=d_conv, n_lat_tiles=n_lat_tiles,
        lat_tiles_per_seq=lat_tiles_per_seq, ctx_tiles_per_seq=ctx_tiles_per_seq,
        dn_alpha=dn_alpha, row_chunk=32)
    const2 = lambda t: (0, 0)
    return pl.pallas_call(
        kern,
        grid=(n_tiles,),
        in_specs=[
            pl.BlockSpec((HALO, z.shape[1]), lambda t: (jnp.maximum(t * hb - 1, 0), 0)),
            pl.BlockSpec((tm, z.shape[1]), lambda t: (t, 0)),
            pl.BlockSpec((HALO, z.shape[1]), lambda t: (jnp.minimum((t + 1) * hb, n_hblocks - 1), 0)),
            pl.BlockSpec((tm, d), lambda t: (t, 0)),
            pl.BlockSpec((None, N_MOD, d), lambda t: (stream_of_tile(t), 0, 0)),
            pl.BlockSpec(pool_w.shape, lambda t: (0, 0, 0)),
            pl.BlockSpec((1, d_pool), const2),
            pl.BlockSpec(conv_w.shape, const2),
            pl.BlockSpec((1, d_conv), const2),
            pl.BlockSpec((1, d_conv), const2),
            pl.BlockSpec((1, d_conv), const2),
            pl.BlockSpec(w_out.shape, const2),
            pl.BlockSpec((1, d), const2),
            pl.BlockSpec((1, d), const2),
        ],
        out_specs=pl.BlockSpec((tm, d), lambda t: (t, 0)),
        out_shape=jax.ShapeDtypeStruct((ntok, d), F32),
        scratch_shapes=[
            pltpu.VMEM((tm + 2 * HALO, d_pool), F32),
            pltpu.VMEM((tm + 2 * HALO, d_conv), F32),
            pltpu.VMEM((tm, d_pool + d_conv), BF16),
        ],
        compiler_params=_cparams("parallel"),
        name="ab_tail",
    )(z, z, z, x, mod_l, pool_w.astype(BF16), row(pool_ls), conv_w, row(conv_b), row(cln_g),
      row(cln_b), w_out.astype(BF16), row(ln_g), row(ln_b))


def _top_values(s, k):
    outs = []
    cur = s
    for it in range(k):
        m = jnp.max(cur, axis=0, keepdims=True)
        outs.append(m)
        if it + 1 < k:
            cur = jnp.where(cur >= m, NEG_INF, cur)
    return outs


def _peer_route_kernel(x_ref, mod_ref, wq_ref, keys_ref, a_ref, nd_ref, c_ref, bn_ref, qt_ref,
                       *, tm, half):
    h = x_ref[...] * (1.0 + mod_ref[SC2:SC2 + 1, :]) + mod_ref[SH2:SH2 + 1, :]
    qt_ref[...] = _bdot_nt(wq_ref[...], h).astype(BF16)
    k = PEER_TOPK
    for hd in range(PEER_HEADS):
        for lt in range(tm // LANES):
            lanes = slice(lt * LANES, (lt + 1) * LANES)
            r0 = hd * 2 * half
            s1 = jnp.dot(keys_ref[0], qt_ref[r0:r0 + half, lanes], preferred_element_type=F32)
            s2 = jnp.dot(keys_ref[1], qt_ref[r0 + half:r0 + 2 * half, lanes],
                         preferred_element_type=F32)
            t1 = _top_values(s1, k)
            t2 = _top_values(s2, k)
            t2_all = jnp.concatenate(t2, axis=0)
            t1_tail = jnp.concatenate(t1[k // 2:], axis=0)
            pieces = []
            for a in range(k // 2):
                nb = min(k, (k + 1) // (a + 1))
                pieces.append(t1[a] + t2_all[:nb, :])
            pieces.append(t1_tail + t2[0])
            n_cand = sum(p.shape[0] for p in pieces)
            pad = (-n_cand) % SUBLANES
            if pad:
                pieces.append(jnp.full((pad, LANES), NEG_INF, F32))
            cand = jnp.concatenate(pieces, axis=0)
            top = _top_values(cand, k + 1)
            kth = top[k - 1]
            thr = 0.5 * (kth + jnp.maximum(top[k], kth - 1.0))
            m1, m2 = t1[0], t2[0]
            z = jnp.sum(jnp.where(cand >= thr, jnp.exp(cand - (m1 + m2)), 0.0), axis=0, keepdims=True)
            a_ref[hd, :, lanes] = jnp.where(s1 >= t1[k - 1], jnp.exp(s1 - m1), 0.0)
            nd_ref[hd, :, lanes] = -s1
            c_ref[hd, :, lanes] = s2 - thr
            bn_ref[hd, :, lanes] = jnp.where(s2 >= t2[k - 1], jnp.exp(s2 - m2), 0.0) / z


def _peer_route(x, mod_l, w_q, keys, tm, stream_of_tile):
    ntok, d = x.shape
    nq = w_q.shape[1]
    half = keys.shape[2]
    nk = keys.shape[1]
    n_tiles = ntok // tm
    sel_shape = jax.ShapeDtypeStruct((PEER_HEADS, nk, ntok), F32)
    sel_spec = pl.BlockSpec((PEER_HEADS, nk, tm), lambda t: (0, 0, t))
    return pl.pallas_call(
        functools.partial(_peer_route_kernel, tm=tm, half=half),
        grid=(n_tiles,),
        in_specs=[
            pl.BlockSpec((tm, d), lambda t: (t, 0)),
            pl.BlockSpec((None, N_MOD, d), lambda t: (stream_of_tile(t), 0, 0)),
            pl.BlockSpec((nq, d), lambda t: (0, 0)),
            pl.BlockSpec(keys.shape, lambda t: (0, 0, 0)),
        ],
        out_specs=[sel_spec] * 4,
        out_shape=[sel_shape] * 4,
        scratch_shapes=[pltpu.VMEM((nq, tm), BF16)],
        compiler_params=_cparams("parallel"),
        name="peer_route",
    )(x, mod_l, w_q.T.astype(BF16), keys.astype(BF16))


def _gelu_exact(a):
    return 0.5 * a * (1.0 + lax.erf(a * (1.0 / math.sqrt(2.0))))


def _peer_dense_kernel(x_ref, mod_ref, a_ref, nd_ref, c_ref, bn_ref, u_ref, vt_ref, lg_ref, lb_ref,
                       o_ref, hbf, acc, wt, at, *, tm, groups, fk, nk, dn_alpha, row_chunk):
    e = pl.program_id(1)

    @pl.when(e == 0)
    def _():
        h = x_ref[...] * (1.0 + mod_ref[SC2:SC2 + 1, :]) + mod_ref[SH2:SH2 + 1, :]
        hbf[...] = h.astype(BF16)
        acc[...] = jnp.zeros_like(acc)

    def group_step(p, carry):
        at[...] = _bdot_nt(u_ref[p], hbf[...])
        for lt in range(tm // LANES):
            lanes = slice(lt * LANES, (lt + 1) * LANES)
            for rc in range(nk // row_chunk):
                rows = slice(rc * row_chunk, (rc + 1) * row_chunk)
                gates = [None] * fk
                for hd in range(PEER_HEADS):
                    cv = c_ref[hd, rows, lanes]
                    bv = bn_ref[hd, rows, lanes]
                    for q in range(fk):
                        a_row = a_ref[hd, p * fk + q, :, lanes]
                        nd_row = nd_ref[hd, p * fk + q, :, lanes]
                        g = jnp.where(cv >= nd_row, bv * a_row, 0.0)
                        gates[q] = g if gates[q] is None else gates[q] + g
                for q in range(fk):
                    erows = slice(q * nk + rc * row_chunk, q * nk + (rc + 1) * row_chunk)
                    wt[erows, lanes] = (_gelu_exact(at[erows, lanes]) * gates[q]).astype(BF16)
        acc[...] += jnp.dot(vt_ref[p], wt[...], preferred_element_type=F32)
        return carry

    lax.fori_loop(0, groups, group_step, 0)

    @pl.when(e == pl.num_programs(1) - 1)
    def _():
        f = acc[...].T
        v = dn_alpha * x_ref[...] + mod_ref[G2:G2 + 1, :] * f
        o_ref[...] = _layer_norm_rows(v, lg_ref[...], lb_ref[...])


PEER_FIRST_KEYS_PER_PASS = 2
PEER_GROUPS_PER_STEP = 4


def _peer_dense(x, mod_l, sel, u_tab, v_tab, ln_g, ln_b, tm, stream_of_tile, dn_alpha):
    ntok, d = x.shape
    n_exp = u_tab.shape[0]
    a_sel, nd_sel, c_sel, bn_sel = sel
    nk = c_sel.shape[1]
    n_tiles = ntok // tm
    fk, groups = PEER_FIRST_KEYS_PER_PASS, PEER_GROUPS_PER_STEP
    ge = fk * nk
    n_groups = n_exp // ge
    a_sel = a_sel.reshape(PEER_HEADS, nk, 1, ntok)
    nd_sel = nd_sel.reshape(PEER_HEADS, nk, 1, ntok)
    u3 = u_tab.astype(BF16).reshape(n_groups, ge, d)
    vt3 = v_tab.astype(BF16).reshape(n_groups, ge, d).transpose(0, 2, 1)
    row_spec = pl.BlockSpec((PEER_HEADS, groups * fk, 1, tm), lambda t, e: (0, e, 0, t))
    full_spec = pl.BlockSpec((PEER_HEADS, nk, tm), lambda t, e: (0, 0, t))
    kern = functools.partial(_peer_dense_kernel, tm=tm, groups=groups, fk=fk, nk=nk, dn_alpha=dn_alpha,
                             row_chunk=2 * SUBLANES)
    return pl.pallas_call(
        kern,
        grid=(n_tiles, n_groups // groups),
        in_specs=[
            pl.BlockSpec((tm, d), lambda t, e: (t, 0)),
            pl.BlockSpec((None, N_MOD, d), lambda t, e: (stream_of_tile(t), 0, 0)),
            row_spec, row_spec, full_spec, full_spec,
            pl.BlockSpec((groups, ge, d), lambda t, e: (e, 0, 0)),
            pl.BlockSpec((groups, d, ge), lambda t, e: (e, 0, 0)),
            pl.BlockSpec((1, d), lambda t, e: (0, 0)),
            pl.BlockSpec((1, d), lambda t, e: (0, 0)),
        ],
        out_specs=pl.BlockSpec((tm, d), lambda t, e: (t, 0)),
        out_shape=jax.ShapeDtypeStruct((ntok, d), F32),
        scratch_shapes=[
            pltpu.VMEM((tm, d), BF16),
            pltpu.VMEM((d, tm), F32),
            pltpu.VMEM((ge, tm), BF16),
            pltpu.VMEM((ge, tm), F32),
        ],
        compiler_params=_cparams("parallel", "arbitrary"),
        name="peer_dense",
    )(x, mod_l, a_sel, nd_sel, c_sel, bn_sel, u3, vt3, ln_g.reshape(1, d), ln_b.reshape(1, d))


def _peer_layer(x, mod_l, w_q, keys, u_tab, v_tab, ln_g, ln_b, tm, stream_of_tile, dn_alpha):
    sel = _peer_route(x, mod_l, w_q, keys, tm, stream_of_tile)
    return _peer_dense(x, mod_l, sel, u_tab, v_tab, ln_g, ln_b, tm, stream_of_tile, dn_alpha)


def _rope(v, cos, sin_signed):
    n = v.shape[-1]
    lane = lax.broadcasted_iota(jnp.int32, v.shape, v.ndim - 1)
    quarter = sin_signed.shape[-1] // 8
    partner = jnp.where(lane % (2 * quarter) < quarter,
                        pltpu.roll(v, n - quarter, v.ndim - 1), pltpu.roll(v, quarter, v.ndim - 1))
    reps = n // cos.shape[-1]
    if reps > 1:
        cos = jnp.concatenate([cos] * reps, axis=-1)
        sin_signed = jnp.concatenate([sin_signed] * reps, axis=-1)
    return v * cos + partner * sin_signed


def _cd_in_kernel(x_ref, mod_ref, w_ref, cos_ref, sin_ref, qg_ref, kg_ref,
                  qm_ref, km_ref, vm_ref, om_ref, qx_ref, ka_ref, va_ref, gt_ref,
                  *, d_m, d_qx, d_kv, att_dh, k_scale):
    h = x_ref[...] * (1.0 + mod_ref[SC1:SC1 + 1, :]) + mod_ref[SH1:SH1 + 1, :]
    z = _bdot(h, w_ref[...])
    qm_ref[...] = z[:, 0:d_m].astype(BF16)
    km_ref[...] = (z[:, d_m:2 * d_m] * k_scale).astype(BF16)
    vm_ref[...] = z[:, 2 * d_m:3 * d_m].astype(BF16)
    om_ref[...] = z[:, 3 * d_m:4 * d_m]
    off = 4 * d_m
    cos = cos_ref[...]
    sin = sin_ref[...]
    qx = z[:, off:off + d_qx]
    pieces = []
    for hq in range(d_qx // LANES):
        blk = qx[:, hq * LANES:(hq + 1) * LANES]
        ms = jnp.sum(blk * blk, axis=-1, keepdims=True) * (1.0 / att_dh)
        pieces.append(blk * lax.rsqrt(ms + RMS_EPS))
    qn = jnp.concatenate(pieces, axis=-1) * qg_ref[...]
    qx_ref[...] = _rope(qn, cos, sin).astype(BF16)
    off += d_qx
    kk = z[:, off:off + d_kv]
    lane = lax.broadcasted_iota(jnp.int32, kk.shape, 1)
    sq = kk * kk
    pieces = []
    for hk in range(d_kv // att_dh):
        sel = (lane >= hk * att_dh) & (lane < (hk + 1) * att_dh)
        ms = jnp.sum(jnp.where(sel, sq, 0.0), axis=-1, keepdims=True) * (1.0 / att_dh)
        pieces.append((sel, lax.rsqrt(ms + RMS_EPS)))
    scale = jnp.zeros_like(kk)
    for sel, r in pieces:
        scale = jnp.where(sel, r, scale)
    ka_ref[...] = _rope(kk * scale * kg_ref[...], cos, sin).astype(BF16)
    off += d_kv
    va_ref[...] = z[:, off:off + d_kv].astype(BF16)
    off += d_kv
    gt_ref[...] = z[:, off:]


def _cd_in(x, mod_l, w, cos_t, sin_t, q_gain, k_gain, tm, n_tiles, in_tile, stream_of_tile, pos_tile,
           d_m, d_qx, d_kv, att_dh, k_scale):
    d = x.shape[1]
    n = w.shape[1]
    ntok = n_tiles * tm
    widths = (d_m, d_m, d_m, d_m, d_qx, d_kv, d_kv, n - 4 * d_m - d_qx - 2 * d_kv)
    dtypes = (BF16, BF16, BF16, F32, BF16, BF16, BF16, F32)
    kern = functools.partial(_cd_in_kernel, d_m=d_m, d_qx=d_qx, d_kv=d_kv, att_dh=att_dh, k_scale=k_scale)
    return pl.pallas_call(
        kern,
        grid=(n_tiles,),
        in_specs=[
            pl.BlockSpec((tm, d), lambda t: (in_tile(t), 0)),
            pl.BlockSpec((None, N_MOD, d), lambda t: (stream_of_tile(t), 0, 0)),
            pl.BlockSpec((d, n), lambda t: (0, 0)),
            pl.BlockSpec((tm, cos_t.shape[1]), lambda t: (pos_tile(t), 0)),
            pl.BlockSpec((tm, sin_t.shape[1]), lambda t: (pos_tile(t), 0)),
            pl.BlockSpec((1, d_qx), lambda t: (0, 0)),
            pl.BlockSpec((1, d_kv), lambda t: (0, 0)),
        ],
        out_specs=[pl.BlockSpec((tm, wd), lambda t: (t, 0)) for wd in widths],
        out_shape=[jax.ShapeDtypeStruct((ntok, wd), dt) for wd, dt in zip(widths, dtypes)],
        compiler_params=_cparams("parallel"),
        name="cd_in",
    )(x, mod_l, w.astype(BF16), cos_t, sin_t, q_gain, k_gain)


def _mlstm_kernel(qf_ref, kf_ref, vf_ref, gf_ref, qb_ref, kb_ref, vb_ref, gb_ref, bias_ref,
                  hf_ref, hb_ref, c_s, n_s, m_s, *, chunk, dh, heads):
    s = pl.program_id(1)

    @pl.when(s == 0)
    def _():
        c_s[...] = jnp.zeros_like(c_s)
        n_s[...] = jnp.zeros_like(n_s)
        m_s[...] = jnp.zeros_like(m_s)

    ri = lax.broadcasted_iota(jnp.int32, (chunk, chunk), 0)
    cj = lax.broadcasted_iota(jnp.int32, (chunk, chunk), 1)
    streams = ((qf_ref, kf_ref, vf_ref, gf_ref, hf_ref), (qb_ref, kb_ref, vb_ref, gb_ref, hb_ref))
    for direction, (q_ref, k_ref, v_ref, g_ref, h_ref) in enumerate(streams):
        mask = (cj <= ri) if direction == 0 else (cj >= ri)
        edge = chunk - 1 if direction == 0 else 0
        gates = g_ref[...] + bias_ref[...]
        logf = jax.nn.log_sigmoid(gates)
        bc = jnp.dot(mask.astype(F32), logf, preferred_element_type=F32, precision=lax.Precision.HIGHEST)
        br = bc.T
        gr = gates.T
        for hd in range(heads):
            ci = (2 * direction) * heads + hd
            cf = (2 * direction + 1) * heads + hd
            idx = direction * heads + hd
            b_col = bc[:, cf:cf + 1]
            b_row = br[cf:cf + 1, :]
            ig_row = gr[ci:ci + 1, :]
            ig_col = gates[:, ci:ci + 1]
            m_prev = m_s[idx][:, 0:1]
            dmat = jnp.where(mask, b_col - b_row + ig_row, NEG_INF)
            inter = b_col + m_prev
            m_t = jnp.maximum(inter, jnp.max(dmat, axis=-1, keepdims=True))
            dexp = jnp.exp(dmat - m_t)
            w_inter = jnp.exp(inter - m_t)
            cols = slice(hd * dh, (hd + 1) * dh)
            q = q_ref[:, cols]
            k = k_ref[:, cols]
            v = v_ref[:, cols]
            c_prev = c_s[idx]
            n_prev = n_s[idx]
            sm = _bdot_nt(q, k) * dexp
            num = _bdot(sm, v) + w_inter * _bdot(q, c_prev)
            den = (jnp.sum(sm, axis=-1, keepdims=True)
                   + w_inter * jnp.sum(q.astype(F32) * n_prev, axis=-1, keepdims=True))
            h_ref[:, cols] = num / jnp.maximum(jnp.abs(den), jnp.exp(-m_t))
            b_last = bc[edge:edge + 1, cf:cf + 1]
            g_log = b_last - b_col + ig_col
            m_new = jnp.maximum(b_last + m_prev, jnp.max(g_log, axis=0, keepdims=True))
            wk = jnp.exp(g_log - m_new)
            decay = jnp.exp(b_last + m_prev - m_new)
            kw = k.astype(F32) * wk
            c_s[idx] = decay * c_prev + lax.dot_general(
                kw.astype(BF16), v, (((0,), (0,)), ((), ())), preferred_element_type=F32)
            n_s[idx] = decay * n_prev + jnp.sum(kw, axis=0, keepdims=True)
            m_s[idx] = jnp.broadcast_to(m_new, (1, LANES))


def _mlstm(qm, km, vm, gt, gate_bias, n_batch, chunks_per_seq, ctx_chunks, heads, dh):
    chunk = MLSTM_CHUNK
    ntok, dm = qm.shape
    ng = gt.shape[1]

    def fwd(b, s):
        return (b * chunks_per_seq + s, 0)

    def bwd(b, s):
        r = jnp.where(s < ctx_chunks, ctx_chunks - 1 - s, ctx_chunks + chunks_per_seq - 1 - s)
        return (b * chunks_per_seq + r, 0)

    tok_f = pl.BlockSpec((chunk, dm), fwd)
    tok_b = pl.BlockSpec((chunk, dm), bwd)
    kern = functools.partial(_mlstm_kernel, chunk=chunk, dh=dh, heads=heads)
    return pl.pallas_call(
        kern,
        grid=(n_batch, chunks_per_seq),
        in_specs=[tok_f, tok_f, tok_f, pl.BlockSpec((chunk, ng), fwd),
                  tok_b, tok_b, tok_b, pl.BlockSpec((chunk, ng), bwd),
                  pl.BlockSpec((1, ng), lambda b, s: (0, 0))],
        out_specs=[pl.BlockSpec((chunk, dm), fwd), pl.BlockSpec((chunk, dm), bwd)],
        out_shape=[jax.ShapeDtypeStruct((ntok, dm), F32)] * 2,
        scratch_shapes=[
            pltpu.VMEM((2 * heads, dh, dh), F32),
            pltpu.VMEM((2 * heads, 1, dh), F32),
            pltpu.VMEM((2 * heads, 1, LANES), F32),
        ],
        compiler_params=_cparams("parallel", "arbitrary"),
        name="mlstm",
    )(qm, km, vm, gt, qm, km, vm, gt, gate_bias)


def _attn_kernel(q_ref, k_ref, v_ref, o_ref, *, n_qh):
    k = k_ref[...]
    v = v_ref[...]
    for hq in range(n_qh):
        cols = slice(hq * LANES, (hq + 1) * LANES)
        s = _bdot_nt(q_ref[:, cols], k)
        p = jnp.exp(s - jnp.max(s, axis=-1, keepdims=True))
        l = jnp.sum(p, axis=-1, keepdims=True)
        o = jnp.dot(p.astype(BF16), v, preferred_element_type=F32)
        o_ref[:, cols] = (o / l).astype(o_ref.dtype)


def _attention(qx, ka, va, tq, n_batch, seq_tiles, q_tile_of, kv_len):
    n_qh = qx.shape[1] // LANES
    return pl.pallas_call(
        functools.partial(_attn_kernel, n_qh=n_qh),
        grid=(n_batch, seq_tiles),
        in_specs=[
            pl.BlockSpec((tq, qx.shape[1]), lambda b, t: (q_tile_of(b, t), 0)),
            pl.BlockSpec((kv_len, ka.shape[1]), lambda b, t: (b, 0)),
            pl.BlockSpec((kv_len, va.shape[1]), lambda b, t: (b, 0)),
        ],
        out_specs=pl.BlockSpec((tq, qx.shape[1]), lambda b, t: (b * seq_tiles + t, 0)),
        out_shape=jax.ShapeDtypeStruct((n_batch * seq_tiles * tq, qx.shape[1]), BF16),
        compiler_params=_cparams("parallel", "parallel"),
        name="attention",
    )(qx, ka, va)


def _cd_tail_kernel(hf_ref, hb_ref, om_ref, att_ref, x_ref, mod_ref, ng_ref, wm_ref, wa_ref, lg_ref, lb_ref,
                    o_ref, *, dh, heads, dn_alpha):
    hsum = hf_ref[...] + hb_ref[...]
    pieces = []
    for hd in range(heads):
        cols = slice(hd * dh, (hd + 1) * dh)
        blk = hsum[:, cols]
        mu = jnp.mean(blk, axis=-1, keepdims=True)
        dlt = blk - mu
        var = jnp.mean(dlt * dlt, axis=-1, keepdims=True)
        pieces.append(dlt * lax.rsqrt(var + LN_EPS))
    hn = jnp.concatenate(pieces, axis=-1) * ng_ref[...] * jax.nn.sigmoid(om_ref[...])
    y = _bdot(hn, wm_ref[...]) + jnp.dot(att_ref[...], wa_ref[...], preferred_element_type=F32)
    v = dn_alpha * x_ref[...] + mod_ref[G1:G1 + 1, :] * y
    o_ref[...] = _layer_norm_rows(v, lg_ref[...], lb_ref[...])


def _cd_tail(hf, hb, om, att, x, mod_l, norm_g, w_m, w_a, ln_g, ln_b, tm, n_tiles, seq_tile_of,
             stream_of_tile, heads, dh, dn_alpha):
    d = x.shape[1]
    dm = hf.shape[1]
    const2 = lambda t: (0, 0)
    kern = functools.partial(_cd_tail_kernel, dh=dh, heads=heads, dn_alpha=dn_alpha)
    return pl.pallas_call(
        kern,
        grid=(n_tiles,),
        in_specs=[
            pl.BlockSpec((tm, dm), lambda t: (seq_tile_of(t), 0)),
            pl.BlockSpec((tm, dm), lambda t: (seq_tile_of(t), 0)),
            pl.BlockSpec((tm, dm), lambda t: (seq_tile_of(t), 0)),
            pl.BlockSpec((tm, att.shape[1]), lambda t: (t, 0)),
            pl.BlockSpec((tm, d), lambda t: (t, 0)),
            pl.BlockSpec((None, N_MOD, d), lambda t: (stream_of_tile(t), 0, 0)),
            pl.BlockSpec((1, dm), const2),
            pl.BlockSpec(w_m.shape, const2),
            pl.BlockSpec(w_a.shape, const2),
            pl.BlockSpec((1, d), const2),
            pl.BlockSpec((1, d), const2),
        ],
        out_specs=pl.BlockSpec((tm, d), lambda t: (t, 0)),
        out_shape=jax.ShapeDtypeStruct((n_tiles * tm, d), F32),
        compiler_params=_cparams("parallel"),
        name="cd_tail",
    )(hf, hb, om, att, x, mod_l, norm_g.reshape(1, dm), w_m.astype(BF16), w_a.astype(BF16),
      ln_g.reshape(1, d), ln_b.reshape(1, d))


def _layer0(xf, mod_l, p, geom):
    tm = geom["tm_mix"]
    n_tiles = xf.shape[0] // tm
    n_lat_tiles = geom["n_lat"] // tm
    lat_per_seq = geom["seq"] // tm
    ctx_per_seq = geom["ctx"] // tm
    n_batch = geom["batch"]

    def stream(t):
        return jnp.where(t < n_lat_tiles, t // lat_per_seq, n_batch)

    (z,) = _modulated_matmul(xf, mod_l, p["ab_w_in"], (p["ab_w_in"].shape[1],), (F32,), tm,
                             lambda t: t, stream, n_tiles)
    x1 = _ab_tail(z, xf, mod_l, p["pool_w"], p["pool_ls"], p["conv_w"], p["conv_b"], p["conv_ln_g"],
                  p["conv_ln_b"], p["ab_w_out"], p["ln_g"][0], p["ln_b"][0], tm, n_lat_tiles,
                  lat_per_seq, ctx_per_seq, stream, geom["dn_alpha"])
    tp = geom["tm_peer"]
    n_lat_p = geom["n_lat"] // tp
    lat_per_seq_p = geom["seq"] // tp

    def stream_p(t):
        return jnp.where(t < n_lat_p, t // lat_per_seq_p, n_batch)

    return _peer_layer(x1, mod_l, p["peer_w_q"], p["peer_keys"], p["peer_u"], p["peer_v"],
                       p["ln_g"][1], p["ln_b"][1], tp, stream_p, geom["dn_alpha"])


def _rope_tables(seq, ctx_len, att_dh):
    n_freq = att_dh // 4
    t = jnp.arange(seq)
    freqs = ROPE_THETA ** (-jnp.arange(n_freq, dtype=F32) / n_freq)
    ar = (t // GRID_W).astype(F32)[:, None] * freqs
    ac = (t % GRID_W).astype(F32)[:, None] * freqs
    cos = jnp.concatenate([jnp.cos(ar), jnp.cos(ar), jnp.cos(ac), jnp.cos(ac)], axis=-1)
    sin = jnp.concatenate([-jnp.sin(ar), jnp.sin(ar), -jnp.sin(ac), jnp.sin(ac)], axis=-1)
    cos = jnp.concatenate([jnp.ones((ctx_len, att_dh), F32), cos], axis=0)
    sin = jnp.concatenate([jnp.zeros((ctx_len, att_dh), F32), sin], axis=0)
    reps = LANES // att_dh
    return jnp.tile(cos, (1, reps)), jnp.tile(sin, (1, reps))


def _layer1(xf, mod_l, p, geom):
    tm = geom["tm_mix"]
    n_batch, seq, ctx_len = geom["batch"], geom["seq"], geom["ctx"]
    d = xf.shape[1]
    heads = MLSTM_HEADS
    d_m = p["mlstm_norm_g"].shape[0]
    dh = d_m // heads
    att_dh = p["q_norm_g"].shape[0]
    n_qh, n_kvh = ATT_QH, ATT_KVH
    group = n_qh // n_kvh
    d_q, d_kv = n_qh * att_dh, n_kvh * att_dh
    n_gate = 4 * heads
    ctx_tiles, lat_tiles = ctx_len // tm, seq // tm
    per_b = ctx_tiles + lat_tiles
    n_lat_tiles = n_batch * lat_tiles

    w = p["cd_w_in"]
    cuts = np.cumsum([0, d_m, d_m, d_m, d_m, n_gate, d_q, d_kv, d_kv])
    w_qm, w_km, w_vm, w_om, w_gt, w_qa, w_ka, w_va = (w[:, cuts[i]:cuts[i + 1]] for i in range(8))
    w_qx = jnp.zeros((d, n_qh, LANES), F32)
    q_gain = jnp.zeros((n_qh, LANES), F32)
    w_att = jnp.zeros((n_qh, LANES, d), F32)
    for hq in range(n_qh):
        lo = (hq // group) * att_dh
        w_qx = w_qx.at[:, hq, lo:lo + att_dh].set(w_qa[:, hq * att_dh:(hq + 1) * att_dh])
        q_gain = q_gain.at[hq, lo:lo + att_dh].set(p["q_norm_g"] * att_dh ** -0.5)
        w_att = w_att.at[hq, lo:lo + att_dh, :].set(p["cd_w_out"][d_m + hq * att_dh:d_m + (hq + 1) * att_dh])
    d_qx = n_qh * LANES
    w_all = jnp.concatenate([w_qm, w_km, w_vm, w_om, w_qx.reshape(d, d_qx), w_ka, w_va,
                             jnp.pad(w_gt, ((0, 0), (0, LANES - n_gate)))], axis=1)
    gate_bias = jnp.pad(p["mlstm_gate_b"].reshape(1, n_gate), ((0, 0), (0, LANES - n_gate)))
    cos_t, sin_t = _rope_tables(seq, ctx_len, att_dh)

    def in_tile(t):
        b, r = t // per_b, t % per_b
        return jnp.where(r < ctx_tiles, n_lat_tiles + b * ctx_tiles + r, b * lat_tiles + r - ctx_tiles)

    def stream_seq(t):
        return jnp.where(t % per_b < ctx_tiles, n_batch, t // per_b)

    qm, km, vm, om, qx, ka, va, gt = _cd_in(
        xf, mod_l, w_all, cos_t, sin_t, q_gain.reshape(1, d_qx), jnp.tile(p["k_norm_g"], n_kvh).reshape(1, d_kv),
        tm, n_batch * per_b, in_tile, stream_seq, lambda t: t % per_b, d_m, d_qx, d_kv, att_dh, dh ** -0.5)

    chunks_per_seq = (ctx_len + seq) // MLSTM_CHUNK
    hf, hb = _mlstm(qm, km, vm, gt, gate_bias, n_batch, chunks_per_seq, ctx_len // MLSTM_CHUNK, heads, dh)
    att = _attention(qx, ka, va, tm, n_batch, lat_tiles, lambda b, t: b * per_b + ctx_tiles + t,
                     ctx_len + seq)

    def seq_tile_of(t):
        return (t // lat_tiles) * per_b + ctx_tiles + t % lat_tiles

    x1 = _cd_tail(hf, hb, om, att, xf, mod_l, p["mlstm_norm_g"], p["cd_w_out"][:d_m], w_att.reshape(d_qx, d),
                  p["ln_g"][0], p["ln_b"][0], tm, n_lat_tiles, seq_tile_of, lambda t: t // lat_tiles,
                  heads, dh, geom["dn_alpha"])
    tp = geom["tm_peer"]
    lat_per_seq_p = seq // tp
    return _peer_layer(x1, mod_l, p["peer_w_q"], p["peer_keys"], p["peer_u"], p["peer_v"],
                       p["ln_g"][1], p["ln_b"][1], tp, lambda t: t // lat_per_seq_p, geom["dn_alpha"])


def kernel(x, c, ctx, c_ctx, mod_w, mod_b, ln_g, ln_b, ab_w_in, pool_w, pool_ls, conv_w, conv_b,
           conv_ln_g, conv_ln_b, ab_w_out, cd_w_in, mlstm_gate_b, mlstm_norm_g, q_norm_g, k_norm_g,
           cd_w_out, peer_w_q, peer_keys, peer_u, peer_v):
    n_batch, seq, d = x.shape
    ctx_len = ctx.shape[1]
    depth = mod_w.shape[0]
    assert depth == 2, "one pooling/convolution layer followed by one mLSTM/attention layer"
    n_streams = SUBLANES * (-(-(n_batch + 1) // SUBLANES))
    cc = jnp.concatenate([c, c_ctx[None], jnp.zeros((n_streams - n_batch - 1, d), F32)], axis=0)
    mod = _modulation(cc, mod_w, mod_b)
    geom = dict(tm_mix=256, tm_peer=512, n_lat=n_batch * seq, seq=seq, ctx=ctx_len, batch=n_batch,
                dn_alpha=(2 * depth) ** 0.25)
    xf = jnp.concatenate([x.reshape(-1, d), ctx.reshape(-1, d)], axis=0)
    p0 = dict(ab_w_in=ab_w_in[0], pool_w=pool_w[0], pool_ls=pool_ls[0], conv_w=conv_w[0],
              conv_b=conv_b[0], conv_ln_g=conv_ln_g[0], conv_ln_b=conv_ln_b[0], ab_w_out=ab_w_out[0],
              peer_w_q=peer_w_q[0], peer_keys=peer_keys[0], peer_u=peer_u[0], peer_v=peer_v[0],
              ln_g=ln_g[0], ln_b=ln_b[0])
    xf = _layer0(xf, mod[0], p0, geom)
    p1 = dict(cd_w_in=cd_w_in[0], mlstm_gate_b=mlstm_gate_b[0], mlstm_norm_g=mlstm_norm_g[0],
              q_norm_g=q_norm_g[0], k_norm_g=k_norm_g[0], cd_w_out=cd_w_out[0],
              peer_w_q=peer_w_q[1], peer_keys=peer_keys[1], peer_u=peer_u[1], peer_v=peer_v[1],
              ln_g=ln_g[1], ln_b=ln_b[1])
    out = _layer1(xf, mod[1], p1, geom)
    return out.reshape(n_batch, seq, d)
```

```python
import functools
import math

import jax
import jax.numpy as jnp
import numpy as np
from jax import lax
from jax.experimental import pallas as pl
from jax.experimental.pallas import tpu as pltpu

F32 = jnp.float32
BF16 = jnp.bfloat16

LANES = 128
SUBLANES = 8
VMEM_LIMIT_BYTES = 56 * 1024 * 1024

GRID_W = 64
N_MOD = 6
POOL_WINDOWS = (2, 4, 8, 16)
CONV_W = 31
HALO = 16
MLSTM_HEADS = 4
MLSTM_CHUNK = 128
ATT_QH = 8
ATT_KVH = 2
ROPE_THETA = 10000.0
PEER_HEADS = 8
PEER_NKEYS = 128
PEER_TOPK = 16
LN_EPS = 1e-5
RMS_EPS = 1e-6
NEG_INF = float("-inf")

SH1, SC1, G1, SH2, SC2, G2 = range(N_MOD)


def _cparams(*sem, flags=None):
    return pltpu.CompilerParams(dimension_semantics=sem, vmem_limit_bytes=VMEM_LIMIT_BYTES, flags=flags)


def _bdot(a, b):
    return jnp.dot(a.astype(BF16), b.astype(BF16), preferred_element_type=F32)


def _bdot_nt(a, b):
    return lax.dot_general(a.astype(BF16), b.astype(BF16), (((1,), (1,)), ((), ())),
                           preferred_element_type=F32)


def _layer_norm_rows(v, g, b):
    mu = jnp.mean(v, axis=-1, keepdims=True)
    d = v - mu
    var = jnp.mean(d * d, axis=-1, keepdims=True)
    return d * lax.rsqrt(var + LN_EPS) * g + b


def _mod_kernel(c_ref, w_ref, b_ref, o_ref):
    c = c_ref[...]
    a = c * jax.nn.sigmoid(c)
    o_ref[...] = jnp.dot(a, w_ref[...], preferred_element_type=F32,
                         precision=lax.Precision.HIGHEST) + b_ref[...]


def _modulation(cc, mod_w, mod_b):
    depth, d, n = mod_w.shape
    rows = cc.shape[0]
    bn = d
    out = pl.pallas_call(
        _mod_kernel,
        grid=(depth, n // bn),
        in_specs=[
            pl.BlockSpec((rows, d), lambda l, j: (0, 0)),
            pl.BlockSpec((None, d, bn), lambda l, j: (l, 0, j)),
            pl.BlockSpec((None, 1, bn), lambda l, j: (l, 0, j)),
        ],
        out_specs=pl.BlockSpec((None, rows, bn), lambda l, j: (l, 0, j)),
        out_shape=jax.ShapeDtypeStruct((depth, rows, n), F32),
        compiler_params=_cparams("parallel", "parallel"),
        name="modulation",
    )(cc, mod_w, mod_b.reshape(depth, 1, n))
    return out.reshape(depth, rows, N_MOD, d)


def _modmm_kernel(x_ref, mod_ref, w_ref, *o_refs, splits):
    h = x_ref[...] * (1.0 + mod_ref[SC1:SC1 + 1, :]) + mod_ref[SH1:SH1 + 1, :]
    z = _bdot(h, w_ref[...])
    off = 0
    for o_ref, n in zip(o_refs, splits):
        o_ref[...] = z[:, off:off + n].astype(o_ref.dtype)
        off += n


def _modulated_matmul(x, mod_l, w, splits, out_dtypes, tm, in_tile, stream_of_tile, n_tiles):
    d = x.shape[1]
    n = w.shape[1]
    assert sum(splits) == n
    return pl.pallas_call(
        functools.partial(_modmm_kernel, splits=splits),
        grid=(n_tiles,),
        in_specs=[
            pl.BlockSpec((tm, d), lambda t: (in_tile(t), 0)),
            pl.BlockSpec((None, N_MOD, d), lambda t: (stream_of_tile(t), 0, 0)),
            pl.BlockSpec((d, n), lambda t: (0, 0)),
        ],
        out_specs=[pl.BlockSpec((tm, s), lambda t: (t, 0)) for s in splits],
        out_shape=[jax.ShapeDtypeStruct((n_tiles * tm, s), dt) for s, dt in zip(splits, out_dtypes)],
        compiler_params=_cparams("parallel"),
        name="modulated_matmul",
    )(x, mod_l, w.astype(BF16))


def _ab_tail_kernel(zp_ref, zm_ref, zn_ref, x_ref, mod_ref, pw_ref, pls_ref, cw_ref, cb_ref,
                    cg_ref, cbb_ref, wo_ref, lg_ref, lb_ref, o_ref, zpool, ubuf, ymix,
                    *, tm, d_pool, d_conv, n_lat_tiles, lat_tiles_per_seq, ctx_tiles_per_seq,
                    dn_alpha, row_chunk):
    t = pl.program_id(0)
    is_lat = t < n_lat_tiles
    per_seq = jnp.where(is_lat, lat_tiles_per_seq, ctx_tiles_per_seq)
    pos_tile = jnp.where(is_lat, t, t - n_lat_tiles) % per_seq
    first = pos_tile == 0
    last = pos_tile == per_seq - 1
    seq_len = per_seq * tm

    def glu(z):
        return z[:, d_pool:d_pool + d_conv] * jax.nn.sigmoid(z[:, d_pool + d_conv:])

    zp = jnp.where(first, 0.0, zp_ref[...])
    zn = jnp.where(last, 0.0, zn_ref[...])
    zm = zm_ref[...]
    zpool[0:HALO, :] = zp[:, :d_pool]
    zpool[HALO:HALO + tm, :] = zm[:, :d_pool]
    zpool[HALO + tm:, :] = zn[:, :d_pool]
    ubuf[0:HALO, :] = glu(zp)
    ubuf[HALO:HALO + tm, :] = glu(zm)
    ubuf[HALO + tm:, :] = glu(zn)

    gw = d_pool // len(POOL_WINDOWS)
    tpos = pos_tile * tm + lax.broadcasted_iota(jnp.int32, (tm, gw), 0)
    for g, w in enumerate(POOL_WINDOWS):
        cols = slice(g * gw, (g + 1) * gw)
        acc = zpool[HALO - w // 2:HALO - w // 2 + tm, cols]
        for s in range(1, w):
            acc = acc + zpool[HALO - w // 2 + s:HALO - w // 2 + s + tm, cols]
        cnt = jnp.minimum(tpos + w // 2, seq_len) - jnp.maximum(tpos - w // 2, 0)
        diff = acc / cnt.astype(F32) - zpool[HALO:HALO + tm, cols]
        ymix[:, cols] = (_bdot(diff, pw_ref[g]) * pls_ref[:, cols]).astype(BF16)

    half = CONV_W // 2
    for r in range(0, tm, row_chunk):
        base = HALO - half + r
        acc = ubuf[base:base + row_chunk, :] * cw_ref[0:1, :]
        for k in range(1, CONV_W):
            acc = acc + ubuf[base + k:base + k + row_chunk, :] * cw_ref[k:k + 1, :]
        yn = _layer_norm_rows(acc + cb_ref[...], cg_ref[...], cbb_ref[...])
        ymix[r:r + row_chunk, d_pool:] = (yn * jax.nn.sigmoid(yn)).astype(BF16)

    y = jnp.dot(ymix[...], wo_ref[...], preferred_element_type=F32)
    v = dn_alpha * x_ref[...] + mod_ref[G1:G1 + 1, :] * y
    o_ref[...] = _layer_norm_rows(v, lg_ref[...], lb_ref[...])


def _ab_tail(z, x, mod_l, pool_w, pool_ls, conv_w, conv_b, cln_g, cln_b, w_out, ln_g, ln_b,
             tm, n_lat_tiles, lat_tiles_per_seq, ctx_tiles_per_seq, stream_of_tile, dn_alpha):
    ntok, d = x.shape
    d_pool = pool_ls.shape[0]
    d_conv = conv_b.shape[0]
    n_tiles = ntok // tm
    hb = tm // HALO
    n_hblocks = ntok // HALO
    row = lambda a: a.reshape(1, -1)
    kern = functools.partial(
        _ab_tail_kernel, tm=tm, d_pool=d_pool, d_conv=d_conv, n_lat_tiles=n_lat_tiles,
        lat_tiles_per_seq=lat_tiles_per_seq, ctx_tiles_per_seq=ctx_tiles_per_seq,
        dn_alpha=dn_alpha, row_chunk=32)
    const2 = lambda t: (0, 0)
    return pl.pallas_call(
        kern,
        grid=(n_tiles,),
        in_specs=[
            pl.BlockSpec((HALO, z.shape[1]), lambda t: (jnp.maximum(t * hb - 1, 0), 0)),
            pl.BlockSpec((tm, z.shape[1]), lambda t: (t, 0)),
            pl.BlockSpec((HALO, z.shape[1]), lambda t: (jnp.minimum((t + 1) * hb, n_hblocks - 1), 0)),
            pl.BlockSpec((tm, d), lambda t: (t, 0)),
            pl.BlockSpec((None, N_MOD, d), lambda t: (stream_of_tile(t), 0, 0)),
            pl.BlockSpec(pool_w.shape, lambda t: (0, 0, 0)),
            pl.BlockSpec((1, d_pool), const2),
            pl.BlockSpec(conv_w.shape, const2),
            pl.BlockSpec((1, d_conv), const2),
            pl.BlockSpec((1, d_conv), const2),
            pl.BlockSpec((1, d_conv), const2),
            pl.BlockSpec(w_out.shape, const2),
            pl.BlockSpec((1, d), const2),
            pl.BlockSpec((1, d), const2),
        ],
        out_specs=pl.BlockSpec((tm, d), lambda t: (t, 0)),
        out_shape=jax.ShapeDtypeStruct((ntok, d), F32),
        scratch_shapes=[
            pltpu.VMEM((tm + 2 * HALO, d_pool), F32),
            pltpu.VMEM((tm + 2 * HALO, d_conv), F32),
            pltpu.VMEM((tm, d_pool + d_conv), BF16),
        ],
        compiler_params=_cparams("parallel"),
        name="ab_tail",
    )(z, z, z, x, mod_l, pool_w.astype(BF16), row(pool_ls), conv_w, row(conv_b), row(cln_g),
      row(cln_b), w_out.astype(BF16), row(ln_g), row(ln_b))


def _top_values(s, k):
    outs = []
    cur = s
    for it in range(k):
        m = jnp.max(cur, axis=0, keepdims=True)
        outs.append(m)
        if it + 1 < k:
            cur = jnp.where(cur >= m, NEG_INF, cur)
    return outs


def _peer_route_kernel(x_ref, mod_ref, wq_ref, keys_ref, a_ref, nd_ref, c_ref, bn_ref, qt_ref,
                       *, tm, half):
    h = x_ref[...] * (1.0 + mod_ref[SC2:SC2 + 1, :]) + mod_ref[SH2:SH2 + 1, :]
    qt_ref[...] = _bdot_nt(wq_ref[...], h).astype(BF16)
    k = PEER_TOPK
    for hd in range(PEER_HEADS):
        for lt in range(tm // LANES):
            lanes = slice(lt * LANES, (lt + 1) * LANES)
            r0 = hd * 2 * half
            s1 = jnp.dot(keys_ref[0], qt_ref[r0:r0 + half, lanes], preferred_element_type=F32)
            s2 = jnp.dot(keys_ref[1], qt_ref[r0 + half:r0 + 2 * half, lanes],
                         preferred_element_type=F32)
            t1 = _top_values(s1, k)
            t2 = _top_values(s2, k)
            t2_all = jnp.concatenate(t2, axis=0)
            t1_tail = jnp.concatenate(t1[k // 2:], axis=0)
            pieces = []
            for a in range(k // 2):
                nb = min(k, (k + 1) // (a + 1))
                pieces.append(t1[a] + t2_all[:nb, :])
            pieces.append(t1_tail + t2[0])
            n_cand = sum(p.shape[0] for p in pieces)
            pad = (-n_cand) % SUBLANES
            if pad:
                pieces.append(jnp.full((pad, LANES), NEG_INF, F32))
            cand = jnp.concatenate(pieces, axis=0)
            top = _top_values(cand, k + 1)
            kth = top[k - 1]
            thr = 0.5 * (kth + jnp.maximum(top[k], kth - 1.0))
            m1, m2 = t1[0], t2[0]
            z = jnp.sum(jnp.where(cand >= thr, jnp.exp(cand - (m1 + m2)), 0.0), axis=0, keepdims=True)
            a_ref[hd, :, lanes] = jnp.where(s1 >= t1[k - 1], jnp.exp(s1 - m1), 0.0)
            nd_ref[hd, :, lanes] = -s1
            c_ref[hd, :, lanes] = s2 - thr
            bn_ref[hd, :, lanes] = jnp.where(s2 >= t2[k - 1], jnp.exp(s2 - m2), 0.0) / z


def _peer_route(x, mod_l, w_q, keys, tm, stream_of_tile):
    ntok, d = x.shape
    nq = w_q.shape[1]
    half = keys.shape[2]
    nk = keys.shape[1]
    n_tiles = ntok // tm
    sel_shape = jax.ShapeDtypeStruct((PEER_HEADS, nk, ntok), F32)
    sel_spec = pl.BlockSpec((PEER_HEADS, nk, tm), lambda t: (0, 0, t))
    return pl.pallas_call(
        functools.partial(_peer_route_kernel, tm=tm, half=half),
        grid=(n_tiles,),
        in_specs=[
            pl.BlockSpec((tm, d), lambda t: (t, 0)),
            pl.BlockSpec((None, N_MOD, d), lambda t: (stream_of_tile(t), 0, 0)),
            pl.BlockSpec((nq, d), lambda t: (0, 0)),
            pl.BlockSpec(keys.shape, lambda t: (0, 0, 0)),
        ],
        out_specs=[sel_spec] * 4,
        out_shape=[sel_shape] * 4,
        scratch_shapes=[pltpu.VMEM((nq, tm), BF16)],
        compiler_params=_cparams("parallel"),
        name="peer_route",
    )(x, mod_l, w_q.T.astype(BF16), keys.astype(BF16))


def _gelu_exact(a):
    return 0.5 * a * (1.0 + lax.erf(a * (1.0 / math.sqrt(2.0))))


def _peer_dense_kernel(x_ref, mod_ref, a_ref, nd_ref, c_ref, bn_ref, u_ref, vt_ref, lg_ref, lb_ref,
                       o_ref, hbf, acc, wt, at, *, tm, groups, fk, nk, dn_alpha, row_chunk, rc_block):
    e = pl.program_id(1)

    @pl.when(e == 0)
    def _():
        h = x_ref[...] * (1.0 + mod_ref[SC2:SC2 + 1, :]) + mod_ref[SH2:SH2 + 1, :]
        hbf[...] = h.astype(BF16)
        acc[...] = jnp.zeros_like(acc)

    at[0] = _bdot_nt(u_ref[0], hbf[...])
    wt[1] = jnp.zeros(wt.shape[1:], BF16)

    def group_step(p, carry):
        slot = p % 2
        at[1 - slot] = _bdot_nt(u_ref[jnp.minimum(p + 1, groups - 1)], hbf[...])
        acc[...] += jnp.dot(vt_ref[jnp.maximum(p - 1, 0)], wt[1 - slot], preferred_element_type=F32)
        for lt in range(tm // LANES):
            lanes = slice(lt * LANES, (lt + 1) * LANES)
            for rb in range(0, nk // row_chunk, rc_block):
                gates = [[None] * rc_block for _ in range(fk)]
                for hd in range(PEER_HEADS):
                    a_rows = [jnp.broadcast_to(a_ref[hd, p * fk + q, :, lanes], (row_chunk, LANES))
                              for q in range(fk)]
                    nd_rows = [jnp.broadcast_to(nd_ref[hd, p * fk + q, :, lanes], (row_chunk, LANES))
                               for q in range(fk)]
                    for r in range(rc_block):
                        rows = slice((rb + r) * row_chunk, (rb + r + 1) * row_chunk)
                        cv = c_ref[hd, rows, lanes]
                        bv = bn_ref[hd, rows, lanes]
                        for q in range(fk):
                            g = jnp.where(cv >= nd_rows[q], bv * a_rows[q], 0.0)
                            gates[q][r] = g if gates[q][r] is None else gates[q][r] + g
                for q in range(fk):
                    for r in range(rc_block):
                        erows = slice(q * nk + (rb + r) * row_chunk, q * nk + (rb + r + 1) * row_chunk)
                        wt[slot, erows, lanes] = (_gelu_exact(at[slot, erows, lanes]) * gates[q][r]).astype(BF16)
        return carry

    lax.fori_loop(0, groups, group_step, 0)
    acc[...] += jnp.dot(vt_ref[groups - 1], wt[(groups - 1) % 2], preferred_element_type=F32)

    @pl.when(e == pl.num_programs(1) - 1)
    def _():
        f = acc[...].T
        v = dn_alpha * x_ref[...] + mod_ref[G2:G2 + 1, :] * f
        o_ref[...] = _layer_norm_rows(v, lg_ref[...], lb_ref[...])


PEER_FIRST_KEYS_PER_PASS = 4
PEER_GROUPS_PER_STEP = 4
PEER_ROW_CHUNKS_PER_BLOCK = 2
PEER_DENSE_FLAGS = None


def _peer_dense(x, mod_l, sel, u_tab, v_tab, ln_g, ln_b, tm, stream_of_tile, dn_alpha):
    ntok, d = x.shape
    n_exp = u_tab.shape[0]
    a_sel, nd_sel, c_sel, bn_sel = sel
    nk = c_sel.shape[1]
    n_tiles = ntok // tm
    fk, groups = PEER_FIRST_KEYS_PER_PASS, PEER_GROUPS_PER_STEP
    ge = fk * nk
    n_groups = n_exp // ge
    a_sel = a_sel.reshape(PEER_HEADS, nk, 1, ntok)
    nd_sel = nd_sel.reshape(PEER_HEADS, nk, 1, ntok)
    u3 = u_tab.astype(BF16).reshape(n_groups, ge, d)
    vt3 = v_tab.astype(BF16).reshape(n_groups, ge, d).transpose(0, 2, 1)
    row_spec = pl.BlockSpec((PEER_HEADS, groups * fk, 1, tm), lambda t, e: (0, e, 0, t))
    full_spec = pl.BlockSpec((PEER_HEADS, nk, tm), lambda t, e: (0, 0, t))
    kern = functools.partial(_peer_dense_kernel, tm=tm, groups=groups, fk=fk, nk=nk, dn_alpha=dn_alpha,
                             row_chunk=2 * SUBLANES, rc_block=PEER_ROW_CHUNKS_PER_BLOCK)
    return pl.pallas_call(
        kern,
        grid=(n_tiles, n_groups // groups),
        in_specs=[
            pl.BlockSpec((tm, d), lambda t, e: (t, 0)),
            pl.BlockSpec((None, N_MOD, d), lambda t, e: (stream_of_tile(t), 0, 0)),
            row_spec, row_spec, full_spec, full_spec,
            pl.BlockSpec((groups, ge, d), lambda t, e: (e, 0, 0)),
            pl.BlockSpec((groups, d, ge), lambda t, e: (e, 0, 0)),
            pl.BlockSpec((1, d), lambda t, e: (0, 0)),
            pl.BlockSpec((1, d), lambda t, e: (0, 0)),
        ],
        out_specs=pl.BlockSpec((tm, d), lambda t, e: (t, 0)),
        out_shape=jax.ShapeDtypeStruct((ntok, d), F32),
        scratch_shapes=[
            pltpu.VMEM((tm, d), BF16),
            pltpu.VMEM((d, tm), F32),
            pltpu.VMEM((2, ge, tm), BF16),
            pltpu.VMEM((2, ge, tm), F32),
        ],
        compiler_params=_cparams("parallel", "arbitrary", flags=PEER_DENSE_FLAGS),
        name="peer_dense",
    )(x, mod_l, a_sel, nd_sel, c_sel, bn_sel, u3, vt3, ln_g.reshape(1, d), ln_b.reshape(1, d))


def _peer_layer(x, mod_l, w_q, keys, u_tab, v_tab, ln_g, ln_b, tm, stream_of_tile, dn_alpha):
    sel = _peer_route(x, mod_l, w_q, keys, tm, stream_of_tile)
    return _peer_dense(x, mod_l, sel, u_tab, v_tab, ln_g, ln_b, tm, stream_of_tile, dn_alpha)


def _rope(v, cos, sin_signed):
    n = v.shape[-1]
    lane = lax.broadcasted_iota(jnp.int32, v.shape, v.ndim - 1)
    quarter = sin_signed.shape[-1] // 8
    partner = jnp.where(lane % (2 * quarter) < quarter,
                        pltpu.roll(v, n - quarter, v.ndim - 1), pltpu.roll(v, quarter, v.ndim - 1))
    reps = n // cos.shape[-1]
    if reps > 1:
        cos = jnp.concatenate([cos] * reps, axis=-1)
        sin_signed = jnp.concatenate([sin_signed] * reps, axis=-1)
    return v * cos + partner * sin_signed


def _cd_in_kernel(x_ref, mod_ref, w_ref, cos_ref, sin_ref, qg_ref, kg_ref,
                  qm_ref, km_ref, vm_ref, om_ref, qx_ref, ka_ref, va_ref, gt_ref,
                  *, d_m, d_qx, d_kv, att_dh, k_scale):
    h = x_ref[...] * (1.0 + mod_ref[SC1:SC1 + 1, :]) + mod_ref[SH1:SH1 + 1, :]
    z = _bdot(h, w_ref[...])
    qm_ref[...] = z[:, 0:d_m].astype(BF16)
    km_ref[...] = (z[:, d_m:2 * d_m] * k_scale).astype(BF16)
    vm_ref[...] = z[:, 2 * d_m:3 * d_m].astype(BF16)
    om_ref[...] = z[:, 3 * d_m:4 * d_m]
    off = 4 * d_m
    cos = cos_ref[...]
    sin = sin_ref[...]
    qx = z[:, off:off + d_qx]
    pieces = []
    for hq in range(d_qx // LANES):
        blk = qx[:, hq * LANES:(hq + 1) * LANES]
        ms = jnp.sum(blk * blk, axis=-1, keepdims=True) * (1.0 / att_dh)
        pieces.append(blk * lax.rsqrt(ms + RMS_EPS))
    qn = jnp.concatenate(pieces, axis=-1) * qg_ref[...]
    qx_ref[...] = _rope(qn, cos, sin).astype(BF16)
    off += d_qx
    kk = z[:, off:off + d_kv]
    lane = lax.broadcasted_iota(jnp.int32, kk.shape, 1)
    sq = kk * kk
    pieces = []
    for hk in range(d_kv // att_dh):
        sel = (lane >= hk * att_dh) & (lane < (hk + 1) * att_dh)
        ms = jnp.sum(jnp.where(sel, sq, 0.0), axis=-1, keepdims=True) * (1.0 / att_dh)
        pieces.append((sel, lax.rsqrt(ms + RMS_EPS)))
    scale = jnp.zeros_like(kk)
    for sel, r in pieces:
        scale = jnp.where(sel, r, scale)
    ka_ref[...] = _rope(kk * scale * kg_ref[...], cos, sin).astype(BF16)
    off += d_kv
    va_ref[...] = z[:, off:off + d_kv].astype(BF16)
    off += d_kv
    gt_ref[...] = z[:, off:]


def _cd_in(x, mod_l, w, cos_t, sin_t, q_gain, k_gain, tm, n_tiles, in_tile, stream_of_tile, pos_tile,
           d_m, d_qx, d_kv, att_dh, k_scale):
    d = x.shape[1]
    n = w.shape[1]
    ntok = n_tiles * tm
    widths = (d_m, d_m, d_m, d_m, d_qx, d_kv, d_kv, n - 4 * d_m - d_qx - 2 * d_kv)
    dtypes = (BF16, BF16, BF16, F32, BF16, BF16, BF16, F32)
    kern = functools.partial(_cd_in_kernel, d_m=d_m, d_qx=d_qx, d_kv=d_kv, att_dh=att_dh, k_scale=k_scale)
    return pl.pallas_call(
        kern,
        grid=(n_tiles,),
        in_specs=[
            pl.BlockSpec((tm, d), lambda t: (in_tile(t), 0)),
            pl.BlockSpec((None, N_MOD, d), lambda t: (stream_of_tile(t), 0, 0)),
            pl.BlockSpec((d, n), lambda t: (0, 0)),
            pl.BlockSpec((tm, cos_t.shape[1]), lambda t: (pos_tile(t), 0)),
            pl.BlockSpec((tm, sin_t.shape[1]), lambda t: (pos_tile(t), 0)),
            pl.BlockSpec((1, d_qx), lambda t: (0, 0)),
            pl.BlockSpec((1, d_kv), lambda t: (0, 0)),
        ],
        out_specs=[pl.BlockSpec((tm, wd), lambda t: (t, 0)) for wd in widths],
        out_shape=[jax.ShapeDtypeStruct((ntok, wd), dt) for wd, dt in zip(widths, dtypes)],
        compiler_params=_cparams("parallel"),
        name="cd_in",
    )(x, mod_l, w.astype(BF16), cos_t, sin_t, q_gain, k_gain)


def _mlstm_kernel(qf_ref, kf_ref, vf_ref, gf_ref, qb_ref, kb_ref, vb_ref, gb_ref, bias_ref,
                  hf_ref, hb_ref, c_s, n_s, m_s, *, chunk, dh, heads):
    s = pl.program_id(1)

    @pl.when(s == 0)
    def _():
        c_s[...] = jnp.zeros_like(c_s)
        n_s[...] = jnp.zeros_like(n_s)
        m_s[...] = jnp.zeros_like(m_s)

    ri = lax.broadcasted_iota(jnp.int32, (chunk, chunk), 0)
    cj = lax.broadcasted_iota(jnp.int32, (chunk, chunk), 1)
    streams = ((qf_ref, kf_ref, vf_ref, gf_ref, hf_ref), (qb_ref, kb_ref, vb_ref, gb_ref, hb_ref))
    for direction, (q_ref, k_ref, v_ref, g_ref, h_ref) in enumerate(streams):
        mask = (cj <= ri) if direction == 0 else (cj >= ri)
        edge = chunk - 1 if direction == 0 else 0
        gates = g_ref[...] + bias_ref[...]
        logf = jax.nn.log_sigmoid(gates)
        bc = jnp.dot(mask.astype(F32), logf, preferred_element_type=F32, precision=lax.Precision.HIGHEST)
        br = bc.T
        gr = gates.T
        for hd in range(heads):
            ci = (2 * direction) * heads + hd
            cf = (2 * direction + 1) * heads + hd
            idx = direction * heads + hd
            b_col = bc[:, cf:cf + 1]
            b_row = br[cf:cf + 1, :]
            ig_row = gr[ci:ci + 1, :]
            ig_col = gates[:, ci:ci + 1]
            m_prev = m_s[idx][:, 0:1]
            dmat = jnp.where(mask, b_col - b_row + ig_row, NEG_INF)
            inter = b_col + m_prev
            m_t = jnp.maximum(inter, jnp.max(dmat, axis=-1, keepdims=True))
            dexp = jnp.exp(dmat - m_t)
            w_inter = jnp.exp(inter - m_t)
            cols = slice(hd * dh, (hd + 1) * dh)
            q = q_ref[:, cols]
            k = k_ref[:, cols]
            v = v_ref[:, cols]
            c_prev = c_s[idx]
            n_prev = n_s[idx]
            sm = _bdot_nt(q, k) * dexp
            num = _bdot(sm, v) + w_inter * _bdot(q, c_prev)
            den = (jnp.sum(sm, axis=-1, keepdims=True)
                   + w_inter * jnp.sum(q.astype(F32) * n_prev, axis=-1, keepdims=True))
            h_ref[:, cols] = num / jnp.maximum(jnp.abs(den), jnp.exp(-m_t))
            b_last = bc[edge:edge + 1, cf:cf + 1]
            g_log = b_last - b_col + ig_col
            m_new = jnp.maximum(b_last + m_prev, jnp.max(g_log, axis=0, keepdims=True))
            wk = jnp.exp(g_log - m_new)
            decay = jnp.exp(b_last + m_prev - m_new)
            kw = k.astype(F32) * wk
            c_s[idx] = decay * c_prev + lax.dot_general(
                kw.astype(BF16), v, (((0,), (0,)), ((), ())), preferred_element_type=F32)
            n_s[idx] = decay * n_prev + jnp.sum(kw, axis=0, keepdims=True)
            m_s[idx] = jnp.broadcast_to(m_new, (1, LANES))


def _mlstm(qm, km, vm, gt, gate_bias, n_batch, chunks_per_seq, ctx_chunks, heads, dh):
    chunk = MLSTM_CHUNK
    ntok, dm = qm.shape
    ng = gt.shape[1]

    def fwd(b, s):
        return (b * chunks_per_seq + s, 0)

    def bwd(b, s):
        r = jnp.where(s < ctx_chunks, ctx_chunks - 1 - s, ctx_chunks + chunks_per_seq - 1 - s)
        return (b * chunks_per_seq + r, 0)

    tok_f = pl.BlockSpec((chunk, dm), fwd)
    tok_b = pl.BlockSpec((chunk, dm), bwd)
    kern = functools.partial(_mlstm_kernel, chunk=chunk, dh=dh, heads=heads)
    return pl.pallas_call(
        kern,
        grid=(n_batch, chunks_per_seq),
        in_specs=[tok_f, tok_f, tok_f, pl.BlockSpec((chunk, ng), fwd),
                  tok_b, tok_b, tok_b, pl.BlockSpec((chunk, ng), bwd),
                  pl.BlockSpec((1, ng), lambda b, s: (0, 0))],
        out_specs=[pl.BlockSpec((chunk, dm), fwd), pl.BlockSpec((chunk, dm), bwd)],
        out_shape=[jax.ShapeDtypeStruct((ntok, dm), F32)] * 2,
        scratch_shapes=[
            pltpu.VMEM((2 * heads, dh, dh), F32),
            pltpu.VMEM((2 * heads, 1, dh), F32),
            pltpu.VMEM((2 * heads, 1, LANES), F32),
        ],
        compiler_params=_cparams("parallel", "arbitrary"),
        name="mlstm",
    )(qm, km, vm, gt, qm, km, vm, gt, gate_bias)


def _attn_kernel(q_ref, k_ref, v_ref, o_ref, *, n_qh):
    k = k_ref[...]
    v = v_ref[...]
    for hq in range(n_qh):
        cols = slice(hq * LANES, (hq + 1) * LANES)
        s = _bdot_nt(q_ref[:, cols], k)
        p = jnp.exp(s - jnp.max(s, axis=-1, keepdims=True))
        l = jnp.sum(p, axis=-1, keepdims=True)
        o = jnp.dot(p.astype(BF16), v, preferred_element_type=F32)
        o_ref[:, cols] = (o / l).astype(o_ref.dtype)


def _attention(qx, ka, va, tq, n_batch, seq_tiles, q_tile_of, kv_len):
    n_qh = qx.shape[1] // LANES
    return pl.pallas_call(
        functools.partial(_attn_kernel, n_qh=n_qh),
        grid=(n_batch, seq_tiles),
        in_specs=[
            pl.BlockSpec((tq, qx.shape[1]), lambda b, t: (q_tile_of(b, t), 0)),
            pl.BlockSpec((kv_len, ka.shape[1]), lambda b, t: (b, 0)),
            pl.BlockSpec((kv_len, va.shape[1]), lambda b, t: (b, 0)),
        ],
        out_specs=pl.BlockSpec((tq, qx.shape[1]), lambda b, t: (b * seq_tiles + t, 0)),
        out_shape=jax.ShapeDtypeStruct((n_batch * seq_tiles * tq, qx.shape[1]), BF16),
        compiler_params=_cparams("parallel", "parallel"),
        name="attention",
    )(qx, ka, va)


def _cd_tail_kernel(hf_ref, hb_ref, om_ref, att_ref, x_ref, mod_ref, ng_ref, wm_ref, wa_ref, lg_ref, lb_ref,
                    o_ref, *, dh, heads, dn_alpha):
    hsum = hf_ref[...] + hb_ref[...]
    pieces = []
    for hd in range(heads):
        cols = slice(hd * dh, (hd + 1) * dh)
        blk = hsum[:, cols]
        mu = jnp.mean(blk, axis=-1, keepdims=True)
        dlt = blk - mu
        var = jnp.mean(dlt * dlt, axis=-1, keepdims=True)
        pieces.append(dlt * lax.rsqrt(var + LN_EPS))
    hn = jnp.concatenate(pieces, axis=-1) * ng_ref[...] * jax.nn.sigmoid(om_ref[...])
    y = _bdot(hn, wm_ref[...]) + jnp.dot(att_ref[...], wa_ref[...], preferred_element_type=F32)
    v = dn_alpha * x_ref[...] + mod_ref[G1:G1 + 1, :] * y
    o_ref[...] = _layer_norm_rows(v, lg_ref[...], lb_ref[...])


def _cd_tail(hf, hb, om, att, x, mod_l, norm_g, w_m, w_a, ln_g, ln_b, tm, n_tiles, seq_tile_of,
             stream_of_tile, heads, dh, dn_alpha):
    d = x.shape[1]
    dm = hf.shape[1]
    const2 = lambda t: (0, 0)
    kern = functools.partial(_cd_tail_kernel, dh=dh, heads=heads, dn_alpha=dn_alpha)
    return pl.pallas_call(
        kern,
        grid=(n_tiles,),
        in_specs=[
            pl.BlockSpec((tm, dm), lambda t: (seq_tile_of(t), 0)),
            pl.BlockSpec((tm, dm), lambda t: (seq_tile_of(t), 0)),
            pl.BlockSpec((tm, dm), lambda t: (seq_tile_of(t), 0)),
            pl.BlockSpec((tm, att.shape[1]), lambda t: (t, 0)),
            pl.BlockSpec((tm, d), lambda t: (t, 0)),
            pl.BlockSpec((None, N_MOD, d), lambda t: (stream_of_tile(t), 0, 0)),
            pl.BlockSpec((1, dm), const2),
            pl.BlockSpec(w_m.shape, const2),
            pl.BlockSpec(w_a.shape, const2),
            pl.BlockSpec((1, d), const2),
            pl.BlockSpec((1, d), const2),
        ],
        out_specs=pl.BlockSpec((tm, d), lambda t: (t, 0)),
        out_shape=jax.ShapeDtypeStruct((n_tiles * tm, d), F32),
        compiler_params=_cparams("parallel"),
        name="cd_tail",
    )(hf, hb, om, att, x, mod_l, norm_g.reshape(1, dm), w_m.astype(BF16), w_a.astype(BF16),
      ln_g.reshape(1, d), ln_b.reshape(1, d))


def _layer0(xf, mod_l, p, geom):
    tm = geom["tm_mix"]
    n_tiles = xf.shape[0] // tm
    n_lat_tiles = geom["n_lat"] // tm
    lat_per_seq = geom["seq"] // tm
    ctx_per_seq = geom["ctx"] // tm
    n_batch = geom["batch"]

    def stream(t):
        return jnp.where(t < n_lat_tiles, t // lat_per_seq, n_batch)

    (z,) = _modulated_matmul(xf, mod_l, p["ab_w_in"], (p["ab_w_in"].shape[1],), (F32,), tm,
                             lambda t: t, stream, n_tiles)
    x1 = _ab_tail(z, xf, mod_l, p["pool_w"], p["pool_ls"], p["conv_w"], p["conv_b"], p["conv_ln_g"],
                  p["conv_ln_b"], p["ab_w_out"], p["ln_g"][0], p["ln_b"][0], tm, n_lat_tiles,
                  lat_per_seq, ctx_per_seq, stream, geom["dn_alpha"])
    tp = geom["tm_peer"]
    n_lat_p = geom["n_lat"] // tp
    lat_per_seq_p = geom["seq"] // tp

    def stream_p(t):
        return jnp.where(t < n_lat_p, t // lat_per_seq_p, n_batch)

    return _peer_layer(x1, mod_l, p["peer_w_q"], p["peer_keys"], p["peer_u"], p["peer_v"],
                       p["ln_g"][1], p["ln_b"][1], tp, stream_p, geom["dn_alpha"])


def _rope_tables(seq, ctx_len, att_dh):
    n_freq = att_dh // 4
    t = jnp.arange(seq)
    freqs = ROPE_THETA ** (-jnp.arange(n_freq, dtype=F32) / n_freq)
    ar = (t // GRID_W).astype(F32)[:, None] * freqs
    ac = (t % GRID_W).astype(F32)[:, None] * freqs
    cos = jnp.concatenate([jnp.cos(ar), jnp.cos(ar), jnp.cos(ac), jnp.cos(ac)], axis=-1)
    sin = jnp.concatenate([-jnp.sin(ar), jnp.sin(ar), -jnp.sin(ac), jnp.sin(ac)], axis=-1)
    cos = jnp.concatenate([jnp.ones((ctx_len, att_dh), F32), cos], axis=0)
    sin = jnp.concatenate([jnp.zeros((ctx_len, att_dh), F32), sin], axis=0)
    reps = LANES // att_dh
    return jnp.tile(cos, (1, reps)), jnp.tile(sin, (1, reps))


def _layer1(xf, mod_l, p, geom):
    tm = geom["tm_mix"]
    n_batch, seq, ctx_len = geom["batch"], geom["seq"], geom["ctx"]
    d = xf.shape[1]
    heads = MLSTM_HEADS
    d_m = p["mlstm_norm_g"].shape[0]
    dh = d_m // heads
    att_dh = p["q_norm_g"].shape[0]
    n_qh, n_kvh = ATT_QH, ATT_KVH
    group = n_qh // n_kvh
    d_q, d_kv = n_qh * att_dh, n_kvh * att_dh
    n_gate = 4 * heads
    ctx_tiles, lat_tiles = ctx_len // tm, seq // tm
    per_b = ctx_tiles + lat_tiles
    n_lat_tiles = n_batch * lat_tiles

    w = p["cd_w_in"]
    cuts = np.cumsum([0, d_m, d_m, d_m, d_m, n_gate, d_q, d_kv, d_kv])
    w_qm, w_km, w_vm, w_om, w_gt, w_qa, w_ka, w_va = (w[:, cuts[i]:cuts[i + 1]] for i in range(8))
    w_qx = jnp.zeros((d, n_qh, LANES), F32)
    q_gain = jnp.zeros((n_qh, LANES), F32)
    w_att = jnp.zeros((n_qh, LANES, d), F32)
    for hq in range(n_qh):
        lo = (hq // group) * att_dh
        w_qx = w_qx.at[:, hq, lo:lo + att_dh].set(w_qa[:, hq * att_dh:(hq + 1) * att_dh])
        q_gain = q_gain.at[hq, lo:lo + att_dh].set(p["q_norm_g"] * att_dh ** -0.5)
        w_att = w_att.at[hq, lo:lo + att_dh, :].set(p["cd_w_out"][d_m + hq * att_dh:d_m + (hq + 1) * att_dh])
    d_qx = n_qh * LANES
    w_all = jnp.concatenate([w_qm, w_km, w_vm, w_om, w_qx.reshape(d, d_qx), w_ka, w_va,
                             jnp.pad(w_gt, ((0, 0), (0, LANES - n_gate)))], axis=1)
    gate_bias = jnp.pad(p["mlstm_gate_b"].reshape(1, n_gate), ((0, 0), (0, LANES - n_gate)))
    cos_t, sin_t = _rope_tables(seq, ctx_len, att_dh)

    def in_tile(t):
        b, r = t // per_b, t % per_b
        return jnp.where(r < ctx_tiles, n_lat_tiles + b * ctx_tiles + r, b * lat_tiles + r - ctx_tiles)

    def stream_seq(t):
        return jnp.where(t % per_b < ctx_tiles, n_batch, t // per_b)

    qm, km, vm, om, qx, ka, va, gt = _cd_in(
        xf, mod_l, w_all, cos_t, sin_t, q_gain.reshape(1, d_qx), jnp.tile(p["k_norm_g"], n_kvh).reshape(1, d_kv),
        tm, n_batch * per_b, in_tile, stream_seq, lambda t: t % per_b, d_m, d_qx, d_kv, att_dh, dh ** -0.5)

    chunks_per_seq = (ctx_len + seq) // MLSTM_CHUNK
    hf, hb = _mlstm(qm, km, vm, gt, gate_bias, n_batch, chunks_per_seq, ctx_len // MLSTM_CHUNK, heads, dh)
    att = _attention(qx, ka, va, tm, n_batch, lat_tiles, lambda b, t: b * per_b + ctx_tiles + t,
                     ctx_len + seq)

    def seq_tile_of(t):
        return (t // lat_tiles) * per_b + ctx_tiles + t % lat_tiles

    x1 = _cd_tail(hf, hb, om, att, xf, mod_l, p["mlstm_norm_g"], p["cd_w_out"][:d_m], w_att.reshape(d_qx, d),
                  p["ln_g"][0], p["ln_b"][0], tm, n_lat_tiles, seq_tile_of, lambda t: t // lat_tiles,
                  heads, dh, geom["dn_alpha"])
    tp = geom["tm_peer"]
    lat_per_seq_p = seq // tp
    return _peer_layer(x1, mod_l, p["peer_w_q"], p["peer_keys"], p["peer_u"], p["peer_v"],
                       p["ln_g"][1], p["ln_b"][1], tp, lambda t: t // lat_per_seq_p, geom["dn_alpha"])


def kernel(x, c, ctx, c_ctx, mod_w, mod_b, ln_g, ln_b, ab_w_in, pool_w, pool_ls, conv_w, conv_b,
           conv_ln_g, conv_ln_b, ab_w_out, cd_w_in, mlstm_gate_b, mlstm_norm_g, q_norm_g, k_norm_g,
           cd_w_out, peer_w_q, peer_keys, peer_u, peer_v):
    n_batch, seq, d = x.shape
    ctx_len = ctx.shape[1]
    depth = mod_w.shape[0]
    assert depth == 2, "one pooling/convolution layer followed by one mLSTM/attention layer"
    n_streams = SUBLANES * (-(-(n_batch + 1) // SUBLANES))
    cc = jnp.concatenate([c, c_ctx[None], jnp.zeros((n_streams - n_batch - 1, d), F32)], axis=0)
    mod = _modulation(cc, mod_w, mod_b)
    geom = dict(tm_mix=256, tm_peer=512, n_lat=n_batch * seq, seq=seq, ctx=ctx_len, batch=n_batch,
                dn_alpha=(2 * depth) ** 0.25)
    xf = jnp.concatenate([x.reshape(-1, d), ctx.reshape(-1, d)], axis=0)
    p0 = dict(ab_w_in=ab_w_in[0], pool_w=pool_w[0], pool_ls=pool_ls[0], conv_w=conv_w[0],
              conv_b=conv_b[0], conv_ln_g=conv_ln_g[0], conv_ln_b=conv_ln_b[0], ab_w_out=ab_w_out[0],
              peer_w_q=peer_w_q[0], peer_keys=peer_keys[0], peer_u=peer_u[0], peer_v=peer_v[0],
              ln_g=ln_g[0], ln_b=ln_b[0])
    xf = _layer0(xf, mod[0], p0, geom)
    p1 = dict(cd_w_in=cd_w_in[0], mlstm_gate_b=mlstm_gate_b[0], mlstm_norm_g=mlstm_norm_g[0],
              q_norm_g=q_norm_g[0], k_norm_g=k_norm_g[0], cd_w_out=cd_w_out[0],
              peer_w_q=peer_w_q[1], peer_keys=peer_keys[1], peer_u=peer_u[1], peer_v=peer_v[1],
              ln_g=ln_g[1], ln_b=ln_b[1])
    out = _layer1(xf, mod[1], p1, geom)
    return out.reshape(n_batch, seq, d)
```

```python
import functools
import math

import jax
import jax.numpy as jnp
import numpy as np
from jax import lax
from jax.experimental import pallas as pl
from jax.experimental.pallas import tpu as pltpu

F32 = jnp.float32
BF16 = jnp.bfloat16

LANES = 128
SUBLANES = 8
VMEM_LIMIT_BYTES = 56 * 1024 * 1024

GRID_W = 64
N_MOD = 6
POOL_WINDOWS = (2, 4, 8, 16)
CONV_W = 31
HALO = 16
MLSTM_HEADS = 4
MLSTM_CHUNK = 128
ATT_QH = 8
ATT_KVH = 2
ROPE_THETA = 10000.0
PEER_HEADS = 8
PEER_NKEYS = 128
PEER_TOPK = 16
LN_EPS = 1e-5
RMS_EPS = 1e-6
NEG_INF = float("-inf")

SH1, SC1, G1, SH2, SC2, G2 = range(N_MOD)


def _cparams(*sem, flags=None):
    return pltpu.CompilerParams(dimension_semantics=sem, vmem_limit_bytes=VMEM_LIMIT_BYTES, flags=flags)


def _bdot(a, b):
    return jnp.dot(a.astype(BF16), b.astype(BF16), preferred_element_type=F32)


def _bdot_nt(a, b):
    return lax.dot_general(a.astype(BF16), b.astype(BF16), (((1,), (1,)), ((), ())),
                           preferred_element_type=F32)


def _layer_norm_rows(v, g, b):
    mu = jnp.mean(v, axis=-1, keepdims=True)
    d = v - mu
    var = jnp.mean(d * d, axis=-1, keepdims=True)
    return d * lax.rsqrt(var + LN_EPS) * g + b


def _mod_kernel(c_ref, w_ref, b_ref, o_ref):
    c = c_ref[...]
    a = c * jax.nn.sigmoid(c)
    o_ref[...] = jnp.dot(a, w_ref[...], preferred_element_type=F32,
                         precision=lax.Precision.HIGHEST) + b_ref[...]


def _modulation(cc, mod_w, mod_b):
    depth, d, n = mod_w.shape
    rows = cc.shape[0]
    bn = d
    out = pl.pallas_call(
        _mod_kernel,
        grid=(depth, n // bn),
        in_specs=[
            pl.BlockSpec((rows, d), lambda l, j: (0, 0)),
            pl.BlockSpec((None, d, bn), lambda l, j: (l, 0, j)),
            pl.BlockSpec((None, 1, bn), lambda l, j: (l, 0, j)),
        ],
        out_specs=pl.BlockSpec((None, rows, bn), lambda l, j: (l, 0, j)),
        out_shape=jax.ShapeDtypeStruct((depth, rows, n), F32),
        compiler_params=_cparams("parallel", "parallel"),
        name="modulation",
    )(cc, mod_w, mod_b.reshape(depth, 1, n))
    return out.reshape(depth, rows, N_MOD, d)


def _modmm_kernel(x_ref, mod_ref, w_ref, *o_refs, splits):
    h = x_ref[...] * (1.0 + mod_ref[SC1:SC1 + 1, :]) + mod_ref[SH1:SH1 + 1, :]
    z = _bdot(h, w_ref[...])
    off = 0
    for o_ref, n in zip(o_refs, splits):
        o_ref[...] = z[:, off:off + n].astype(o_ref.dtype)
        off += n


def _modulated_matmul(x, mod_l, w, splits, out_dtypes, tm, in_tile, stream_of_tile, n_tiles):
    d = x.shape[1]
    n = w.shape[1]
    assert sum(splits) == n
    return pl.pallas_call(
        functools.partial(_modmm_kernel, splits=splits),
        grid=(n_tiles,),
        in_specs=[
            pl.BlockSpec((tm, d), lambda t: (in_tile(t), 0)),
            pl.BlockSpec((None, N_MOD, d), lambda t: (stream_of_tile(t), 0, 0)),
            pl.BlockSpec((d, n), lambda t: (0, 0)),
        ],
        out_specs=[pl.BlockSpec((tm, s), lambda t: (t, 0)) for s in splits],
        out_shape=[jax.ShapeDtypeStruct((n_tiles * tm, s), dt) for s, dt in zip(splits, out_dtypes)],
        compiler_params=_cparams("parallel"),
        name="modulated_matmul",
    )(x, mod_l, w.astype(BF16))


def _ab_tail_kernel(zp_ref, zm_ref, zn_ref, x_ref, mod_ref, pw_ref, pls_ref, cw_ref, cb_ref,
                    cg_ref, cbb_ref, wo_ref, lg_ref, lb_ref, o_ref, zpool, ubuf, ymix,
                    *, tm, d_pool, d_conv, n_lat_tiles, lat_tiles_per_seq, ctx_tiles_per_seq,
                    dn_alpha, row_chunk):
    t = pl.program_id(0)
    is_lat = t < n_lat_tiles
    per_seq = jnp.where(is_lat, lat_tiles_per_seq, ctx_tiles_per_seq)
    pos_tile = jnp.where(is_lat, t, t - n_lat_tiles) % per_seq
    first = pos_tile == 0
    last = pos_tile == per_seq - 1
    seq_len = per_seq * tm

    def glu(z):
        return z[:, d_pool:d_pool + d_conv] * jax.nn.sigmoid(z[:, d_pool + d_conv:])

    zp = jnp.where(first, 0.0, zp_ref[...])
    zn = jnp.where(last, 0.0, zn_ref[...])
    zm = zm_ref[...]
    zpool[0:HALO, :] = zp[:, :d_pool]
    zpool[HALO:HALO + tm, :] = zm[:, :d_pool]
    zpool[HALO + tm:, :] = zn[:, :d_pool]
    ubuf[0:HALO, :] = glu(zp)
    ubuf[HALO:HALO + tm, :] = glu(zm)
    ubuf[HALO + tm:, :] = glu(zn)

    gw = d_pool // len(POOL_WINDOWS)
    tpos = pos_tile * tm + lax.broadcasted_iota(jnp.int32, (tm, gw), 0)
    for g, w in enumerate(POOL_WINDOWS):
        cols = slice(g * gw, (g + 1) * gw)
        acc = zpool[HALO - w // 2:HALO - w // 2 + tm, cols]
        for s in range(1, w):
            acc = acc + zpool[HALO - w // 2 + s:HALO - w // 2 + s + tm, cols]
        cnt = jnp.minimum(tpos + w // 2, seq_len) - jnp.maximum(tpos - w // 2, 0)
        diff = acc / cnt.astype(F32) - zpool[HALO:HALO + tm, cols]
        ymix[:, cols] = (_bdot(diff, pw_ref[g]) * pls_ref[:, cols]).astype(BF16)

    half = CONV_W // 2
    for r in range(0, tm, row_chunk):
        base = HALO - half + r
        acc = ubuf[base:base + row_chunk, :] * cw_ref[0:1, :]
        for k in range(1, CONV_W):
            acc = acc + ubuf[base + k:base + k + row_chunk, :] * cw_ref[k:k + 1, :]
        yn = _layer_norm_rows(acc + cb_ref[...], cg_ref[...], cbb_ref[...])
        ymix[r:r + row_chunk, d_pool:] = (yn * jax.nn.sigmoid(yn)).astype(BF16)

    y = jnp.dot(ymix[...], wo_ref[...], preferred_element_type=F32)
    v = dn_alpha * x_ref[...] + mod_ref[G1:G1 + 1, :] * y
    o_ref[...] = _layer_norm_rows(v, lg_ref[...], lb_ref[...])


def _ab_tail(z, x, mod_l, pool_w, pool_ls, conv_w, conv_b, cln_g, cln_b, w_out, ln_g, ln_b,
             tm, n_lat_tiles, lat_tiles_per_seq, ctx_tiles_per_seq, stream_of_tile, dn_alpha):
    ntok, d = x.shape
    d_pool = pool_ls.shape[0]
    d_conv = conv_b.shape[0]
    n_tiles = ntok // tm
    hb = tm // HALO
    n_hblocks = ntok // HALO
    row = lambda a: a.reshape(1, -1)
    kern = functools.partial(
        _ab_tail_kernel, tm=tm, d_pool=d_pool, d_conv=d_conv, n_lat_tiles=n_lat_tiles,
        lat_tiles_per_seq=lat_tiles_per_seq, ctx_tiles_per_seq=ctx_tiles_per_seq,
        dn_alpha=dn_alpha, row_chunk=32)
    const2 = lambda t: (0, 0)
    return pl.pallas_call(
        kern,
        grid=(n_tiles,),
        in_specs=[
            pl.BlockSpec((HALO, z.shape[1]), lambda t: (jnp.maximum(t * hb - 1, 0), 0)),
            pl.BlockSpec((tm, z.shape[1]), lambda t: (t, 0)),
            pl.BlockSpec((HALO, z.shape[1]), lambda t: (jnp.minimum((t + 1) * hb, n_hblocks - 1), 0)),
            pl.BlockSpec((tm, d), lambda t: (t, 0)),
            pl.BlockSpec((None, N_MOD, d), lambda t: (stream_of_tile(t), 0, 0)),
            pl.BlockSpec(pool_w.shape, lambda t: (0, 0, 0)),
            pl.BlockSpec((1, d_pool), const2),
            pl.BlockSpec(conv_w.shape, const2),
            pl.BlockSpec((1, d_conv), const2),
            pl.BlockSpec((1, d_conv), const2),
            pl.BlockSpec((1, d_conv), const2),
            pl.BlockSpec(w_out.shape, const2),
            pl.BlockSpec((1, d), const2),
            pl.BlockSpec((1, d), const2),
        ],
        out_specs=pl.BlockSpec((tm, d), lambda t: (t, 0)),
        out_shape=jax.ShapeDtypeStruct((ntok, d), F32),
        scratch_shapes=[
            pltpu.VMEM((tm + 2 * HALO, d_pool), F32),
            pltpu.VMEM((tm + 2 * HALO, d_conv), F32),
            pltpu.VMEM((tm, d_pool + d_conv), BF16),
        ],
        compiler_params=_cparams("parallel"),
        name="ab_tail",
    )(z, z, z, x, mod_l, pool_w.astype(BF16), row(pool_ls), conv_w, row(conv_b), row(cln_g),
      row(cln_b), w_out.astype(BF16), row(ln_g), row(ln_b))


def _paired_lanes(lt, tm):
    shifted = (lt + 1) % (tm // LANES)
    return slice(shifted * LANES, (shifted + 1) * LANES)


def _top_values(s, k):
    outs = []
    cur = s
    for it in range(k):
        m = jnp.max(cur, axis=0, keepdims=True)
        outs.append(m)
        if it + 1 < k:
            cur = jnp.where(cur >= m, NEG_INF, cur)
    return outs


def _peer_route_kernel(x_ref, mod_ref, wq_ref, keys_ref, a_ref, nd_ref, c_ref, bn_ref, qt_ref,
                       *, tm, half):
    h = x_ref[...] * (1.0 + mod_ref[SC2:SC2 + 1, :]) + mod_ref[SH2:SH2 + 1, :]
    qt_ref[...] = _bdot_nt(wq_ref[...], h).astype(BF16)
    k = PEER_TOPK
    for hd in range(PEER_HEADS):
        for lt in range(tm // LANES):
            lanes = slice(lt * LANES, (lt + 1) * LANES)
            r0 = hd * 2 * half
            s1 = jnp.dot(keys_ref[0], qt_ref[r0:r0 + half, lanes], preferred_element_type=F32)
            s2 = jnp.dot(keys_ref[1], qt_ref[r0 + half:r0 + 2 * half, lanes],
                         preferred_element_type=F32)
            t1 = _top_values(s1, k)
            t2 = _top_values(s2, k)
            t2_all = jnp.concatenate(t2, axis=0)
            t1_tail = jnp.concatenate(t1[k // 2:], axis=0)
            pieces = []
            for a in range(k // 2):
                nb = min(k, (k + 1) // (a + 1))
                pieces.append(t1[a] + t2_all[:nb, :])
            pieces.append(t1_tail + t2[0])
            n_cand = sum(p.shape[0] for p in pieces)
            pad = (-n_cand) % SUBLANES
            if pad:
                pieces.append(jnp.full((pad, LANES), NEG_INF, F32))
            cand = jnp.concatenate(pieces, axis=0)
            top = _top_values(cand, k + 1)
            kth = top[k - 1]
            thr = 0.5 * (kth + jnp.maximum(top[k], kth - 1.0))
            m1, m2 = t1[0], t2[0]
            z = jnp.sum(jnp.where(cand >= thr, jnp.exp(cand - (m1 + m2)), 0.0), axis=0, keepdims=True)
            a_ref[hd, :, lanes] = jnp.where(s1 >= t1[k - 1], jnp.exp(s1 - m1), 0.0)
            nd_ref[hd, :, lanes] = -s1
            c_ref[hd, :, lanes] = s2 - thr
            bn_ref[hd, :, _paired_lanes(lt, tm)] = jnp.where(s2 >= t2[k - 1], jnp.exp(s2 - m2), 0.0) / z


def _peer_route(x, mod_l, w_q, keys, tm, stream_of_tile):
    ntok, d = x.shape
    nq = w_q.shape[1]
    half = keys.shape[2]
    nk = keys.shape[1]
    n_tiles = ntok // tm
    sel_shape = jax.ShapeDtypeStruct((PEER_HEADS, nk, ntok), F32)
    sel_spec = pl.BlockSpec((PEER_HEADS, nk, tm), lambda t: (0, 0, t))
    return pl.pallas_call(
        functools.partial(_peer_route_kernel, tm=tm, half=half),
        grid=(n_tiles,),
        in_specs=[
            pl.BlockSpec((tm, d), lambda t: (t, 0)),
            pl.BlockSpec((None, N_MOD, d), lambda t: (stream_of_tile(t), 0, 0)),
            pl.BlockSpec((nq, d), lambda t: (0, 0)),
            pl.BlockSpec(keys.shape, lambda t: (0, 0, 0)),
        ],
        out_specs=[sel_spec] * 4,
        out_shape=[sel_shape] * 4,
        scratch_shapes=[pltpu.VMEM((nq, tm), BF16)],
        compiler_params=_cparams("parallel"),
        name="peer_route",
    )(x, mod_l, w_q.T.astype(BF16), keys.astype(BF16))


def _gelu_exact(a):
    return 0.5 * a * (1.0 + lax.erf(a * (1.0 / math.sqrt(2.0))))


def _peer_dense_kernel(x_ref, mod_ref, a_ref, nd_ref, c_ref, bn_ref, u_ref, vt_ref, lg_ref, lb_ref,
                       o_ref, hbf, acc, wt, at, *, tm, groups, fk, nk, dn_alpha, row_chunk, rc_block):
    e = pl.program_id(1)

    @pl.when(e == 0)
    def _():
        h = x_ref[...] * (1.0 + mod_ref[SC2:SC2 + 1, :]) + mod_ref[SH2:SH2 + 1, :]
        hbf[...] = h.T.astype(BF16)
        acc[...] = jnp.zeros_like(acc)

    at[0] = jnp.dot(u_ref[0], hbf[...], preferred_element_type=F32)
    wt[1] = jnp.zeros(wt.shape[1:], BF16)

    def group_step(p, carry):
        slot = p % 2
        at[1 - slot] = jnp.dot(u_ref[jnp.minimum(p + 1, groups - 1)], hbf[...], preferred_element_type=F32)
        acc[...] += jnp.dot(vt_ref[jnp.maximum(p - 1, 0)], wt[1 - slot], preferred_element_type=F32)
        for lt in range(tm // LANES):
            lanes = slice(lt * LANES, (lt + 1) * LANES)
            for rb in range(0, nk // row_chunk, rc_block):
                gates = [[None] * rc_block for _ in range(fk)]
                for hd in range(PEER_HEADS):
                    a_rows = [jnp.broadcast_to(a_ref[hd, p * fk + q, :, lanes], (row_chunk, LANES))
                              for q in range(fk)]
                    nd_rows = [jnp.broadcast_to(nd_ref[hd, p * fk + q, :, lanes], (row_chunk, LANES))
                               for q in range(fk)]
                    for r in range(rc_block):
                        rows = slice((rb + r) * row_chunk, (rb + r + 1) * row_chunk)
                        cv = c_ref[hd, rows, lanes]
                        bv = bn_ref[hd, rows, _paired_lanes(lt, tm)]
                        for q in range(fk):
                            g = jnp.where(cv >= nd_rows[q], bv * a_rows[q], 0.0)
                            gates[q][r] = g if gates[q][r] is None else gates[q][r] + g
                for q in range(fk):
                    for r in range(rc_block):
                        erows = slice(q * nk + (rb + r) * row_chunk, q * nk + (rb + r + 1) * row_chunk)
                        wt[slot, erows, lanes] = (_gelu_exact(at[slot, erows, lanes]) * gates[q][r]).astype(BF16)
        return carry

    lax.fori_loop(0, groups, group_step, 0)
    acc[...] += jnp.dot(vt_ref[groups - 1], wt[(groups - 1) % 2], preferred_element_type=F32)

    @pl.when(e == pl.num_programs(1) - 1)
    def _():
        f = acc[...].T
        v = dn_alpha * x_ref[...] + mod_ref[G2:G2 + 1, :] * f
        o_ref[...] = _layer_norm_rows(v, lg_ref[...], lb_ref[...])


PEER_FIRST_KEYS_PER_PASS = 4
PEER_GROUPS_PER_STEP = 4
PEER_ROW_CHUNKS_PER_BLOCK = 2
PEER_DENSE_FLAGS = None


def _peer_dense(x, mod_l, sel, u_tab, v_tab, ln_g, ln_b, tm, stream_of_tile, dn_alpha):
    ntok, d = x.shape
    n_exp = u_tab.shape[0]
    a_sel, nd_sel, c_sel, bn_sel = sel
    nk = c_sel.shape[1]
    n_tiles = ntok // tm
    fk, groups = PEER_FIRST_KEYS_PER_PASS, PEER_GROUPS_PER_STEP
    ge = fk * nk
    n_groups = n_exp // ge
    a_sel = a_sel.reshape(PEER_HEADS, nk, 1, ntok)
    nd_sel = nd_sel.reshape(PEER_HEADS, nk, 1, ntok)
    u3 = u_tab.astype(BF16).reshape(n_groups, ge, d)
    vt3 = v_tab.astype(BF16).reshape(n_groups, ge, d).transpose(0, 2, 1)
    row_spec = pl.BlockSpec((PEER_HEADS, groups * fk, 1, tm), lambda t, e: (0, e, 0, t))
    full_spec = pl.BlockSpec((PEER_HEADS, nk, tm), lambda t, e: (0, 0, t))
    kern = functools.partial(_peer_dense_kernel, tm=tm, groups=groups, fk=fk, nk=nk, dn_alpha=dn_alpha,
                             row_chunk=2 * SUBLANES, rc_block=PEER_ROW_CHUNKS_PER_BLOCK)
    return pl.pallas_call(
        kern,
        grid=(n_tiles, n_groups // groups),
        in_specs=[
            pl.BlockSpec((tm, d), lambda t, e: (t, 0)),
            pl.BlockSpec((None, N_MOD, d), lambda t, e: (stream_of_tile(t), 0, 0)),
            row_spec, row_spec, full_spec, full_spec,
            pl.BlockSpec((groups, ge, d), lambda t, e: (e, 0, 0)),
            pl.BlockSpec((groups, d, ge), lambda t, e: (e, 0, 0)),
            pl.BlockSpec((1, d), lambda t, e: (0, 0)),
            pl.BlockSpec((1, d), lambda t, e: (0, 0)),
        ],
        out_specs=pl.BlockSpec((tm, d), lambda t, e: (t, 0)),
        out_shape=jax.ShapeDtypeStruct((ntok, d), F32),
        scratch_shapes=[
            pltpu.VMEM((d, tm), BF16),
            pltpu.VMEM((d, tm), F32),
            pltpu.VMEM((2, ge, tm), BF16),
            pltpu.VMEM((2, ge, tm), F32),
        ],
        compiler_params=_cparams("parallel", "arbitrary", flags=PEER_DENSE_FLAGS),
        name="peer_dense",
    )(x, mod_l, a_sel, nd_sel, c_sel, bn_sel, u3, vt3, ln_g.reshape(1, d), ln_b.reshape(1, d))


def _peer_layer(x, mod_l, w_q, keys, u_tab, v_tab, ln_g, ln_b, tm, stream_of_tile, dn_alpha):
    sel = _peer_route(x, mod_l, w_q, keys, tm, stream_of_tile)
    return _peer_dense(x, mod_l, sel, u_tab, v_tab, ln_g, ln_b, tm, stream_of_tile, dn_alpha)


def _rope(v, cos, sin_signed):
    n = v.shape[-1]
    lane = lax.broadcasted_iota(jnp.int32, v.shape, v.ndim - 1)
    quarter = sin_signed.shape[-1] // 8
    partner = jnp.where(lane % (2 * quarter) < quarter,
                        pltpu.roll(v, n - quarter, v.ndim - 1), pltpu.roll(v, quarter, v.ndim - 1))
    reps = n // cos.shape[-1]
    if reps > 1:
        cos = jnp.concatenate([cos] * reps, axis=-1)
        sin_signed = jnp.concatenate([sin_signed] * reps, axis=-1)
    return v * cos + partner * sin_signed


def _cd_in_kernel(x_ref, mod_ref, w_ref, cos_ref, sin_ref, qg_ref, kg_ref,
                  qm_ref, km_ref, vm_ref, om_ref, qx_ref, ka_ref, va_ref, gt_ref,
                  *, d_m, d_qx, d_kv, att_dh, k_scale):
    h = x_ref[...] * (1.0 + mod_ref[SC1:SC1 + 1, :]) + mod_ref[SH1:SH1 + 1, :]
    z = _bdot(h, w_ref[...])
    qm_ref[...] = z[:, 0:d_m].astype(BF16)
    km_ref[...] = (z[:, d_m:2 * d_m] * k_scale).astype(BF16)
    vm_ref[...] = z[:, 2 * d_m:3 * d_m].astype(BF16)
    om_ref[...] = z[:, 3 * d_m:4 * d_m]
    off = 4 * d_m
    cos = cos_ref[...]
    sin = sin_ref[...]
    qx = z[:, off:off + d_qx]
    pieces = []
    for hq in range(d_qx // LANES):
        blk = qx[:, hq * LANES:(hq + 1) * LANES]
        ms = jnp.sum(blk * blk, axis=-1, keepdims=True) * (1.0 / att_dh)
        pieces.append(blk * lax.rsqrt(ms + RMS_EPS))
    qn = jnp.concatenate(pieces, axis=-1) * qg_ref[...]
    qx_ref[...] = _rope(qn, cos, sin).astype(BF16)
    off += d_qx
    kk = z[:, off:off + d_kv]
    lane = lax.broadcasted_iota(jnp.int32, kk.shape, 1)
    sq = kk * kk
    pieces = []
    for hk in range(d_kv // att_dh):
        sel = (lane >= hk * att_dh) & (lane < (hk + 1) * att_dh)
        ms = jnp.sum(jnp.where(sel, sq, 0.0), axis=-1, keepdims=True) * (1.0 / att_dh)
        pieces.append((sel, lax.rsqrt(ms + RMS_EPS)))
    scale = jnp.zeros_like(kk)
    for sel, r in pieces:
        scale = jnp.where(sel, r, scale)
    ka_ref[...] = _rope(kk * scale * kg_ref[...], cos, sin).astype(BF16)
    off += d_kv
    va_ref[...] = z[:, off:off + d_kv].astype(BF16)
    off += d_kv
    gt_ref[...] = z[:, off:]


def _cd_in(x, mod_l, w, cos_t, sin_t, q_gain, k_gain, tm, n_tiles, in_tile, stream_of_tile, pos_tile,
           d_m, d_qx, d_kv, att_dh, k_scale):
    d = x.shape[1]
    n = w.shape[1]
    ntok = n_tiles * tm
    widths = (d_m, d_m, d_m, d_m, d_qx, d_kv, d_kv, n - 4 * d_m - d_qx - 2 * d_kv)
    dtypes = (BF16, BF16, BF16, F32, BF16, BF16, BF16, F32)
    kern = functools.partial(_cd_in_kernel, d_m=d_m, d_qx=d_qx, d_kv=d_kv, att_dh=att_dh, k_scale=k_scale)
    return pl.pallas_call(
        kern,
        grid=(n_tiles,),
        in_specs=[
            pl.BlockSpec((tm, d), lambda t: (in_tile(t), 0)),
            pl.BlockSpec((None, N_MOD, d), lambda t: (stream_of_tile(t), 0, 0)),
            pl.BlockSpec((d, n), lambda t: (0, 0)),
            pl.BlockSpec((tm, cos_t.shape[1]), lambda t: (pos_tile(t), 0)),
            pl.BlockSpec((tm, sin_t.shape[1]), lambda t: (pos_tile(t), 0)),
            pl.BlockSpec((1, d_qx), lambda t: (0, 0)),
            pl.BlockSpec((1, d_kv), lambda t: (0, 0)),
        ],
        out_specs=[pl.BlockSpec((tm, wd), lambda t: (t, 0)) for wd in widths],
        out_shape=[jax.ShapeDtypeStruct((ntok, wd), dt) for wd, dt in zip(widths, dtypes)],
        compiler_params=_cparams("parallel"),
        name="cd_in",
    )(x, mod_l, w.astype(BF16), cos_t, sin_t, q_gain, k_gain)


def _mlstm_kernel(qf_ref, kf_ref, vf_ref, gf_ref, qb_ref, kb_ref, vb_ref, gb_ref, bias_ref,
                  hf_ref, hb_ref, c_s, n_s, m_s, *, chunk, dh, heads):
    s = pl.program_id(1)

    @pl.when(s == 0)
    def _():
        c_s[...] = jnp.zeros_like(c_s)
        n_s[...] = jnp.zeros_like(n_s)
        m_s[...] = jnp.zeros_like(m_s)

    ri = lax.broadcasted_iota(jnp.int32, (chunk, chunk), 0)
    cj = lax.broadcasted_iota(jnp.int32, (chunk, chunk), 1)
    streams = ((qf_ref, kf_ref, vf_ref, gf_ref, hf_ref), (qb_ref, kb_ref, vb_ref, gb_ref, hb_ref))
    for direction, (q_ref, k_ref, v_ref, g_ref, h_ref) in enumerate(streams):
        mask = (cj <= ri) if direction == 0 else (cj >= ri)
        edge = chunk - 1 if direction == 0 else 0
        gates = g_ref[...] + bias_ref[...]
        logf = jax.nn.log_sigmoid(gates)
        bc = jnp.dot(mask.astype(F32), logf, preferred_element_type=F32, precision=lax.Precision.HIGHEST)
        br = bc.T
        gr = gates.T
        for hd in range(heads):
            ci = (2 * direction) * heads + hd
            cf = (2 * direction + 1) * heads + hd
            idx = direction * heads + hd
            b_col = bc[:, cf:cf + 1]
            b_row = br[cf:cf + 1, :]
            ig_row = gr[ci:ci + 1, :]
            ig_col = gates[:, ci:ci + 1]
            m_prev = m_s[idx][:, 0:1]
            dmat = jnp.where(mask, b_col - b_row + ig_row, NEG_INF)
            inter = b_col + m_prev
            m_t = jnp.maximum(inter, jnp.max(dmat, axis=-1, keepdims=True))
            dexp = jnp.exp(dmat - m_t)
            w_inter = jnp.exp(inter - m_t)
            cols = slice(hd * dh, (hd + 1) * dh)
            q = q_ref[:, cols]
            k = k_ref[:, cols]
            v = v_ref[:, cols]
            c_prev = c_s[idx]
            n_prev = n_s[idx]
            sm = _bdot_nt(q, k) * dexp
            num = _bdot(sm, v) + w_inter * _bdot(q, c_prev)
            den = (jnp.sum(sm, axis=-1, keepdims=True)
                   + w_inter * jnp.sum(q.astype(F32) * n_prev, axis=-1, keepdims=True))
            h_ref[:, cols] = num / jnp.maximum(jnp.abs(den), jnp.exp(-m_t))
            b_last = bc[edge:edge + 1, cf:cf + 1]
            g_log = b_last - b_col + ig_col
            m_new = jnp.maximum(b_last + m_prev, jnp.max(g_log, axis=0, keepdims=True))
            wk = jnp.exp(g_log - m_new)
            decay = jnp.exp(b_last + m_prev - m_new)
            kw = k.astype(F32) * wk
            c_s[idx] = decay * c_prev + lax.dot_general(
                kw.astype(BF16), v, (((0,), (0,)), ((), ())), preferred_element_type=F32)
            n_s[idx] = decay * n_prev + jnp.sum(kw, axis=0, keepdims=True)
            m_s[idx] = jnp.broadcast_to(m_new, (1, LANES))


def _mlstm(qm, km, vm, gt, gate_bias, n_batch, chunks_per_seq, ctx_chunks, heads, dh):
    chunk = MLSTM_CHUNK
    ntok, dm = qm.shape
    ng = gt.shape[1]

    def fwd(b, s):
        return (b * chunks_per_seq + s, 0)

    def bwd(b, s):
        r = jnp.where(s < ctx_chunks, ctx_chunks - 1 - s, ctx_chunks + chunks_per_seq - 1 - s)
        return (b * chunks_per_seq + r, 0)

    tok_f = pl.BlockSpec((chunk, dm), fwd)
    tok_b = pl.BlockSpec((chunk, dm), bwd)
    kern = functools.partial(_mlstm_kernel, chunk=chunk, dh=dh, heads=heads)
    return pl.pallas_call(
        kern,
        grid=(n_batch, chunks_per_seq),
        in_specs=[tok_f, tok_f, tok_f, pl.BlockSpec((chunk, ng), fwd),
                  tok_b, tok_b, tok_b, pl.BlockSpec((chunk, ng), bwd),
                  pl.BlockSpec((1, ng), lambda b, s: (0, 0))],
        out_specs=[pl.BlockSpec((chunk, dm), fwd), pl.BlockSpec((chunk, dm), bwd)],
        out_shape=[jax.ShapeDtypeStruct((ntok, dm), F32)] * 2,
        scratch_shapes=[
            pltpu.VMEM((2 * heads, dh, dh), F32),
            pltpu.VMEM((2 * heads, 1, dh), F32),
            pltpu.VMEM((2 * heads, 1, LANES), F32),
        ],
        compiler_params=_cparams("parallel", "arbitrary"),
        name="mlstm",
    )(qm, km, vm, gt, qm, km, vm, gt, gate_bias)


def _attn_kernel(q_ref, k_ref, v_ref, o_ref, *, n_qh):
    k = k_ref[...]
    v = v_ref[...]
    for hq in range(n_qh):
        cols = slice(hq * LANES, (hq + 1) * LANES)
        s = _bdot_nt(q_ref[:, cols], k)
        p = jnp.exp(s - jnp.max(s, axis=-1, keepdims=True))
        l = jnp.sum(p, axis=-1, keepdims=True)
        o = jnp.dot(p.astype(BF16), v, preferred_element_type=F32)
        o_ref[:, cols] = (o / l).astype(o_ref.dtype)


def _attention(qx, ka, va, tq, n_batch, seq_tiles, q_tile_of, kv_len):
    n_qh = qx.shape[1] // LANES
    return pl.pallas_call(
        functools.partial(_attn_kernel, n_qh=n_qh),
        grid=(n_batch, seq_tiles),
        in_specs=[
            pl.BlockSpec((tq, qx.shape[1]), lambda b, t: (q_tile_of(b, t), 0)),
            pl.BlockSpec((kv_len, ka.shape[1]), lambda b, t: (b, 0)),
            pl.BlockSpec((kv_len, va.shape[1]), lambda b, t: (b, 0)),
        ],
        out_specs=pl.BlockSpec((tq, qx.shape[1]), lambda b, t: (b * seq_tiles + t, 0)),
        out_shape=jax.ShapeDtypeStruct((n_batch * seq_tiles * tq, qx.shape[1]), BF16),
        compiler_params=_cparams("parallel", "parallel"),
        name="attention",
    )(qx, ka, va)


def _cd_tail_kernel(hf_ref, hb_ref, om_ref, att_ref, x_ref, mod_ref, ng_ref, wm_ref, wa_ref, lg_ref, lb_ref,
                    o_ref, *, dh, heads, dn_alpha):
    hsum = hf_ref[...] + hb_ref[...]
    pieces = []
    for hd in range(heads):
        cols = slice(hd * dh, (hd + 1) * dh)
        blk = hsum[:, cols]
        mu = jnp.mean(blk, axis=-1, keepdims=True)
        dlt = blk - mu
        var = jnp.mean(dlt * dlt, axis=-1, keepdims=True)
        pieces.append(dlt * lax.rsqrt(var + LN_EPS))
    hn = jnp.concatenate(pieces, axis=-1) * ng_ref[...] * jax.nn.sigmoid(om_ref[...])
    y = _bdot(hn, wm_ref[...]) + jnp.dot(att_ref[...], wa_ref[...], preferred_element_type=F32)
    v = dn_alpha * x_ref[...] + mod_ref[G1:G1 + 1, :] * y
    o_ref[...] = _layer_norm_rows(v, lg_ref[...], lb_ref[...])


def _cd_tail(hf, hb, om, att, x, mod_l, norm_g, w_m, w_a, ln_g, ln_b, tm, n_tiles, seq_tile_of,
             stream_of_tile, heads, dh, dn_alpha):
    d = x.shape[1]
    dm = hf.shape[1]
    const2 = lambda t: (0, 0)
    kern = functools.partial(_cd_tail_kernel, dh=dh, heads=heads, dn_alpha=dn_alpha)
    return pl.pallas_call(
        kern,
        grid=(n_tiles,),
        in_specs=[
            pl.BlockSpec((tm, dm), lambda t: (seq_tile_of(t), 0)),
            pl.BlockSpec((tm, dm), lambda t: (seq_tile_of(t), 0)),
            pl.BlockSpec((tm, dm), lambda t: (seq_tile_of(t), 0)),
            pl.BlockSpec((tm, att.shape[1]), lambda t: (t, 0)),
            pl.BlockSpec((tm, d), lambda t: (t, 0)),
            pl.BlockSpec((None, N_MOD, d), lambda t: (stream_of_tile(t), 0, 0)),
            pl.BlockSpec((1, dm), const2),
            pl.BlockSpec(w_m.shape, const2),
            pl.BlockSpec(w_a.shape, const2),
            pl.BlockSpec((1, d), const2),
            pl.BlockSpec((1, d), const2),
        ],
        out_specs=pl.BlockSpec((tm, d), lambda t: (t, 0)),
        out_shape=jax.ShapeDtypeStruct((n_tiles * tm, d), F32),
        compiler_params=_cparams("parallel"),
        name="cd_tail",
    )(hf, hb, om, att, x, mod_l, norm_g.reshape(1, dm), w_m.astype(BF16), w_a.astype(BF16),
      ln_g.reshape(1, d), ln_b.reshape(1, d))


def _layer0(xf, mod_l, p, geom):
    tm = geom["tm_mix"]
    n_tiles = xf.shape[0] // tm
    n_lat_tiles = geom["n_lat"] // tm
    lat_per_seq = geom["seq"] // tm
    ctx_per_seq = geom["ctx"] // tm
    n_batch = geom["batch"]

    def stream(t):
        return jnp.where(t < n_lat_tiles, t // lat_per_seq, n_batch)

    (z,) = _modulated_matmul(xf, mod_l, p["ab_w_in"], (p["ab_w_in"].shape[1],), (F32,), tm,
                             lambda t: t, stream, n_tiles)
    x1 = _ab_tail(z, xf, mod_l, p["pool_w"], p["pool_ls"], p["conv_w"], p["conv_b"], p["conv_ln_g"],
                  p["conv_ln_b"], p["ab_w_out"], p["ln_g"][0], p["ln_b"][0], tm, n_lat_tiles,
                  lat_per_seq, ctx_per_seq, stream, geom["dn_alpha"])
    tp = geom["tm_peer"]
    n_lat_p = geom["n_lat"] // tp
    lat_per_seq_p = geom["seq"] // tp

    def stream_p(t):
        return jnp.where(t < n_lat_p, t // lat_per_seq_p, n_batch)

    return _peer_layer(x1, mod_l, p["peer_w_q"], p["peer_keys"], p["peer_u"], p["peer_v"],
                       p["ln_g"][1], p["ln_b"][1], tp, stream_p, geom["dn_alpha"])


def _rope_tables(seq, ctx_len, att_dh):
    n_freq = att_dh // 4
    t = jnp.arange(seq)
    freqs = ROPE_THETA ** (-jnp.arange(n_freq, dtype=F32) / n_freq)
    ar = (t // GRID_W).astype(F32)[:, None] * freqs
    ac = (t % GRID_W).astype(F32)[:, None] * freqs
    cos = jnp.concatenate([jnp.cos(ar), jnp.cos(ar), jnp.cos(ac), jnp.cos(ac)], axis=-1)
    sin = jnp.concatenate([-jnp.sin(ar), jnp.sin(ar), -jnp.sin(ac), jnp.sin(ac)], axis=-1)
    cos = jnp.concatenate([jnp.ones((ctx_len, att_dh), F32), cos], axis=0)
    sin = jnp.concatenate([jnp.zeros((ctx_len, att_dh), F32), sin], axis=0)
    reps = LANES // att_dh
    return jnp.tile(cos, (1, reps)), jnp.tile(sin, (1, reps))


def _layer1(xf, mod_l, p, geom):
    tm = geom["tm_mix"]
    n_batch, seq, ctx_len = geom["batch"], geom["seq"], geom["ctx"]
    d = xf.shape[1]
    heads = MLSTM_HEADS
    d_m = p["mlstm_norm_g"].shape[0]
    dh = d_m // heads
    att_dh = p["q_norm_g"].shape[0]
    n_qh, n_kvh = ATT_QH, ATT_KVH
    group = n_qh // n_kvh
    d_q, d_kv = n_qh * att_dh, n_kvh * att_dh
    n_gate = 4 * heads
    ctx_tiles, lat_tiles = ctx_len // tm, seq // tm
    per_b = ctx_tiles + lat_tiles
    n_lat_tiles = n_batch * lat_tiles

    w = p["cd_w_in"]
    cuts = np.cumsum([0, d_m, d_m, d_m, d_m, n_gate, d_q, d_kv, d_kv])
    w_qm, w_km, w_vm, w_om, w_gt, w_qa, w_ka, w_va = (w[:, cuts[i]:cuts[i + 1]] for i in range(8))
    w_qx = jnp.zeros((d, n_qh, LANES), F32)
    q_gain = jnp.zeros((n_qh, LANES), F32)
    w_att = jnp.zeros((n_qh, LANES, d), F32)
    for hq in range(n_qh):
        lo = (hq // group) * att_dh
        w_qx = w_qx.at[:, hq, lo:lo + att_dh].set(w_qa[:, hq * att_dh:(hq + 1) * att_dh])
        q_gain = q_gain.at[hq, lo:lo + att_dh].set(p["q_norm_g"] * att_dh ** -0.5)
        w_att = w_att.at[hq, lo:lo + att_dh, :].set(p["cd_w_out"][d_m + hq * att_dh:d_m + (hq + 1) * att_dh])
    d_qx = n_qh * LANES
    w_all = jnp.concatenate([w_qm, w_km, w_vm, w_om, w_qx.reshape(d, d_qx), w_ka, w_va,
                             jnp.pad(w_gt, ((0, 0), (0, LANES - n_gate)))], axis=1)
    gate_bias = jnp.pad(p["mlstm_gate_b"].reshape(1, n_gate), ((0, 0), (0, LANES - n_gate)))
    cos_t, sin_t = _rope_tables(seq, ctx_len, att_dh)

    def in_tile(t):
        b, r = t // per_b, t % per_b
        return jnp.where(r < ctx_tiles, n_lat_tiles + b * ctx_tiles + r, b * lat_tiles + r - ctx_tiles)

    def stream_seq(t):
        return jnp.where(t % per_b < ctx_tiles, n_batch, t // per_b)

    qm, km, vm, om, qx, ka, va, gt = _cd_in(
        xf, mod_l, w_all, cos_t, sin_t, q_gain.reshape(1, d_qx), jnp.tile(p["k_norm_g"], n_kvh).reshape(1, d_kv),
        tm, n_batch * per_b, in_tile, stream_seq, lambda t: t % per_b, d_m, d_qx, d_kv, att_dh, dh ** -0.5)

    chunks_per_seq = (ctx_len + seq) // MLSTM_CHUNK
    hf, hb = _mlstm(qm, km, vm, gt, gate_bias, n_batch, chunks_per_seq, ctx_len // MLSTM_CHUNK, heads, dh)
    att = _attention(qx, ka, va, tm, n_batch, lat_tiles, lambda b, t: b * per_b + ctx_tiles + t,
                     ctx_len + seq)

    def seq_tile_of(t):
        return (t // lat_tiles) * per_b + ctx_tiles + t % lat_tiles

    x1 = _cd_tail(hf, hb, om, att, xf, mod_l, p["mlstm_norm_g"], p["cd_w_out"][:d_m], w_att.reshape(d_qx, d),
                  p["ln_g"][0], p["ln_b"][0], tm, n_lat_tiles, seq_tile_of, lambda t: t // lat_tiles,
                  heads, dh, geom["dn_alpha"])
    tp = geom["tm_peer"]
    lat_per_seq_p = seq // tp
    return _peer_layer(x1, mod_l, p["peer_w_q"], p["peer_keys"], p["peer_u"], p["peer_v"],
                       p["ln_g"][1], p["ln_b"][1], tp, lambda t: t // lat_per_seq_p, geom["dn_alpha"])


def kernel(x, c, ctx, c_ctx, mod_w, mod_b, ln_g, ln_b, ab_w_in, pool_w, pool_ls, conv_w, conv_b,
           conv_ln_g, conv_ln_b, ab_w_out, cd_w_in, mlstm_gate_b, mlstm_norm_g, q_norm_g, k_norm_g,
           cd_w_out, peer_w_q, peer_keys, peer_u, peer_v):
    n_batch, seq, d = x.shape
    ctx_len = ctx.shape[1]
    depth = mod_w.shape[0]
    assert depth == 2, "one pooling/convolution layer followed by one mLSTM/attention layer"
    n_streams = SUBLANES * (-(-(n_batch + 1) // SUBLANES))
    cc = jnp.concatenate([c, c_ctx[None], jnp.zeros((n_streams - n_batch - 1, d), F32)], axis=0)
    mod = _modulation(cc, mod_w, mod_b)
    geom = dict(tm_mix=256, tm_peer=512, n_lat=n_batch * seq, seq=seq, ctx=ctx_len, batch=n_batch,
                dn_alpha=(2 * depth) ** 0.25)
    xf = jnp.concatenate([x.reshape(-1, d), ctx.reshape(-1, d)], axis=0)
    p0 = dict(ab_w_in=ab_w_in[0], pool_w=pool_w[0], pool_ls=pool_ls[0], conv_w=conv_w[0],
              conv_b=conv_b[0], conv_ln_g=conv_ln_g[0], conv_ln_b=conv_ln_b[0], ab_w_out=ab_w_out[0],
              peer_w_q=peer_w_q[0], peer_keys=peer_keys[0], peer_u=peer_u[0], peer_v=peer_v[0],
              ln_g=ln_g[0], ln_b=ln_b[0])
    xf = _layer0(xf, mod[0], p0, geom)
    p1 = dict(cd_w_in=cd_w_in[0], mlstm_gate_b=mlstm_gate_b[0], mlstm_norm_g=mlstm_norm_g[0],
              q_norm_g=q_norm_g[0], k_norm_g=k_norm_g[0], cd_w_out=cd_w_out[0],
              peer_w_q=peer_w_q[1], peer_keys=peer_keys[1], peer_u=peer_u[1], peer_v=peer_v[1],
              ln_g=ln_g[1], ln_b=ln_b[1])
    out = _layer1(xf, mod[1], p1, geom)
    return out.reshape(n_batch, seq, d)
```

```python
import functools
import math

import jax
import jax.numpy as jnp
import numpy as np
from jax import lax
from jax.experimental import pallas as pl
from jax.experimental.pallas import tpu as pltpu

F32 = jnp.float32
BF16 = jnp.bfloat16

LANES = 128
SUBLANES = 8
VMEM_LIMIT_BYTES = 56 * 1024 * 1024

GRID_W = 64
N_MOD = 6
POOL_WINDOWS = (2, 4, 8, 16)
CONV_W = 31
HALO = 16
MLSTM_HEADS = 4
MLSTM_CHUNK = 128
ATT_QH = 8
ATT_KVH = 2
ROPE_THETA = 10000.0
PEER_HEADS = 8
PEER_NKEYS = 128
PEER_TOPK = 16
LN_EPS = 1e-5
RMS_EPS = 1e-6
NEG_INF = float("-inf")

SH1, SC1, G1, SH2, SC2, G2 = range(N_MOD)


def _cparams(*sem, flags=None):
    return pltpu.CompilerParams(dimension_semantics=sem, vmem_limit_bytes=VMEM_LIMIT_BYTES, flags=flags)


def _bdot(a, b):
    return jnp.dot(a.astype(BF16), b.astype(BF16), preferred_element_type=F32)


def _bdot_nt(a, b):
    return lax.dot_general(a.astype(BF16), b.astype(BF16), (((1,), (1,)), ((), ())),
                           preferred_element_type=F32)


def _layer_norm_rows(v, g, b):
    mu = jnp.mean(v, axis=-1, keepdims=True)
    d = v - mu
    var = jnp.mean(d * d, axis=-1, keepdims=True)
    return d * lax.rsqrt(var + LN_EPS) * g + b


def _mod_kernel(c_ref, w_ref, b_ref, o_ref):
    c = c_ref[...]
    a = c * jax.nn.sigmoid(c)
    o_ref[...] = jnp.dot(a, w_ref[...], preferred_element_type=F32,
                         precision=lax.Precision.HIGHEST) + b_ref[...]


def _modulation(cc, mod_w, mod_b):
    depth, d, n = mod_w.shape
    rows = cc.shape[0]
    bn = d
    out = pl.pallas_call(
        _mod_kernel,
        grid=(depth, n // bn),
        in_specs=[
            pl.BlockSpec((rows, d), lambda l, j: (0, 0)),
            pl.BlockSpec((None, d, bn), lambda l, j: (l, 0, j)),
            pl.BlockSpec((None, 1, bn), lambda l, j: (l, 0, j)),
        ],
        out_specs=pl.BlockSpec((None, rows, bn), lambda l, j: (l, 0, j)),
        out_shape=jax.ShapeDtypeStruct((depth, rows, n), F32),
        compiler_params=_cparams("parallel", "parallel"),
        name="modulation",
    )(cc, mod_w, mod_b.reshape(depth, 1, n))
    return out.reshape(depth, rows, N_MOD, d)


def _modmm_kernel(x_ref, mod_ref, w_ref, *o_refs, splits):
    h = x_ref[...] * (1.0 + mod_ref[SC1:SC1 + 1, :]) + mod_ref[SH1:SH1 + 1, :]
    z = _bdot(h, w_ref[...])
    off = 0
    for o_ref, n in zip(o_refs, splits):
        o_ref[...] = z[:, off:off + n].astype(o_ref.dtype)
        off += n


def _modulated_matmul(x, mod_l, w, splits, out_dtypes, tm, in_tile, stream_of_tile, n_tiles):
    d = x.shape[1]
    n = w.shape[1]
    assert sum(splits) == n
    return pl.pallas_call(
        functools.partial(_modmm_kernel, splits=splits),
        grid=(n_tiles,),
        in_specs=[
            pl.BlockSpec((tm, d), lambda t: (in_tile(t), 0)),
            pl.BlockSpec((None, N_MOD, d), lambda t: (stream_of_tile(t), 0, 0)),
            pl.BlockSpec((d, n), lambda t: (0, 0)),
        ],
        out_specs=[pl.BlockSpec((tm, s), lambda t: (t, 0)) for s in splits],
        out_shape=[jax.ShapeDtypeStruct((n_tiles * tm, s), dt) for s, dt in zip(splits, out_dtypes)],
        compiler_params=_cparams("parallel"),
        name="modulated_matmul",
    )(x, mod_l, w.astype(BF16))


def _ab_tail_kernel(zp_ref, zm_ref, zn_ref, x_ref, mod_ref, pw_ref, pls_ref, cw_ref, cb_ref,
                    cg_ref, cbb_ref, wo_ref, lg_ref, lb_ref, o_ref, zpool, ubuf, ymix,
                    *, tm, d_pool, d_conv, n_lat_tiles, lat_tiles_per_seq, ctx_tiles_per_seq,
                    dn_alpha, row_chunk):
    t = pl.program_id(0)
    is_lat = t < n_lat_tiles
    per_seq = jnp.where(is_lat, lat_tiles_per_seq, ctx_tiles_per_seq)
    pos_tile = jnp.where(is_lat, t, t - n_lat_tiles) % per_seq
    first = pos_tile == 0
    last = pos_tile == per_seq - 1
    seq_len = per_seq * tm

    def glu(z):
        return z[:, d_pool:d_pool + d_conv] * jax.nn.sigmoid(z[:, d_pool + d_conv:])

    zp = jnp.where(first, 0.0, zp_ref[...])
    zn = jnp.where(last, 0.0, zn_ref[...])
    zm = zm_ref[...]
    zpool[0:HALO, :] = zp[:, :d_pool]
    zpool[HALO:HALO + tm, :] = zm[:, :d_pool]
    zpool[HALO + tm:, :] = zn[:, :d_pool]
    ubuf[0:HALO, :] = glu(zp)
    ubuf[HALO:HALO + tm, :] = glu(zm)
    ubuf[HALO + tm:, :] = glu(zn)

    gw = d_pool // len(POOL_WINDOWS)
    tpos = pos_tile * tm + lax.broadcasted_iota(jnp.int32, (tm, gw), 0)
    for g, w in enumerate(POOL_WINDOWS):
        cols = slice(g * gw, (g + 1) * gw)
        acc = zpool[HALO - w // 2:HALO - w // 2 + tm, cols]
        for s in range(1, w):
            acc = acc + zpool[HALO - w // 2 + s:HALO - w // 2 + s + tm, cols]
        cnt = jnp.minimum(tpos + w // 2, seq_len) - jnp.maximum(tpos - w // 2, 0)
        diff = acc / cnt.astype(F32) - zpool[HALO:HALO + tm, cols]
        ymix[:, cols] = (_bdot(diff, pw_ref[g]) * pls_ref[:, cols]).astype(BF16)

    half = CONV_W // 2
    for r in range(0, tm, row_chunk):
        base = HALO - half + r
        acc = ubuf[base:base + row_chunk, :] * cw_ref[0:1, :]
        for k in range(1, CONV_W):
            acc = acc + ubuf[base + k:base + k + row_chunk, :] * cw_ref[k:k + 1, :]
        yn = _layer_norm_rows(acc + cb_ref[...], cg_ref[...], cbb_ref[...])
        ymix[r:r + row_chunk, d_pool:] = (yn * jax.nn.sigmoid(yn)).astype(BF16)

    y = jnp.dot(ymix[...], wo_ref[...], preferred_element_type=F32)
    v = dn_alpha * x_ref[...] + mod_ref[G1:G1 + 1, :] * y
    o_ref[...] = _layer_norm_rows(v, lg_ref[...], lb_ref[...])


def _ab_tail(z, x, mod_l, pool_w, pool_ls, conv_w, conv_b, cln_g, cln_b, w_out, ln_g, ln_b,
             tm, n_lat_tiles, lat_tiles_per_seq, ctx_tiles_per_seq, stream_of_tile, dn_alpha):
    ntok, d = x.shape
    d_pool = pool_ls.shape[0]
    d_conv = conv_b.shape[0]
    n_tiles = ntok // tm
    hb = tm // HALO
    n_hblocks = ntok // HALO
    row = lambda a: a.reshape(1, -1)
    kern = functools.partial(
        _ab_tail_kernel, tm=tm, d_pool=d_pool, d_conv=d_conv, n_lat_tiles=n_lat_tiles,
        lat_tiles_per_seq=lat_tiles_per_seq, ctx_tiles_per_seq=ctx_tiles_per_seq,
        dn_alpha=dn_alpha, row_chunk=32)
    const2 = lambda t: (0, 0)
    return pl.pallas_call(
        kern,
        grid=(n_tiles,),
        in_specs=[
            pl.BlockSpec((HALO, z.shape[1]), lambda t: (jnp.maximum(t * hb - 1, 0), 0)),
            pl.BlockSpec((tm, z.shape[1]), lambda t: (t, 0)),
            pl.BlockSpec((HALO, z.shape[1]), lambda t: (jnp.minimum((t + 1) * hb, n_hblocks - 1), 0)),
            pl.BlockSpec((tm, d), lambda t: (t, 0)),
            pl.BlockSpec((None, N_MOD, d), lambda t: (stream_of_tile(t), 0, 0)),
            pl.BlockSpec(pool_w.shape, lambda t: (0, 0, 0)),
            pl.BlockSpec((1, d_pool), const2),
            pl.BlockSpec(conv_w.shape, const2),
            pl.BlockSpec((1, d_conv), const2),
            pl.BlockSpec((1, d_conv), const2),
            pl.BlockSpec((1, d_conv), const2),
            pl.BlockSpec(w_out.shape, const2),
            pl.BlockSpec((1, d), const2),
            pl.BlockSpec((1, d), const2),
        ],
        out_specs=pl.BlockSpec((tm, d), lambda t: (t, 0)),
        out_shape=jax.ShapeDtypeStruct((ntok, d), F32),
        scratch_shapes=[
            pltpu.VMEM((tm + 2 * HALO, d_pool), F32),
            pltpu.VMEM((tm + 2 * HALO, d_conv), F32),
            pltpu.VMEM((tm, d_pool + d_conv), BF16),
        ],
        compiler_params=_cparams("parallel"),
        name="ab_tail",
    )(z, z, z, x, mod_l, pool_w.astype(BF16), row(pool_ls), conv_w, row(conv_b), row(cln_g),
      row(cln_b), w_out.astype(BF16), row(ln_g), row(ln_b))


def _paired_lanes(lt, tm):
    shifted = (lt + 1) % (tm // LANES)
    return slice(shifted * LANES, (shifted + 1) * LANES)


def _top_values(s, k, want_rank=False):
    outs = []
    cur = s
    rank = jnp.full(s.shape, float(k), F32) if want_rank else None
    for it in range(k):
        m = jnp.max(cur, axis=0, keepdims=True)
        outs.append(m)
        hit = cur >= m
        if want_rank:
            rank = jnp.where(hit, float(it), rank)
        if it + 1 < k:
            cur = jnp.where(hit, NEG_INF, cur)
    return (outs, rank) if want_rank else outs


def _peer_route_kernel(x_ref, mod_ref, wq_ref, keys_ref, a_ref, ri_ref, r2_ref, bn_ref, qt_ref,
                       *, tm, half):
    h = x_ref[...] * (1.0 + mod_ref[SC2:SC2 + 1, :]) + mod_ref[SH2:SH2 + 1, :]
    qt_ref[...] = _bdot_nt(wq_ref[...], h).astype(BF16)
    k = PEER_TOPK
    for hd in range(PEER_HEADS):
        for lt in range(tm // LANES):
            lanes = slice(lt * LANES, (lt + 1) * LANES)
            r0 = hd * 2 * half
            s1 = jnp.dot(keys_ref[0], qt_ref[r0:r0 + half, lanes], preferred_element_type=F32)
            s2 = jnp.dot(keys_ref[1], qt_ref[r0 + half:r0 + 2 * half, lanes],
                         preferred_element_type=F32)
            t1 = _top_values(s1, k)
            t2, rank2 = _top_values(s2, k, want_rank=True)
            t2_all = jnp.concatenate(t2, axis=0)
            t1_tail = jnp.concatenate(t1[k // 2:], axis=0)
            pieces = []
            for a in range(k // 2):
                nb = min(k, (k + 1) // (a + 1))
                pieces.append(t1[a] + t2_all[:nb, :])
            pieces.append(t1_tail + t2[0])
            n_cand = sum(p.shape[0] for p in pieces)
            pad = (-n_cand) % SUBLANES
            if pad:
                pieces.append(jnp.full((pad, LANES), NEG_INF, F32))
            cand = jnp.concatenate(pieces, axis=0)
            top = _top_values(cand, k + 1)
            kth = top[k - 1]
            thr = 0.5 * (kth + jnp.maximum(top[k], kth - 1.0))
            m1, m2 = t1[0], t2[0]
            z = jnp.sum(jnp.where(cand >= thr, jnp.exp(cand - (m1 + m2)), 0.0), axis=0, keepdims=True)
            wins = jnp.zeros_like(s1)
            for b in range(k):
                wins = wins + jnp.where(s1 >= thr - t2[b], 1.0, 0.0)
            a_ref[hd, :, lanes] = jnp.exp(s1 - m1)
            ri_ref[hd, :, lanes] = jnp.where(s1 >= t1[k - 1], wins, 0.0)
            r2_ref[hd, :, lanes] = rank2.astype(BF16)
            bn_ref[hd, :, _paired_lanes(lt, tm)] = (jnp.exp(s2 - m2) / z).astype(BF16)


def _peer_route(x, mod_l, w_q, keys, tm, stream_of_tile):
    ntok, d = x.shape
    nq = w_q.shape[1]
    half = keys.shape[2]
    nk = keys.shape[1]
    n_tiles = ntok // tm
    sel_spec = pl.BlockSpec((PEER_HEADS, nk, tm), lambda t: (0, 0, t))
    return pl.pallas_call(
        functools.partial(_peer_route_kernel, tm=tm, half=half),
        grid=(n_tiles,),
        in_specs=[
            pl.BlockSpec((tm, d), lambda t: (t, 0)),
            pl.BlockSpec((None, N_MOD, d), lambda t: (stream_of_tile(t), 0, 0)),
            pl.BlockSpec((nq, d), lambda t: (0, 0)),
            pl.BlockSpec(keys.shape, lambda t: (0, 0, 0)),
        ],
        out_specs=[sel_spec] * 4,
        out_shape=[jax.ShapeDtypeStruct((PEER_HEADS, nk, ntok), dt) for dt in (F32, F32, BF16, BF16)],
        scratch_shapes=[pltpu.VMEM((nq, tm), BF16)],
        compiler_params=_cparams("parallel"),
        name="peer_route",
    )(x, mod_l, w_q.T.astype(BF16), keys.astype(BF16))


def _gelu_exact(a):
    return 0.5 * a * (1.0 + lax.erf(a * (1.0 / math.sqrt(2.0))))


def _peer_dense_kernel(x_ref, mod_ref, a_ref, ri_ref, r2_ref, bn_ref, u_ref, vt_ref, lg_ref, lb_ref,
                       o_ref, hbf, acc, wt, at, rows_scr, r2_s, bn_s,
                       *, tm, groups, fk, nk, dn_alpha, row_chunk, rc_block):
    e = pl.program_id(1)

    @pl.when(e == 0)
    def _():
        h = x_ref[...] * (1.0 + mod_ref[SC2:SC2 + 1, :]) + mod_ref[SH2:SH2 + 1, :]
        hbf[...] = h.T.astype(BF16)
        acc[...] = jnp.zeros_like(acc)
        r2_s[...] = r2_ref[...]
        bn_s[...] = bn_ref[...]

    at[0] = jnp.dot(u_ref[0], hbf[...], preferred_element_type=F32)
    wt[1] = jnp.zeros(wt.shape[1:], BF16)

    def group_step(p, carry):
        slot = p % 2
        at[1 - slot] = jnp.dot(u_ref[jnp.minimum(p + 1, groups - 1)], hbf[...], preferred_element_type=F32)
        acc[...] += jnp.dot(vt_ref[jnp.maximum(p - 1, 0)], wt[1 - slot], preferred_element_type=F32)
        tile_row = pl.multiple_of((p * fk // SUBLANES) * SUBLANES, SUBLANES)
        upper = (p * fk) % SUBLANES != 0
        for which, src in enumerate((a_ref, ri_ref)):
            for hd in range(PEER_HEADS):
                tile = src[hd, pl.ds(tile_row, SUBLANES), :]
                mine = jnp.where(upper, tile[fk:2 * fk, :], tile[0:fk, :]) if fk < SUBLANES else tile
                for q in range(fk):
                    rows_scr[which, hd, q] = jnp.broadcast_to(mine[q:q + 1, :], (row_chunk, tm)).astype(BF16)
        zero = jnp.zeros((row_chunk, LANES), BF16)
        for lt in range(tm // LANES):
            lanes = slice(lt * LANES, (lt + 1) * LANES)
            for rb in range(0, nk // row_chunk, rc_block):
                gates = [[None] * rc_block for _ in range(fk)]
                for hd in range(PEER_HEADS):
                    a_rows = [rows_scr[0, hd, q, :, lanes] for q in range(fk)]
                    ri_rows = [rows_scr[1, hd, q, :, lanes] for q in range(fk)]
                    for r in range(rc_block):
                        rows = slice((rb + r) * row_chunk, (rb + r + 1) * row_chunk)
                        rk = r2_s[hd, rows, lanes]
                        bv = bn_s[hd, rows, _paired_lanes(lt, tm)]
                        for q in range(fk):
                            g = jnp.where(rk < ri_rows[q], bv * a_rows[q], zero)
                            gates[q][r] = g if gates[q][r] is None else gates[q][r] + g
                for q in range(fk):
                    for r in range(rc_block):
                        erows = slice(q * nk + (rb + r) * row_chunk, q * nk + (rb + r + 1) * row_chunk)
                        wt[slot, erows, lanes] = _gelu_exact(at[slot, erows, lanes]).astype(BF16) * gates[q][r]
        return carry

    lax.fori_loop(0, groups, group_step, 0)
    acc[...] += jnp.dot(vt_ref[groups - 1], wt[(groups - 1) % 2], preferred_element_type=F32)

    @pl.when(e == pl.num_programs(1) - 1)
    def _():
        f = acc[...].T
        v = dn_alpha * x_ref[...] + mod_ref[G2:G2 + 1, :] * f
        o_ref[...] = _layer_norm_rows(v, lg_ref[...], lb_ref[...])


PEER_FIRST_KEYS_PER_PASS = 4
PEER_GROUPS_PER_STEP = 4
PEER_ROW_CHUNKS_PER_BLOCK = 2
PEER_DENSE_FLAGS = None


def _peer_dense(x, mod_l, sel, u_tab, v_tab, ln_g, ln_b, tm, stream_of_tile, dn_alpha):
    ntok, d = x.shape
    n_exp = u_tab.shape[0]
    a_sel, ri_sel, r2_sel, bn_sel = sel
    nk = r2_sel.shape[1]
    n_tiles = ntok // tm
    fk, groups = PEER_FIRST_KEYS_PER_PASS, PEER_GROUPS_PER_STEP
    assert SUBLANES % fk == 0 and (groups * fk) % SUBLANES == 0
    ge = fk * nk
    n_groups = n_exp // ge
    u3 = u_tab.astype(BF16).reshape(n_groups, ge, d)
    vt3 = v_tab.astype(BF16).reshape(n_groups, ge, d).transpose(0, 2, 1)
    row_spec = pl.BlockSpec((PEER_HEADS, groups * fk, tm), lambda t, e: (0, e, t))
    full_spec = pl.BlockSpec((PEER_HEADS, nk, tm), lambda t, e: (0, 0, t))
    kern = functools.partial(_peer_dense_kernel, tm=tm, groups=groups, fk=fk, nk=nk, dn_alpha=dn_alpha,
                             row_chunk=2 * SUBLANES, rc_block=PEER_ROW_CHUNKS_PER_BLOCK)
    return pl.pallas_call(
        kern,
        grid=(n_tiles, n_groups // groups),
        in_specs=[
            pl.BlockSpec((tm, d), lambda t, e: (t, 0)),
            pl.BlockSpec((None, N_MOD, d), lambda t, e: (stream_of_tile(t), 0, 0)),
            row_spec, row_spec, full_spec, full_spec,
            pl.BlockSpec((groups, ge, d), lambda t, e: (e, 0, 0)),
            pl.BlockSpec((groups, d, ge), lambda t, e: (e, 0, 0)),
            pl.BlockSpec((1, d), lambda t, e: (0, 0)),
            pl.BlockSpec((1, d), lambda t, e: (0, 0)),
        ],
        out_specs=pl.BlockSpec((tm, d), lambda t, e: (t, 0)),
        out_shape=jax.ShapeDtypeStruct((ntok, d), F32),
        scratch_shapes=[
            pltpu.VMEM((d, tm), BF16),
            pltpu.VMEM((d, tm), F32),
            pltpu.VMEM((2, ge, tm), BF16),
            pltpu.VMEM((2, ge, tm), F32),
            pltpu.VMEM((2, PEER_HEADS, fk, 2 * SUBLANES, tm), BF16),
            pltpu.VMEM((PEER_HEADS, nk, tm), BF16),
            pltpu.VMEM((PEER_HEADS, nk, tm), BF16),
        ],
        compiler_params=_cparams("parallel", "arbitrary", flags=PEER_DENSE_FLAGS),
        name="peer_dense",
    )(x, mod_l, a_sel, ri_sel, r2_sel, bn_sel, u3, vt3, ln_g.reshape(1, d), ln_b.reshape(1, d))


def _peer_layer(x, mod_l, w_q, keys, u_tab, v_tab, ln_g, ln_b, tm, stream_of_tile, dn_alpha):
    sel = _peer_route(x, mod_l, w_q, keys, tm, stream_of_tile)
    return _peer_dense(x, mod_l, sel, u_tab, v_tab, ln_g, ln_b, tm, stream_of_tile, dn_alpha)


def _rope(v, cos, sin_signed):
    n = v.shape[-1]
    lane = lax.broadcasted_iota(jnp.int32, v.shape, v.ndim - 1)
    quarter = sin_signed.shape[-1] // 8
    partner = jnp.where(lane % (2 * quarter) < quarter,
                        pltpu.roll(v, n - quarter, v.ndim - 1), pltpu.roll(v, quarter, v.ndim - 1))
    reps = n // cos.shape[-1]
    if reps > 1:
        cos = jnp.concatenate([cos] * reps, axis=-1)
        sin_signed = jnp.concatenate([sin_signed] * reps, axis=-1)
    return v * cos + partner * sin_signed


def _cd_in_kernel(x_ref, mod_ref, w_ref, cos_ref, sin_ref, qg_ref, kg_ref,
                  qm_ref, km_ref, vm_ref, om_ref, qx_ref, ka_ref, va_ref, gt_ref,
                  *, d_m, d_qx, d_kv, att_dh, k_scale):
    h = x_ref[...] * (1.0 + mod_ref[SC1:SC1 + 1, :]) + mod_ref[SH1:SH1 + 1, :]
    z = _bdot(h, w_ref[...])
    qm_ref[...] = z[:, 0:d_m].astype(BF16)
    km_ref[...] = (z[:, d_m:2 * d_m] * k_scale).astype(BF16)
    vm_ref[...] = z[:, 2 * d_m:3 * d_m].astype(BF16)
    om_ref[...] = z[:, 3 * d_m:4 * d_m]
    off = 4 * d_m
    cos = cos_ref[...]
    sin = sin_ref[...]
    qx = z[:, off:off + d_qx]
    pieces = []
    for hq in range(d_qx // LANES):
        blk = qx[:, hq * LANES:(hq + 1) * LANES]
        ms = jnp.sum(blk * blk, axis=-1, keepdims=True) * (1.0 / att_dh)
        pieces.append(blk * lax.rsqrt(ms + RMS_EPS))
    qn = jnp.concatenate(pieces, axis=-1) * qg_ref[...]
    qx_ref[...] = _rope(qn, cos, sin).astype(BF16)
    off += d_qx
    kk = z[:, off:off + d_kv]
    lane = lax.broadcasted_iota(jnp.int32, kk.shape, 1)
    sq = kk * kk
    pieces = []
    for hk in range(d_kv // att_dh):
        sel = (lane >= hk * att_dh) & (lane < (hk + 1) * att_dh)
        ms = jnp.sum(jnp.where(sel, sq, 0.0), axis=-1, keepdims=True) * (1.0 / att_dh)
        pieces.append((sel, lax.rsqrt(ms + RMS_EPS)))
    scale = jnp.zeros_like(kk)
    for sel, r in pieces:
        scale = jnp.where(sel, r, scale)
    ka_ref[...] = _rope(kk * scale * kg_ref[...], cos, sin).astype(BF16)
    off += d_kv
    va_ref[...] = z[:, off:off + d_kv].astype(BF16)
    off += d_kv
    gt_ref[...] = z[:, off:]


def _cd_in(x, mod_l, w, cos_t, sin_t, q_gain, k_gain, tm, n_tiles, in_tile, stream_of_tile, pos_tile,
           d_m, d_qx, d_kv, att_dh, k_scale):
    d = x.shape[1]
    n = w.shape[1]
    ntok = n_tiles * tm
    widths = (d_m, d_m, d_m, d_m, d_qx, d_kv, d_kv, n - 4 * d_m - d_qx - 2 * d_kv)
    dtypes = (BF16, BF16, BF16, F32, BF16, BF16, BF16, F32)
    kern = functools.partial(_cd_in_kernel, d_m=d_m, d_qx=d_qx, d_kv=d_kv, att_dh=att_dh, k_scale=k_scale)
    return pl.pallas_call(
        kern,
        grid=(n_tiles,),
        in_specs=[
            pl.BlockSpec((tm, d), lambda t: (in_tile(t), 0)),
            pl.BlockSpec((None, N_MOD, d), lambda t: (stream_of_tile(t), 0, 0)),
            pl.BlockSpec((d, n), lambda t: (0, 0)),
            pl.BlockSpec((tm, cos_t.shape[1]), lambda t: (pos_tile(t), 0)),
            pl.BlockSpec((tm, sin_t.shape[1]), lambda t: (pos_tile(t), 0)),
            pl.BlockSpec((1, d_qx), lambda t: (0, 0)),
            pl.BlockSpec((1, d_kv), lambda t: (0, 0)),
        ],
        out_specs=[pl.BlockSpec((tm, wd), lambda t: (t, 0)) for wd in widths],
        out_shape=[jax.ShapeDtypeStruct((ntok, wd), dt) for wd, dt in zip(widths, dtypes)],
        compiler_params=_cparams("parallel"),
        name="cd_in",
    )(x, mod_l, w.astype(BF16), cos_t, sin_t, q_gain, k_gain)


def _mlstm_kernel(qf_ref, kf_ref, vf_ref, gf_ref, qb_ref, kb_ref, vb_ref, gb_ref, bias_ref,
                  hf_ref, hb_ref, c_s, n_s, m_s, *, chunk, dh, heads):
    s = pl.program_id(1)

    @pl.when(s == 0)
    def _():
        c_s[...] = jnp.zeros_like(c_s)
        n_s[...] = jnp.zeros_like(n_s)
        m_s[...] = jnp.zeros_like(m_s)

    ri = lax.broadcasted_iota(jnp.int32, (chunk, chunk), 0)
    cj = lax.broadcasted_iota(jnp.int32, (chunk, chunk), 1)
    streams = ((qf_ref, kf_ref, vf_ref, gf_ref, hf_ref), (qb_ref, kb_ref, vb_ref, gb_ref, hb_ref))
    for direction, (q_ref, k_ref, v_ref, g_ref, h_ref) in enumerate(streams):
        mask = (cj <= ri) if direction == 0 else (cj >= ri)
        edge = chunk - 1 if direction == 0 else 0
        gates = g_ref[...] + bias_ref[...]
        logf = jax.nn.log_sigmoid(gates)
        bc = jnp.dot(mask.astype(F32), logf, preferred_element_type=F32, precision=lax.Precision.HIGHEST)
        br = bc.T
        gr = gates.T
        for hd in range(heads):
            ci = (2 * direction) * heads + hd
            cf = (2 * direction + 1) * heads + hd
            idx = direction * heads + hd
            b_col = bc[:, cf:cf + 1]
            b_row = br[cf:cf + 1, :]
            ig_row = gr[ci:ci + 1, :]
            ig_col = gates[:, ci:ci + 1]
            m_prev = m_s[idx][:, 0:1]
            dmat = jnp.where(mask, b_col - b_row + ig_row, NEG_INF)
            inter = b_col + m_prev
            m_t = jnp.maximum(inter, jnp.max(dmat, axis=-1, keepdims=True))
            dexp = jnp.exp(dmat - m_t)
            w_inter = jnp.exp(inter - m_t)
            cols = slice(hd * dh, (hd + 1) * dh)
            q = q_ref[:, cols]
            k = k_ref[:, cols]
            v = v_ref[:, cols]
            c_prev = c_s[idx]
            n_prev = n_s[idx]
            sm = _bdot_nt(q, k) * dexp
            num = _bdot(sm, v) + w_inter * _bdot(q, c_prev)
            den = (jnp.sum(sm, axis=-1, keepdims=True)
                   + w_inter * jnp.sum(q.astype(F32) * n_prev, axis=-1, keepdims=True))
            h_ref[:, cols] = num / jnp.maximum(jnp.abs(den), jnp.exp(-m_t))
            b_last = bc[edge:edge + 1, cf:cf + 1]
            g_log = b_last - b_col + ig_col
            m_new = jnp.maximum(b_last + m_prev, jnp.max(g_log, axis=0, keepdims=True))
            wk = jnp.exp(g_log - m_new)
            decay = jnp.exp(b_last + m_prev - m_new)
            kw = k.astype(F32) * wk
            c_s[idx] = decay * c_prev + lax.dot_general(
                kw.astype(BF16), v, (((0,), (0,)), ((), ())), preferred_element_type=F32)
            n_s[idx] = decay * n_prev + jnp.sum(kw, axis=0, keepdims=True)
            m_s[idx] = jnp.broadcast_to(m_new, (1, LANES))


def _mlstm(qm, km, vm, gt, gate_bias, n_batch, chunks_per_seq, ctx_chunks, heads, dh):
    chunk = MLSTM_CHUNK
    ntok, dm = qm.shape
    ng = gt.shape[1]

    def fwd(b, s):
        return (b * chunks_per_seq + s, 0)

    def bwd(b, s):
        r = jnp.where(s < ctx_chunks, ctx_chunks - 1 - s, ctx_chunks + chunks_per_seq - 1 - s)
        return (b * chunks_per_seq + r, 0)

    tok_f = pl.BlockSpec((chunk, dm), fwd)
    tok_b = pl.BlockSpec((chunk, dm), bwd)
    kern = functools.partial(_mlstm_kernel, chunk=chunk, dh=dh, heads=heads)
    return pl.pallas_call(
        kern,
        grid=(n_batch, chunks_per_seq),
        in_specs=[tok_f, tok_f, tok_f, pl.BlockSpec((chunk, ng), fwd),
                  tok_b, tok_b, tok_b, pl.BlockSpec((chunk, ng), bwd),
                  pl.BlockSpec((1, ng), lambda b, s: (0, 0))],
        out_specs=[pl.BlockSpec((chunk, dm), fwd), pl.BlockSpec((chunk, dm), bwd)],
        out_shape=[jax.ShapeDtypeStruct((ntok, dm), F32)] * 2,
        scratch_shapes=[
            pltpu.VMEM((2 * heads, dh, dh), F32),
            pltpu.VMEM((2 * heads, 1, dh), F32),
            pltpu.VMEM((2 * heads, 1, LANES), F32),
        ],
        compiler_params=_cparams("parallel", "arbitrary"),
        name="mlstm",
    )(qm, km, vm, gt, qm, km, vm, gt, gate_bias)


def _attn_kernel(q_ref, k_ref, v_ref, o_ref, *, n_qh):
    k = k_ref[...]
    v = v_ref[...]
    for hq in range(n_qh):
        cols = slice(hq * LANES, (hq + 1) * LANES)
        s = _bdot_nt(q_ref[:, cols], k)
        p = jnp.exp(s - jnp.max(s, axis=-1, keepdims=True))
        l = jnp.sum(p, axis=-1, keepdims=True)
        o = jnp.dot(p.astype(BF16), v, preferred_element_type=F32)
        o_ref[:, cols] = (o / l).astype(o_ref.dtype)


def _attention(qx, ka, va, tq, n_batch, seq_tiles, q_tile_of, kv_len):
    n_qh = qx.shape[1] // LANES
    return pl.pallas_call(
        functools.partial(_attn_kernel, n_qh=n_qh),
        grid=(n_batch, seq_tiles),
        in_specs=[
            pl.BlockSpec((tq, qx.shape[1]), lambda b, t: (q_tile_of(b, t), 0)),
            pl.BlockSpec((kv_len, ka.shape[1]), lambda b, t: (b, 0)),
            pl.BlockSpec((kv_len, va.shape[1]), lambda b, t: (b, 0)),
        ],
        out_specs=pl.BlockSpec((tq, qx.shape[1]), lambda b, t: (b * seq_tiles + t, 0)),
        out_shape=jax.ShapeDtypeStruct((n_batch * seq_tiles * tq, qx.shape[1]), BF16),
        compiler_params=_cparams("parallel", "parallel"),
        name="attention",
    )(qx, ka, va)


def _cd_tail_kernel(hf_ref, hb_ref, om_ref, att_ref, x_ref, mod_ref, ng_ref, wm_ref, wa_ref, lg_ref, lb_ref,
                    o_ref, *, dh, heads, dn_alpha):
    hsum = hf_ref[...] + hb_ref[...]
    pieces = []
    for hd in range(heads):
        cols = slice(hd * dh, (hd + 1) * dh)
        blk = hsum[:, cols]
        mu = jnp.mean(blk, axis=-1, keepdims=True)
        dlt = blk - mu
        var = jnp.mean(dlt * dlt, axis=-1, keepdims=True)
        pieces.append(dlt * lax.rsqrt(var + LN_EPS))
    hn = jnp.concatenate(pieces, axis=-1) * ng_ref[...] * jax.nn.sigmoid(om_ref[...])
    y = _bdot(hn, wm_ref[...]) + jnp.dot(att_ref[...], wa_ref[...], preferred_element_type=F32)
    v = dn_alpha * x_ref[...] + mod_ref[G1:G1 + 1, :] * y
    o_ref[...] = _layer_norm_rows(v, lg_ref[...], lb_ref[...])


def _cd_tail(hf, hb, om, att, x, mod_l, norm_g, w_m, w_a, ln_g, ln_b, tm, n_tiles, seq_tile_of,
             stream_of_tile, heads, dh, dn_alpha):
    d = x.shape[1]
    dm = hf.shape[1]
    const2 = lambda t: (0, 0)
    kern = functools.partial(_cd_tail_kernel, dh=dh, heads=heads, dn_alpha=dn_alpha)
    return pl.pallas_call(
        kern,
        grid=(n_tiles,),
        in_specs=[
            pl.BlockSpec((tm, dm), lambda t: (seq_tile_of(t), 0)),
            pl.BlockSpec((tm, dm), lambda t: (seq_tile_of(t), 0)),
            pl.BlockSpec((tm, dm), lambda t: (seq_tile_of(t), 0)),
            pl.BlockSpec((tm, att.shape[1]), lambda t: (t, 0)),
            pl.BlockSpec((tm, d), lambda t: (t, 0)),
            pl.BlockSpec((None, N_MOD, d), lambda t: (stream_of_tile(t), 0, 0)),
            pl.BlockSpec((1, dm), const2),
            pl.BlockSpec(w_m.shape, const2),
            pl.BlockSpec(w_a.shape, const2),
            pl.BlockSpec((1, d), const2),
            pl.BlockSpec((1, d), const2),
        ],
        out_specs=pl.BlockSpec((tm, d), lambda t: (t, 0)),
        out_shape=jax.ShapeDtypeStruct((n_tiles * tm, d), F32),
        compiler_params=_cparams("parallel"),
        name="cd_tail",
    )(hf, hb, om, att, x, mod_l, norm_g.reshape(1, dm), w_m.astype(BF16), w_a.astype(BF16),
      ln_g.reshape(1, d), ln_b.reshape(1, d))


def _layer0(xf, mod_l, p, geom):
    tm = geom["tm_mix"]
    n_tiles = xf.shape[0] // tm
    n_lat_tiles = geom["n_lat"] // tm
    lat_per_seq = geom["seq"] // tm
    ctx_per_seq = geom["ctx"] // tm
    n_batch = geom["batch"]

    def stream(t):
        return jnp.where(t < n_lat_tiles, t // lat_per_seq, n_batch)

    (z,) = _modulated_matmul(xf, mod_l, p["ab_w_in"], (p["ab_w_in"].shape[1],), (F32,), tm,
                             lambda t: t, stream, n_tiles)
    x1 = _ab_tail(z, xf, mod_l, p["pool_w"], p["pool_ls"], p["conv_w"], p["conv_b"], p["conv_ln_g"],
                  p["conv_ln_b"], p["ab_w_out"], p["ln_g"][0], p["ln_b"][0], tm, n_lat_tiles,
                  lat_per_seq, ctx_per_seq, stream, geom["dn_alpha"])
    tp = geom["tm_peer"]
    n_lat_p = geom["n_lat"] // tp
    lat_per_seq_p = geom["seq"] // tp

    def stream_p(t):
        return jnp.where(t < n_lat_p, t // lat_per_seq_p, n_batch)

    return _peer_layer(x1, mod_l, p["peer_w_q"], p["peer_keys"], p["peer_u"], p["peer_v"],
                       p["ln_g"][1], p["ln_b"][1], tp, stream_p, geom["dn_alpha"])


def _rope_tables(seq, ctx_len, att_dh):
    n_freq = att_dh // 4
    t = jnp.arange(seq)
    freqs = ROPE_THETA ** (-jnp.arange(n_freq, dtype=F32) / n_freq)
    ar = (t // GRID_W).astype(F32)[:, None] * freqs
    ac = (t % GRID_W).astype(F32)[:, None] * freqs
    cos = jnp.concatenate([jnp.cos(ar), jnp.cos(ar), jnp.cos(ac), jnp.cos(ac)], axis=-1)
    sin = jnp.concatenate([-jnp.sin(ar), jnp.sin(ar), -jnp.sin(ac), jnp.sin(ac)], axis=-1)
    cos = jnp.concatenate([jnp.ones((ctx_len, att_dh), F32), cos], axis=0)
    sin = jnp.concatenate([jnp.zeros((ctx_len, att_dh), F32), sin], axis=0)
    reps = LANES // att_dh
    return jnp.tile(cos, (1, reps)), jnp.tile(sin, (1, reps))


def _layer1(xf, mod_l, p, geom):
    tm = geom["tm_mix"]
    n_batch, seq, ctx_len = geom["batch"], geom["seq"], geom["ctx"]
    d = xf.shape[1]
    heads = MLSTM_HEADS
    d_m = p["mlstm_norm_g"].shape[0]
    dh = d_m // heads
    att_dh = p["q_norm_g"].shape[0]
    n_qh, n_kvh = ATT_QH, ATT_KVH
    group = n_qh // n_kvh
    d_q, d_kv = n_qh * att_dh, n_kvh * att_dh
    n_gate = 4 * heads
    ctx_tiles, lat_tiles = ctx_len // tm, seq // tm
    per_b = ctx_tiles + lat_tiles
    n_lat_tiles = n_batch * lat_tiles

    w = p["cd_w_in"]
    cuts = np.cumsum([0, d_m, d_m, d_m, d_m, n_gate, d_q, d_kv, d_kv])
    w_qm, w_km, w_vm, w_om, w_gt, w_qa, w_ka, w_va = (w[:, cuts[i]:cuts[i + 1]] for i in range(8))
    w_qx = jnp.zeros((d, n_qh, LANES), F32)
    q_gain = jnp.zeros((n_qh, LANES), F32)
    w_att = jnp.zeros((n_qh, LANES, d), F32)
    for hq in range(n_qh):
        lo = (hq // group) * att_dh
        w_qx = w_qx.at[:, hq, lo:lo + att_dh].set(w_qa[:, hq * att_dh:(hq + 1) * att_dh])
        q_gain = q_gain.at[hq, lo:lo + att_dh].set(p["q_norm_g"] * att_dh ** -0.5)
        w_att = w_att.at[hq, lo:lo + att_dh, :].set(p["cd_w_out"][d_m + hq * att_dh:d_m + (hq + 1) * att_dh])
    d_qx = n_qh * LANES
    w_all = jnp.concatenate([w_qm, w_km, w_vm, w_om, w_qx.reshape(d, d_qx), w_ka, w_va,
                             jnp.pad(w_gt, ((0, 0), (0, LANES - n_gate)))], axis=1)
    gate_bias = jnp.pad(p["mlstm_gate_b"].reshape(1, n_gate), ((0, 0), (0, LANES - n_gate)))
    cos_t, sin_t = _rope_tables(seq, ctx_len, att_dh)

    def in_tile(t):
        b, r = t // per_b, t % per_b
        return jnp.where(r < ctx_tiles, n_lat_tiles + b * ctx_tiles + r, b * lat_tiles + r - ctx_tiles)

    def stream_seq(t):
        return jnp.where(t % per_b < ctx_tiles, n_batch, t // per_b)

    qm, km, vm, om, qx, ka, va, gt = _cd_in(
        xf, mod_l, w_all, cos_t, sin_t, q_gain.reshape(1, d_qx), jnp.tile(p["k_norm_g"], n_kvh).reshape(1, d_kv),
        tm, n_batch * per_b, in_tile, stream_seq, lambda t: t % per_b, d_m, d_qx, d_kv, att_dh, dh ** -0.5)

    chunks_per_seq = (ctx_len + seq) // MLSTM_CHUNK
    hf, hb = _mlstm(qm, km, vm, gt, gate_bias, n_batch, chunks_per_seq, ctx_len // MLSTM_CHUNK, heads, dh)
    att = _attention(qx, ka, va, tm, n_batch, lat_tiles, lambda b, t: b * per_b + ctx_tiles + t,
                     ctx_len + seq)

    def seq_tile_of(t):
        return (t // lat_tiles) * per_b + ctx_tiles + t % lat_tiles

    x1 = _cd_tail(hf, hb, om, att, xf, mod_l, p["mlstm_norm_g"], p["cd_w_out"][:d_m], w_att.reshape(d_qx, d),
                  p["ln_g"][0], p["ln_b"][0], tm, n_lat_tiles, seq_tile_of, lambda t: t // lat_tiles,
                  heads, dh, geom["dn_alpha"])
    tp = geom["tm_peer"]
    lat_per_seq_p = seq // tp
    return _peer_layer(x1, mod_l, p["peer_w_q"], p["peer_keys"], p["peer_u"], p["peer_v"],
                       p["ln_g"][1], p["ln_b"][1], tp, lambda t: t // lat_per_seq_p, geom["dn_alpha"])


def kernel(x, c, ctx, c_ctx, mod_w, mod_b, ln_g, ln_b, ab_w_in, pool_w, pool_ls, conv_w, conv_b,
           conv_ln_g, conv_ln_b, ab_w_out, cd_w_in, mlstm_gate_b, mlstm_norm_g, q_norm_g, k_norm_g,
           cd_w_out, peer_w_q, peer_keys, peer_u, peer_v):
    n_batch, seq, d = x.shape
    ctx_len = ctx.shape[1]
    depth = mod_w.shape[0]
    assert depth == 2, "one pooling/convolution layer followed by one mLSTM/attention layer"
    n_streams = SUBLANES * (-(-(n_batch + 1) // SUBLANES))
    cc = jnp.concatenate([c, c_ctx[None], jnp.zeros((n_streams - n_batch - 1, d), F32)], axis=0)
    mod = _modulation(cc, mod_w, mod_b)
    geom = dict(tm_mix=256, tm_peer=512, n_lat=n_batch * seq, seq=seq, ctx=ctx_len, batch=n_batch,
                dn_alpha=(2 * depth) ** 0.25)
    xf = jnp.concatenate([x.reshape(-1, d), ctx.reshape(-1, d)], axis=0)
    p0 = dict(ab_w_in=ab_w_in[0], pool_w=pool_w[0], pool_ls=pool_ls[0], conv_w=conv_w[0],
              conv_b=conv_b[0], conv_ln_g=conv_ln_g[0], conv_ln_b=conv_ln_b[0], ab_w_out=ab_w_out[0],
              peer_w_q=peer_w_q[0], peer_keys=peer_keys[0], peer_u=peer_u[0], peer_v=peer_v[0],
              ln_g=ln_g[0], ln_b=ln_b[0])
    xf = _layer0(xf, mod[0], p0, geom)
    p1 = dict(cd_w_in=cd_w_in[0], mlstm_gate_b=mlstm_gate_b[0], mlstm_norm_g=mlstm_norm_g[0],
              q_norm_g=q_norm_g[0], k_norm_g=k_norm_g[0], cd_w_out=cd_w_out[0],
              peer_w_q=peer_w_q[1], peer_keys=peer_keys[1], peer_u=peer_u[1], peer_v=peer_v[1],
              ln_g=ln_g[1], ln_b=ln_b[1])
    out = _layer1(xf, mod[1], p1, geom)
    return out.reshape(n_batch, seq, d)
```

```python
import functools
import math

import jax
import jax.numpy as jnp
import numpy as np
from jax import lax
from jax.experimental import pallas as pl
from jax.experimental.pallas import tpu as pltpu

F32 = jnp.float32
BF16 = jnp.bfloat16

LANES = 128
SUBLANES = 8
VMEM_LIMIT_BYTES = 56 * 1024 * 1024

GRID_W = 64
N_MOD = 6
POOL_WINDOWS = (2, 4, 8, 16)
CONV_W = 31
HALO = 16
MLSTM_HEADS = 4
MLSTM_CHUNK = 128
ATT_QH = 8
ATT_KVH = 2
ROPE_THETA = 10000.0
PEER_HEADS = 8
PEER_NKEYS = 128
PEER_TOPK = 16
LN_EPS = 1e-5
RMS_EPS = 1e-6
NEG_INF = float("-inf")

SH1, SC1, G1, SH2, SC2, G2 = range(N_MOD)


def _cparams(*sem, flags=None):
    return pltpu.CompilerParams(dimension_semantics=sem, vmem_limit_bytes=VMEM_LIMIT_BYTES, flags=flags)


def _bdot(a, b):
    return jnp.dot(a.astype(BF16), b.astype(BF16), preferred_element_type=F32)


def _bdot_nt(a, b):
    return lax.dot_general(a.astype(BF16), b.astype(BF16), (((1,), (1,)), ((), ())),
                           preferred_element_type=F32)


def _layer_norm_rows(v, g, b):
    mu = jnp.mean(v, axis=-1, keepdims=True)
    d = v - mu
    var = jnp.mean(d * d, axis=-1, keepdims=True)
    return d * lax.rsqrt(var + LN_EPS) * g + b


def _mod_kernel(c_ref, w_ref, b_ref, o_ref):
    c = c_ref[...]
    a = c * jax.nn.sigmoid(c)
    o_ref[...] = jnp.dot(a, w_ref[...], preferred_element_type=F32,
                         precision=lax.Precision.HIGHEST) + b_ref[...]


def _modulation(cc, mod_w, mod_b):
    depth, d, n = mod_w.shape
    rows = cc.shape[0]
    bn = d
    out = pl.pallas_call(
        _mod_kernel,
        grid=(depth, n // bn),
        in_specs=[
            pl.BlockSpec((rows, d), lambda l, j: (0, 0)),
            pl.BlockSpec((None, d, bn), lambda l, j: (l, 0, j)),
            pl.BlockSpec((None, 1, bn), lambda l, j: (l, 0, j)),
        ],
        out_specs=pl.BlockSpec((None, rows, bn), lambda l, j: (l, 0, j)),
        out_shape=jax.ShapeDtypeStruct((depth, rows, n), F32),
        compiler_params=_cparams("parallel", "parallel"),
        name="modulation",
    )(cc, mod_w, mod_b.reshape(depth, 1, n))
    return out.reshape(depth, rows, N_MOD, d)


def _modmm_kernel(x_ref, mod_ref, w_ref, *o_refs, splits):
    h = x_ref[...] * (1.0 + mod_ref[SC1:SC1 + 1, :]) + mod_ref[SH1:SH1 + 1, :]
    z = _bdot(h, w_ref[...])
    off = 0
    for o_ref, n in zip(o_refs, splits):
        o_ref[...] = z[:, off:off + n].astype(o_ref.dtype)
        off += n


def _modulated_matmul(x, mod_l, w, splits, out_dtypes, tm, in_tile, stream_of_tile, n_tiles):
    d = x.shape[1]
    n = w.shape[1]
    assert sum(splits) == n
    return pl.pallas_call(
        functools.partial(_modmm_kernel, splits=splits),
        grid=(n_tiles,),
        in_specs=[
            pl.BlockSpec((tm, d), lambda t: (in_tile(t), 0)),
            pl.BlockSpec((None, N_MOD, d), lambda t: (stream_of_tile(t), 0, 0)),
            pl.BlockSpec((d, n), lambda t: (0, 0)),
        ],
        out_specs=[pl.BlockSpec((tm, s), lambda t: (t, 0)) for s in splits],
        out_shape=[jax.ShapeDtypeStruct((n_tiles * tm, s), dt) for s, dt in zip(splits, out_dtypes)],
        compiler_params=_cparams("parallel"),
        name="modulated_matmul",
    )(x, mod_l, w.astype(BF16))


def _ab_tail_kernel(zp_ref, zm_ref, zn_ref, x_ref, mod_ref, pw_ref, pls_ref, cw_ref, cb_ref,
                    cg_ref, cbb_ref, wo_ref, lg_ref, lb_ref, o_ref, zpool, ubuf, ymix,
                    *, tm, d_pool, d_conv, n_lat_tiles, lat_tiles_per_seq, ctx_tiles_per_seq,
                    dn_alpha, row_chunk):
    t = pl.program_id(0)
    is_lat = t < n_lat_tiles
    per_seq = jnp.where(is_lat, lat_tiles_per_seq, ctx_tiles_per_seq)
    pos_tile = jnp.where(is_lat, t, t - n_lat_tiles) % per_seq
    first = pos_tile == 0
    last = pos_tile == per_seq - 1
    seq_len = per_seq * tm

    def glu(z):
        return z[:, d_pool:d_pool + d_conv] * jax.nn.sigmoid(z[:, d_pool + d_conv:])

    zp = jnp.where(first, 0.0, zp_ref[...])
    zn = jnp.where(last, 0.0, zn_ref[...])
    zm = zm_ref[...]
    zpool[0:HALO, :] = zp[:, :d_pool]
    zpool[HALO:HALO + tm, :] = zm[:, :d_pool]
    zpool[HALO + tm:, :] = zn[:, :d_pool]
    ubuf[0:HALO, :] = glu(zp)
    ubuf[HALO:HALO + tm, :] = glu(zm)
    ubuf[HALO + tm:, :] = glu(zn)

    gw = d_pool // len(POOL_WINDOWS)
    tpos = pos_tile * tm + lax.broadcasted_iota(jnp.int32, (tm, gw), 0)
    for g, w in enumerate(POOL_WINDOWS):
        cols = slice(g * gw, (g + 1) * gw)
        acc = zpool[HALO - w // 2:HALO - w // 2 + tm, cols]
        for s in range(1, w):
            acc = acc + zpool[HALO - w // 2 + s:HALO - w // 2 + s + tm, cols]
        cnt = jnp.minimum(tpos + w // 2, seq_len) - jnp.maximum(tpos - w // 2, 0)
        diff = acc / cnt.astype(F32) - zpool[HALO:HALO + tm, cols]
        ymix[:, cols] = (_bdot(diff, pw_ref[g]) * pls_ref[:, cols]).astype(BF16)

    half = CONV_W // 2
    for r in range(0, tm, row_chunk):
        base = HALO - half + r
        acc = ubuf[base:base + row_chunk, :] * cw_ref[0:1, :]
        for k in range(1, CONV_W):
            acc = acc + ubuf[base + k:base + k + row_chunk, :] * cw_ref[k:k + 1, :]
        yn = _layer_norm_rows(acc + cb_ref[...], cg_ref[...], cbb_ref[...])
        ymix[r:r + row_chunk, d_pool:] = (yn * jax.nn.sigmoid(yn)).astype(BF16)

    y = jnp.dot(ymix[...], wo_ref[...], preferred_element_type=F32)
    v = dn_alpha * x_ref[...] + mod_ref[G1:G1 + 1, :] * y
    o_ref[...] = _layer_norm_rows(v, lg_ref[...], lb_ref[...])


def _ab_tail(z, x, mod_l, pool_w, pool_ls, conv_w, conv_b, cln_g, cln_b, w_out, ln_g, ln_b,
             tm, n_lat_tiles, lat_tiles_per_seq, ctx_tiles_per_seq, stream_of_tile, dn_alpha):
    ntok, d = x.shape
    d_pool = pool_ls.shape[0]
    d_conv = conv_b.shape[0]
    n_tiles = ntok // tm
    hb = tm // HALO
    n_hblocks = ntok // HALO
    row = lambda a: a.reshape(1, -1)
    kern = functools.partial(
        _ab_tail_kernel, tm=tm, d_pool=d_pool, d_conv=d_conv, n_lat_tiles=n_lat_tiles,
        lat_tiles_per_seq=lat_tiles_per_seq, ctx_tiles_per_seq=ctx_tiles_per_seq,
        dn_alpha=dn_alpha, row_chunk=32)
    const2 = lambda t: (0, 0)
    return pl.pallas_call(
        kern,
        grid=(n_tiles,),
        in_specs=[
            pl.BlockSpec((HALO, z.shape[1]), lambda t: (jnp.maximum(t * hb - 1, 0), 0)),
            pl.BlockSpec((tm, z.shape[1]), lambda t: (t, 0)),
            pl.BlockSpec((HALO, z.shape[1]), lambda t: (jnp.minimum((t + 1) * hb, n_hblocks - 1), 0)),
            pl.BlockSpec((tm, d), lambda t: (t, 0)),
            pl.BlockSpec((None, N_MOD, d), lambda t: (stream_of_tile(t), 0, 0)),
            pl.BlockSpec(pool_w.shape, lambda t: (0, 0, 0)),
            pl.BlockSpec((1, d_pool), const2),
            pl.BlockSpec(conv_w.shape, const2),
            pl.BlockSpec((1, d_conv), const2),
            pl.BlockSpec((1, d_conv), const2),
            pl.BlockSpec((1, d_conv), const2),
            pl.BlockSpec(w_out.shape, const2),
            pl.BlockSpec((1, d), const2),
            pl.BlockSpec((1, d), const2),
        ],
        out_specs=pl.BlockSpec((tm, d), lambda t: (t, 0)),
        out_shape=jax.ShapeDtypeStruct((ntok, d), F32),
        scratch_shapes=[
            pltpu.VMEM((tm + 2 * HALO, d_pool), F32),
            pltpu.VMEM((tm + 2 * HALO, d_conv), F32),
            pltpu.VMEM((tm, d_pool + d_conv), BF16),
        ],
        compiler_params=_cparams("parallel"),
        name="ab_tail",
    )(z, z, z, x, mod_l, pool_w.astype(BF16), row(pool_ls), conv_w, row(conv_b), row(cln_g),
      row(cln_b), w_out.astype(BF16), row(ln_g), row(ln_b))


def _paired_lanes(lt, tm):
    shifted = (lt + 1) % (tm // LANES)
    return slice(shifted * LANES, (shifted + 1) * LANES)


def _top_values(s, k):
    outs = []
    cur = s
    for it in range(k):
        m = jnp.max(cur, axis=0, keepdims=True)
        outs.append(m)
        if it + 1 < k:
            cur = jnp.where(cur >= m, NEG_INF, cur)
    return outs


def _peer_route_kernel(x_ref, mod_ref, wq_ref, keys_ref, a_ref, nd_ref, c_ref, bn_ref, qt_ref,
                       *, tm, half):
    h = x_ref[...] * (1.0 + mod_ref[SC2:SC2 + 1, :]) + mod_ref[SH2:SH2 + 1, :]
    qt_ref[...] = _bdot_nt(wq_ref[...], h).astype(BF16)
    k = PEER_TOPK
    for hd in range(PEER_HEADS):
        for lt in range(tm // LANES):
            lanes = slice(lt * LANES, (lt + 1) * LANES)
            r0 = hd * 2 * half
            s1 = jnp.dot(keys_ref[0], qt_ref[r0:r0 + half, lanes], preferred_element_type=F32)
            s2 = jnp.dot(keys_ref[1], qt_ref[r0 + half:r0 + 2 * half, lanes],
                         preferred_element_type=F32)
            t1 = _top_values(s1, k)
            t2 = _top_values(s2, k)
            t2_all = jnp.concatenate(t2, axis=0)
            t1_tail = jnp.concatenate(t1[k // 2:], axis=0)
            pieces = []
            for a in range(k // 2):
                nb = min(k, (k + 1) // (a + 1))
                pieces.append(t1[a] + t2_all[:nb, :])
            pieces.append(t1_tail + t2[0])
            n_cand = sum(p.shape[0] for p in pieces)
            pad = (-n_cand) % SUBLANES
            if pad:
                pieces.append(jnp.full((pad, LANES), NEG_INF, F32))
            cand = jnp.concatenate(pieces, axis=0)
            top = _top_values(cand, k + 1)
            kth = top[k - 1]
            thr = 0.5 * (kth + jnp.maximum(top[k], kth - 1.0))
            m1, m2 = t1[0], t2[0]
            z = jnp.sum(jnp.where(cand >= thr, jnp.exp(cand - (m1 + m2)), 0.0), axis=0, keepdims=True)
            a_ref[hd, :, lanes] = jnp.where(s1 >= t1[k - 1], jnp.exp(s1 - m1), 0.0)
            nd_ref[hd, :, lanes] = -s1
            c_ref[hd, :, lanes] = s2 - thr
            bn_ref[hd, :, _paired_lanes(lt, tm)] = jnp.where(s2 >= t2[k - 1], jnp.exp(s2 - m2), 0.0) / z


def _peer_route(x, mod_l, w_q, keys, tm, stream_of_tile):
    ntok, d = x.shape
    nq = w_q.shape[1]
    half = keys.shape[2]
    nk = keys.shape[1]
    n_tiles = ntok // tm
    sel_spec = pl.BlockSpec((PEER_HEADS, nk, tm), lambda t: (0, 0, t))
    return pl.pallas_call(
        functools.partial(_peer_route_kernel, tm=tm, half=half),
        grid=(n_tiles,),
        in_specs=[
            pl.BlockSpec((tm, d), lambda t: (t, 0)),
            pl.BlockSpec((None, N_MOD, d), lambda t: (stream_of_tile(t), 0, 0)),
            pl.BlockSpec((nq, d), lambda t: (0, 0)),
            pl.BlockSpec(keys.shape, lambda t: (0, 0, 0)),
        ],
        out_specs=[sel_spec] * 4,
        out_shape=[jax.ShapeDtypeStruct((PEER_HEADS, nk, ntok), F32)] * 4,
        scratch_shapes=[pltpu.VMEM((nq, tm), BF16)],
        compiler_params=_cparams("parallel"),
        name="peer_route",
    )(x, mod_l, w_q.T.astype(BF16), keys.astype(BF16))


def _gelu_exact(a):
    return 0.5 * a * (1.0 + lax.erf(a * (1.0 / math.sqrt(2.0))))


def _peer_dense_kernel(x_ref, mod_ref, a_ref, nd_ref, c_ref, bn_ref, u_next_ref, vt_prev_ref, u_first_ref,
                       vt_last_ref, lg_ref, lb_ref, o_ref, hbf, acc, wt, at,
                       *, tm, groups, fk, nk, dn_alpha, row_chunk, rc_block):
    e = pl.program_id(1)
    assert groups % 2 == 0

    @pl.when(e == 0)
    def _():
        h = x_ref[...] * (1.0 + mod_ref[SC2:SC2 + 1, :]) + mod_ref[SH2:SH2 + 1, :]
        hbf[...] = h.T.astype(BF16)
        acc[...] = jnp.zeros_like(acc)
        at[0] = jnp.dot(u_first_ref[0], hbf[...], preferred_element_type=F32)
        wt[1] = jnp.zeros(wt.shape[1:], BF16)

    def group_step(p, carry):
        slot = p % 2
        at[1 - slot] = jnp.dot(u_next_ref[p], hbf[...], preferred_element_type=F32)
        acc[...] += jnp.dot(vt_prev_ref[p], wt[1 - slot], preferred_element_type=F32)
        for lt in range(tm // LANES):
            lanes = slice(lt * LANES, (lt + 1) * LANES)
            for rb in range(0, nk // row_chunk, rc_block):
                gates = [[None] * rc_block for _ in range(fk)]
                for hd in range(PEER_HEADS):
                    a_rows = [jnp.broadcast_to(a_ref[hd, p * fk + q, :, lanes], (row_chunk, LANES))
                              for q in range(fk)]
                    nd_rows = [jnp.broadcast_to(nd_ref[hd, p * fk + q, :, lanes], (row_chunk, LANES))
                               for q in range(fk)]
                    for r in range(rc_block):
                        rows = slice((rb + r) * row_chunk, (rb + r + 1) * row_chunk)
                        cv = c_ref[hd, rows, lanes]
                        bv = bn_ref[hd, rows, _paired_lanes(lt, tm)]
                        for q in range(fk):
                            g = jnp.where(cv >= nd_rows[q], bv * a_rows[q], 0.0)
                            gates[q][r] = g if gates[q][r] is None else gates[q][r] + g
                for q in range(fk):
                    for r in range(rc_block):
                        erows = slice(q * nk + (rb + r) * row_chunk, q * nk + (rb + r + 1) * row_chunk)
                        wt[slot, erows, lanes] = (_gelu_exact(at[slot, erows, lanes]) * gates[q][r]).astype(BF16)
        return carry

    lax.fori_loop(0, groups, group_step, 0)

    @pl.when(e == pl.num_programs(1) - 1)
    def _():
        acc[...] += jnp.dot(vt_last_ref[0], wt[(groups - 1) % 2], preferred_element_type=F32)
        f = acc[...].T
        v = dn_alpha * x_ref[...] + mod_ref[G2:G2 + 1, :] * f
        o_ref[...] = _layer_norm_rows(v, lg_ref[...], lb_ref[...])


PEER_FIRST_KEYS_PER_PASS = 4
PEER_GROUPS_PER_STEP = 4
PEER_ROW_CHUNKS_PER_BLOCK = 2
PEER_DENSE_FLAGS = None


def _peer_dense(x, mod_l, sel, u_tab, v_tab, ln_g, ln_b, tm, stream_of_tile, dn_alpha):
    ntok, d = x.shape
    n_exp = u_tab.shape[0]
    a_sel, nd_sel, c_sel, bn_sel = sel
    nk = c_sel.shape[1]
    n_tiles = ntok // tm
    fk, groups = PEER_FIRST_KEYS_PER_PASS, PEER_GROUPS_PER_STEP
    ge = fk * nk
    n_groups = n_exp // ge
    a_sel = a_sel.reshape(PEER_HEADS, nk, 1, ntok)
    nd_sel = nd_sel.reshape(PEER_HEADS, nk, 1, ntok)
    u3 = u_tab.astype(BF16).reshape(n_groups, ge, d)
    vt3 = v_tab.astype(BF16).reshape(n_groups, ge, d).transpose(0, 2, 1)
    u_next = jnp.roll(u3, -1, axis=0)
    vt_prev = jnp.roll(vt3, 1, axis=0)
    row_spec = pl.BlockSpec((PEER_HEADS, groups * fk, 1, tm), lambda t, e: (0, e, 0, t))
    full_spec = pl.BlockSpec((PEER_HEADS, nk, tm), lambda t, e: (0, 0, t))
    kern = functools.partial(_peer_dense_kernel, tm=tm, groups=groups, fk=fk, nk=nk, dn_alpha=dn_alpha,
                             row_chunk=2 * SUBLANES, rc_block=PEER_ROW_CHUNKS_PER_BLOCK)
    return pl.pallas_call(
        kern,
        grid=(n_tiles, n_groups // groups),
        in_specs=[
            pl.BlockSpec((tm, d), lambda t, e: (t, 0)),
            pl.BlockSpec((None, N_MOD, d), lambda t, e: (stream_of_tile(t), 0, 0)),
            row_spec, row_spec, full_spec, full_spec,
            pl.BlockSpec((groups, ge, d), lambda t, e: (e, 0, 0)),
            pl.BlockSpec((groups, d, ge), lambda t, e: (e, 0, 0)),
            pl.BlockSpec((1, ge, d), lambda t, e: (0, 0, 0)),
            pl.BlockSpec((1, d, ge), lambda t, e: (n_groups - 1, 0, 0)),
            pl.BlockSpec((1, d), lambda t, e: (0, 0)),
            pl.BlockSpec((1, d), lambda t, e: (0, 0)),
        ],
        out_specs=pl.BlockSpec((tm, d), lambda t, e: (t, 0)),
        out_shape=jax.ShapeDtypeStruct((ntok, d), F32),
        scratch_shapes=[
            pltpu.VMEM((d, tm), BF16),
            pltpu.VMEM((d, tm), F32),
            pltpu.VMEM((2, ge, tm), BF16),
            pltpu.VMEM((2, ge, tm), F32),
        ],
        compiler_params=_cparams("parallel", "arbitrary", flags=PEER_DENSE_FLAGS),
        name="peer_dense",
    )(x, mod_l, a_sel, nd_sel, c_sel, bn_sel, u_next, vt_prev, u3, vt3, ln_g.reshape(1, d), ln_b.reshape(1, d))


def _peer_layer(x, mod_l, w_q, keys, u_tab, v_tab, ln_g, ln_b, tm, stream_of_tile, dn_alpha):
    sel = _peer_route(x, mod_l, w_q, keys, tm, stream_of_tile)
    return _peer_dense(x, mod_l, sel, u_tab, v_tab, ln_g, ln_b, tm, stream_of_tile, dn_alpha)


def _rope(v, cos, sin_signed):
    n = v.shape[-1]
    lane = lax.broadcasted_iota(jnp.int32, v.shape, v.ndim - 1)
    quarter = sin_signed.shape[-1] // 8
    partner = jnp.where(lane % (2 * quarter) < quarter,
                        pltpu.roll(v, n - quarter, v.ndim - 1), pltpu.roll(v, quarter, v.ndim - 1))
    reps = n // cos.shape[-1]
    if reps > 1:
        cos = jnp.concatenate([cos] * reps, axis=-1)
        sin_signed = jnp.concatenate([sin_signed] * reps, axis=-1)
    return v * cos + partner * sin_signed


def _cd_in_kernel(x_ref, mod_ref, w_ref, cos_ref, sin_ref, qg_ref, kg_ref,
                  qm_ref, km_ref, vm_ref, om_ref, qx_ref, ka_ref, va_ref, gt_ref,
                  *, d_m, d_qx, d_kv, att_dh, k_scale):
    h = x_ref[...] * (1.0 + mod_ref[SC1:SC1 + 1, :]) + mod_ref[SH1:SH1 + 1, :]
    z = _bdot(h, w_ref[...])
    qm_ref[...] = z[:, 0:d_m].astype(BF16)
    km_ref[...] = (z[:, d_m:2 * d_m] * k_scale).astype(BF16)
    vm_ref[...] = z[:, 2 * d_m:3 * d_m].astype(BF16)
    om_ref[...] = z[:, 3 * d_m:4 * d_m]
    off = 4 * d_m
    cos = cos_ref[...]
    sin = sin_ref[...]
    qx = z[:, off:off + d_qx]
    pieces = []
    for hq in range(d_qx // LANES):
        blk = qx[:, hq * LANES:(hq + 1) * LANES]
        ms = jnp.sum(blk * blk, axis=-1, keepdims=True) * (1.0 / att_dh)
        pieces.append(blk * lax.rsqrt(ms + RMS_EPS))
    qn = jnp.concatenate(pieces, axis=-1) * qg_ref[...]
    qx_ref[...] = _rope(qn, cos, sin).astype(BF16)
    off += d_qx
    kk = z[:, off:off + d_kv]
    lane = lax.broadcasted_iota(jnp.int32, kk.shape, 1)
    sq = kk * kk
    pieces = []
    for hk in range(d_kv // att_dh):
        sel = (lane >= hk * att_dh) & (lane < (hk + 1) * att_dh)
        ms = jnp.sum(jnp.where(sel, sq, 0.0), axis=-1, keepdims=True) * (1.0 / att_dh)
        pieces.append((sel, lax.rsqrt(ms + RMS_EPS)))
    scale = jnp.zeros_like(kk)
    for sel, r in pieces:
        scale = jnp.where(sel, r, scale)
    ka_ref[...] = _rope(kk * scale * kg_ref[...], cos, sin).astype(BF16)
    off += d_kv
    va_ref[...] = z[:, off:off + d_kv].astype(BF16)
    off += d_kv
    gt_ref[...] = z[:, off:]


def _cd_in(x, mod_l, w, cos_t, sin_t, q_gain, k_gain, tm, n_tiles, in_tile, stream_of_tile, pos_tile,
           d_m, d_qx, d_kv, att_dh, k_scale):
    d = x.shape[1]
    n = w.shape[1]
    ntok = n_tiles * tm
    widths = (d_m, d_m, d_m, d_m, d_qx, d_kv, d_kv, n - 4 * d_m - d_qx - 2 * d_kv)
    dtypes = (BF16, BF16, BF16, F32, BF16, BF16, BF16, F32)
    kern = functools.partial(_cd_in_kernel, d_m=d_m, d_qx=d_qx, d_kv=d_kv, att_dh=att_dh, k_scale=k_scale)
    return pl.pallas_call(
        kern,
        grid=(n_tiles,),
        in_specs=[
            pl.BlockSpec((tm, d), lambda t: (in_tile(t), 0)),
            pl.BlockSpec((None, N_MOD, d), lambda t: (stream_of_tile(t), 0, 0)),
            pl.BlockSpec((d, n), lambda t: (0, 0)),
            pl.BlockSpec((tm, cos_t.shape[1]), lambda t: (pos_tile(t), 0)),
            pl.BlockSpec((tm, sin_t.shape[1]), lambda t: (pos_tile(t), 0)),
            pl.BlockSpec((1, d_qx), lambda t: (0, 0)),
            pl.BlockSpec((1, d_kv), lambda t: (0, 0)),
        ],
        out_specs=[pl.BlockSpec((tm, wd), lambda t: (t, 0)) for wd in widths],
        out_shape=[jax.ShapeDtypeStruct((ntok, wd), dt) for wd, dt in zip(widths, dtypes)],
        compiler_params=_cparams("parallel"),
        name="cd_in",
    )(x, mod_l, w.astype(BF16), cos_t, sin_t, q_gain, k_gain)


def _mlstm_kernel(qf_ref, kf_ref, vf_ref, gf_ref, qb_ref, kb_ref, vb_ref, gb_ref, bias_ref,
                  hf_ref, hb_ref, c_s, n_s, m_s, *, chunk, dh, heads):
    s = pl.program_id(1)

    @pl.when(s == 0)
    def _():
        c_s[...] = jnp.zeros_like(c_s)
        n_s[...] = jnp.zeros_like(n_s)
        m_s[...] = jnp.zeros_like(m_s)

    ri = lax.broadcasted_iota(jnp.int32, (chunk, chunk), 0)
    cj = lax.broadcasted_iota(jnp.int32, (chunk, chunk), 1)
    streams = ((qf_ref, kf_ref, vf_ref, gf_ref, hf_ref), (qb_ref, kb_ref, vb_ref, gb_ref, hb_ref))
    for direction, (q_ref, k_ref, v_ref, g_ref, h_ref) in enumerate(streams):
        mask = (cj <= ri) if direction == 0 else (cj >= ri)
        edge = chunk - 1 if direction == 0 else 0
        gates = g_ref[...] + bias_ref[...]
        logf = jax.nn.log_sigmoid(gates)
        bc = jnp.dot(mask.astype(F32), logf, preferred_element_type=F32, precision=lax.Precision.HIGHEST)
        br = bc.T
        gr = gates.T
        for hd in range(heads):
            ci = (2 * direction) * heads + hd
            cf = (2 * direction + 1) * heads + hd
            idx = direction * heads + hd
            b_col = bc[:, cf:cf + 1]
            b_row = br[cf:cf + 1, :]
            ig_row = gr[ci:ci + 1, :]
            ig_col = gates[:, ci:ci + 1]
            m_prev = m_s[idx][:, 0:1]
            dmat = jnp.where(mask, b_col - b_row + ig_row, NEG_INF)
            inter = b_col + m_prev
            m_t = jnp.maximum(inter, jnp.max(dmat, axis=-1, keepdims=True))
            dexp = jnp.exp(dmat - m_t)
            w_inter = jnp.exp(inter - m_t)
            cols = slice(hd * dh, (hd + 1) * dh)
            q = q_ref[:, cols]
            k = k_ref[:, cols]
            v = v_ref[:, cols]
            c_prev = c_s[idx]
            n_prev = n_s[idx]
            sm = _bdot_nt(q, k) * dexp
            num = _bdot(sm, v) + w_inter * _bdot(q, c_prev)
            den = (jnp.sum(sm, axis=-1, keepdims=True)
                   + w_inter * jnp.sum(q.astype(F32) * n_prev, axis=-1, keepdims=True))
            h_ref[:, cols] = num / jnp.maximum(jnp.abs(den), jnp.exp(-m_t))
            b_last = bc[edge:edge + 1, cf:cf + 1]
            g_log = b_last - b_col + ig_col
            m_new = jnp.maximum(b_last + m_prev, jnp.max(g_log, axis=0, keepdims=True))
            wk = jnp.exp(g_log - m_new)
            decay = jnp.exp(b_last + m_prev - m_new)
            kw = k.astype(F32) * wk
            c_s[idx] = decay * c_prev + lax.dot_general(
                kw.astype(BF16), v, (((0,), (0,)), ((), ())), preferred_element_type=F32)
            n_s[idx] = decay * n_prev + jnp.sum(kw, axis=0, keepdims=True)
            m_s[idx] = jnp.broadcast_to(m_new, (1, LANES))


def _mlstm(qm, km, vm, gt, gate_bias, n_batch, chunks_per_seq, ctx_chunks, heads, dh):
    chunk = MLSTM_CHUNK
    ntok, dm = qm.shape
    ng = gt.shape[1]

    def fwd(b, s):
        return (b * chunks_per_seq + s, 0)

    def bwd(b, s):
        r = jnp.where(s < ctx_chunks, ctx_chunks - 1 - s, ctx_chunks + chunks_per_seq - 1 - s)
        return (b * chunks_per_seq + r, 0)

    tok_f = pl.BlockSpec((chunk, dm), fwd)
    tok_b = pl.BlockSpec((chunk, dm), bwd)
    kern = functools.partial(_mlstm_kernel, chunk=chunk, dh=dh, heads=heads)
    return pl.pallas_call(
        kern,
        grid=(n_batch, chunks_per_seq),
        in_specs=[tok_f, tok_f, tok_f, pl.BlockSpec((chunk, ng), fwd),
                  tok_b, tok_b, tok_b, pl.BlockSpec((chunk, ng), bwd),
                  pl.BlockSpec((1, ng), lambda b, s: (0, 0))],
        out_specs=[pl.BlockSpec((chunk, dm), fwd), pl.BlockSpec((chunk, dm), bwd)],
        out_shape=[jax.ShapeDtypeStruct((ntok, dm), F32)] * 2,
        scratch_shapes=[
            pltpu.VMEM((2 * heads, dh, dh), F32),
            pltpu.VMEM((2 * heads, 1, dh), F32),
            pltpu.VMEM((2 * heads, 1, LANES), F32),
        ],
        compiler_params=_cparams("parallel", "arbitrary"),
        name="mlstm",
    )(qm, km, vm, gt, qm, km, vm, gt, gate_bias)


def _attn_kernel(q_ref, k_ref, v_ref, o_ref, *, n_qh):
    k = k_ref[...]
    v = v_ref[...]
    for hq in range(n_qh):
        cols = slice(hq * LANES, (hq + 1) * LANES)
        s = _bdot_nt(q_ref[:, cols], k)
        p = jnp.exp(s - jnp.max(s, axis=-1, keepdims=True))
        l = jnp.sum(p, axis=-1, keepdims=True)
        o = jnp.dot(p.astype(BF16), v, preferred_element_type=F32)
        o_ref[:, cols] = (o / l).astype(o_ref.dtype)


def _attention(qx, ka, va, tq, n_batch, seq_tiles, q_tile_of, kv_len):
    n_qh = qx.shape[1] // LANES
    return pl.pallas_call(
        functools.partial(_attn_kernel, n_qh=n_qh),
        grid=(n_batch, seq_tiles),
        in_specs=[
            pl.BlockSpec((tq, qx.shape[1]), lambda b, t: (q_tile_of(b, t), 0)),
            pl.BlockSpec((kv_len, ka.shape[1]), lambda b, t: (b, 0)),
            pl.BlockSpec((kv_len, va.shape[1]), lambda b, t: (b, 0)),
        ],
        out_specs=pl.BlockSpec((tq, qx.shape[1]), lambda b, t: (b * seq_tiles + t, 0)),
        out_shape=jax.ShapeDtypeStruct((n_batch * seq_tiles * tq, qx.shape[1]), BF16),
        compiler_params=_cparams("parallel", "parallel"),
        name="attention",
    )(qx, ka, va)


def _cd_tail_kernel(hf_ref, hb_ref, om_ref, att_ref, x_ref, mod_ref, ng_ref, wm_ref, wa_ref, lg_ref, lb_ref,
                    o_ref, *, dh, heads, dn_alpha):
    hsum = hf_ref[...] + hb_ref[...]
    pieces = []
    for hd in range(heads):
        cols = slice(hd * dh, (hd + 1) * dh)
        blk = hsum[:, cols]
        mu = jnp.mean(blk, axis=-1, keepdims=True)
        dlt = blk - mu
        var = jnp.mean(dlt * dlt, axis=-1, keepdims=True)
        pieces.append(dlt * lax.rsqrt(var + LN_EPS))
    hn = jnp.concatenate(pieces, axis=-1) * ng_ref[...] * jax.nn.sigmoid(om_ref[...])
    y = _bdot(hn, wm_ref[...]) + jnp.dot(att_ref[...], wa_ref[...], preferred_element_type=F32)
    v = dn_alpha * x_ref[...] + mod_ref[G1:G1 + 1, :] * y
    o_ref[...] = _layer_norm_rows(v, lg_ref[...], lb_ref[...])


def _cd_tail(hf, hb, om, att, x, mod_l, norm_g, w_m, w_a, ln_g, ln_b, tm, n_tiles, seq_tile_of,
             stream_of_tile, heads, dh, dn_alpha):
    d = x.shape[1]
    dm = hf.shape[1]
    const2 = lambda t: (0, 0)
    kern = functools.partial(_cd_tail_kernel, dh=dh, heads=heads, dn_alpha=dn_alpha)
    return pl.pallas_call(
        kern,
        grid=(n_tiles,),
        in_specs=[
            pl.BlockSpec((tm, dm), lambda t: (seq_tile_of(t), 0)),
            pl.BlockSpec((tm, dm), lambda t: (seq_tile_of(t), 0)),
            pl.BlockSpec((tm, dm), lambda t: (seq_tile_of(t), 0)),
            pl.BlockSpec((tm, att.shape[1]), lambda t: (t, 0)),
            pl.BlockSpec((tm, d), lambda t: (t, 0)),
            pl.BlockSpec((None, N_MOD, d), lambda t: (stream_of_tile(t), 0, 0)),
            pl.BlockSpec((1, dm), const2),
            pl.BlockSpec(w_m.shape, const2),
            pl.BlockSpec(w_a.shape, const2),
            pl.BlockSpec((1, d), const2),
            pl.BlockSpec((1, d), const2),
        ],
        out_specs=pl.BlockSpec((tm, d), lambda t: (t, 0)),
        out_shape=jax.ShapeDtypeStruct((n_tiles * tm, d), F32),
        compiler_params=_cparams("parallel"),
        name="cd_tail",
    )(hf, hb, om, att, x, mod_l, norm_g.reshape(1, dm), w_m.astype(BF16), w_a.astype(BF16),
      ln_g.reshape(1, d), ln_b.reshape(1, d))


def _layer0(xf, mod_l, p, geom):
    tm = geom["tm_mix"]
    n_tiles = xf.shape[0] // tm
    n_lat_tiles = geom["n_lat"] // tm
    lat_per_seq = geom["seq"] // tm
    ctx_per_seq = geom["ctx"] // tm
    n_batch = geom["batch"]

    def stream(t):
        return jnp.where(t < n_lat_tiles, t // lat_per_seq, n_batch)

    (z,) = _modulated_matmul(xf, mod_l, p["ab_w_in"], (p["ab_w_in"].shape[1],), (F32,), tm,
                             lambda t: t, stream, n_tiles)
    x1 = _ab_tail(z, xf, mod_l, p["pool_w"], p["pool_ls"], p["conv_w"], p["conv_b"], p["conv_ln_g"],
                  p["conv_ln_b"], p["ab_w_out"], p["ln_g"][0], p["ln_b"][0], tm, n_lat_tiles,
                  lat_per_seq, ctx_per_seq, stream, geom["dn_alpha"])
    tp = geom["tm_peer"]
    n_lat_p = geom["n_lat"] // tp
    lat_per_seq_p = geom["seq"] // tp

    def stream_p(t):
        return jnp.where(t < n_lat_p, t // lat_per_seq_p, n_batch)

    return _peer_layer(x1, mod_l, p["peer_w_q"], p["peer_keys"], p["peer_u"], p["peer_v"],
                       p["ln_g"][1], p["ln_b"][1], tp, stream_p, geom["dn_alpha"])


def _rope_tables(seq, ctx_len, att_dh):
    n_freq = att_dh // 4
    t = jnp.arange(seq)
    freqs = ROPE_THETA ** (-jnp.arange(n_freq, dtype=F32) / n_freq)
    ar = (t // GRID_W).astype(F32)[:, None] * freqs
    ac = (t % GRID_W).astype(F32)[:, None] * freqs
    cos = jnp.concatenate([jnp.cos(ar), jnp.cos(ar), jnp.cos(ac), jnp.cos(ac)], axis=-1)
    sin = jnp.concatenate([-jnp.sin(ar), jnp.sin(ar), -jnp.sin(ac), jnp.sin(ac)], axis=-1)
    cos = jnp.concatenate([jnp.ones((ctx_len, att_dh), F32), cos], axis=0)
    sin = jnp.concatenate([jnp.zeros((ctx_len, att_dh), F32), sin], axis=0)
    reps = LANES // att_dh
    return jnp.tile(cos, (1, reps)), jnp.tile(sin, (1, reps))


def _layer1(xf, mod_l, p, geom):
    tm = geom["tm_mix"]
    n_batch, seq, ctx_len = geom["batch"], geom["seq"], geom["ctx"]
    d = xf.shape[1]
    heads = MLSTM_HEADS
    d_m = p["mlstm_norm_g"].shape[0]
    dh = d_m // heads
    att_dh = p["q_norm_g"].shape[0]
    n_qh, n_kvh = ATT_QH, ATT_KVH
    group = n_qh // n_kvh
    d_q, d_kv = n_qh * att_dh, n_kvh * att_dh
    n_gate = 4 * heads
    ctx_tiles, lat_tiles = ctx_len // tm, seq // tm
    per_b = ctx_tiles + lat_tiles
    n_lat_tiles = n_batch * lat_tiles

    w = p["cd_w_in"]
    cuts = np.cumsum([0, d_m, d_m, d_m, d_m, n_gate, d_q, d_kv, d_kv])
    w_qm, w_km, w_vm, w_om, w_gt, w_qa, w_ka, w_va = (w[:, cuts[i]:cuts[i + 1]] for i in range(8))
    w_qx = jnp.zeros((d, n_qh, LANES), F32)
    q_gain = jnp.zeros((n_qh, LANES), F32)
    w_att = jnp.zeros((n_qh, LANES, d), F32)
    for hq in range(n_qh):
        lo = (hq // group) * att_dh
        w_qx = w_qx.at[:, hq, lo:lo + att_dh].set(w_qa[:, hq * att_dh:(hq + 1) * att_dh])
        q_gain = q_gain.at[hq, lo:lo + att_dh].set(p["q_norm_g"] * att_dh ** -0.5)
        w_att = w_att.at[hq, lo:lo + att_dh, :].set(p["cd_w_out"][d_m + hq * att_dh:d_m + (hq + 1) * att_dh])
    d_qx = n_qh * LANES
    w_all = jnp.concatenate([w_qm, w_km, w_vm, w_om, w_qx.reshape(d, d_qx), w_ka, w_va,
                             jnp.pad(w_gt, ((0, 0), (0, LANES - n_gate)))], axis=1)
    gate_bias = jnp.pad(p["mlstm_gate_b"].reshape(1, n_gate), ((0, 0), (0, LANES - n_gate)))
    cos_t, sin_t = _rope_tables(seq, ctx_len, att_dh)

    def in_tile(t):
        b, r = t // per_b, t % per_b
        return jnp.where(r < ctx_tiles, n_lat_tiles + b * ctx_tiles + r, b * lat_tiles + r - ctx_tiles)

    def stream_seq(t):
        return jnp.where(t % per_b < ctx_tiles, n_batch, t // per_b)

    qm, km, vm, om, qx, ka, va, gt = _cd_in(
        xf, mod_l, w_all, cos_t, sin_t, q_gain.reshape(1, d_qx), jnp.tile(p["k_norm_g"], n_kvh).reshape(1, d_kv),
        tm, n_batch * per_b, in_tile, stream_seq, lambda t: t % per_b, d_m, d_qx, d_kv, att_dh, dh ** -0.5)

    chunks_per_seq = (ctx_len + seq) // MLSTM_CHUNK
    hf, hb = _mlstm(qm, km, vm, gt, gate_bias, n_batch, chunks_per_seq, ctx_len // MLSTM_CHUNK, heads, dh)
    att = _attention(qx, ka, va, tm, n_batch, lat_tiles, lambda b, t: b * per_b + ctx_tiles + t,
                     ctx_len + seq)

    def seq_tile_of(t):
        return (t // lat_tiles) * per_b + ctx_tiles + t % lat_tiles

    x1 = _cd_tail(hf, hb, om, att, xf, mod_l, p["mlstm_norm_g"], p["cd_w_out"][:d_m], w_att.reshape(d_qx, d),
                  p["ln_g"][0], p["ln_b"][0], tm, n_lat_tiles, seq_tile_of, lambda t: t // lat_tiles,
                  heads, dh, geom["dn_alpha"])
    tp = geom["tm_peer"]
    lat_per_seq_p = seq // tp
    return _peer_layer(x1, mod_l, p["peer_w_q"], p["peer_keys"], p["peer_u"], p["peer_v"],
                       p["ln_g"][1], p["ln_b"][1], tp, lambda t: t // lat_per_seq_p, geom["dn_alpha"])


def kernel(x, c, ctx, c_ctx, mod_w, mod_b, ln_g, ln_b, ab_w_in, pool_w, pool_ls, conv_w, conv_b,
           conv_ln_g, conv_ln_b, ab_w_out, cd_w_in, mlstm_gate_b, mlstm_norm_g, q_norm_g, k_norm_g,
           cd_w_out, peer_w_q, peer_keys, peer_u, peer_v):
    n_batch, seq, d = x.shape
    ctx_len = ctx.shape[1]
    depth = mod_w.shape[0]
    assert depth == 2, "one pooling/convolution layer followed by one mLSTM/attention layer"
    n_streams = SUBLANES * (-(-(n_batch + 1) // SUBLANES))
    cc = jnp.concatenate([c, c_ctx[None], jnp.zeros((n_streams - n_batch - 1, d), F32)], axis=0)
    mod = _modulation(cc, mod_w, mod_b)
    geom = dict(tm_mix=256, tm_peer=512, n_lat=n_batch * seq, seq=seq, ctx=ctx_len, batch=n_batch,
                dn_alpha=(2 * depth) ** 0.25)
    xf = jnp.concatenate([x.reshape(-1, d), ctx.reshape(-1, d)], axis=0)
    p0 = dict(ab_w_in=ab_w_in[0], pool_w=pool_w[0], pool_ls=pool_ls[0], conv_w=conv_w[0],
              conv_b=conv_b[0], conv_ln_g=conv_ln_g[0], conv_ln_b=conv_ln_b[0], ab_w_out=ab_w_out[0],
              peer_w_q=peer_w_q[0], peer_keys=peer_keys[0], peer_u=peer_u[0], peer_v=peer_v[0],
              ln_g=ln_g[0], ln_b=ln_b[0])
    xf = _layer0(xf, mod[0], p0, geom)
    p1 = dict(cd_w_in=cd_w_in[0], mlstm_gate_b=mlstm_gate_b[0], mlstm_norm_g=mlstm_norm_g[0],
              q_norm_g=q_norm_g[0], k_norm_g=k_norm_g[0], cd_w_out=cd_w_out[0],
              peer_w_q=peer_w_q[1], peer_keys=peer_keys[1], peer_u=peer_u[1], peer_v=peer_v[1],
              ln_g=ln_g[1], ln_b=ln_b[1])
    out = _layer1(xf, mod[1], p1, geom)
    return out.reshape(n_batch, seq, d)
```

```python
import functools
import math

import jax
import jax.numpy as jnp
import numpy as np
from jax import lax
from jax.experimental import pallas as pl
from jax.experimental.pallas import tpu as pltpu

F32 = jnp.float32
BF16 = jnp.bfloat16

LANES = 128
SUBLANES = 8
VMEM_LIMIT_BYTES = 56 * 1024 * 1024

GRID_W = 64
N_MOD = 6
POOL_WINDOWS = (2, 4, 8, 16)
CONV_W = 31
HALO = 16
MLSTM_HEADS = 4
MLSTM_CHUNK = 128
ATT_QH = 8
ATT_KVH = 2
ROPE_THETA = 10000.0
PEER_HEADS = 8
PEER_NKEYS = 128
PEER_TOPK = 16
LN_EPS = 1e-5
RMS_EPS = 1e-6
NEG_INF = float("-inf")

SH1, SC1, G1, SH2, SC2, G2 = range(N_MOD)


def _cparams(*sem, flags=None):
    return pltpu.CompilerParams(dimension_semantics=sem, vmem_limit_bytes=VMEM_LIMIT_BYTES, flags=flags)


def _bdot(a, b):
    return jnp.dot(a.astype(BF16), b.astype(BF16), preferred_element_type=F32)


def _bdot_nt(a, b):
    return lax.dot_general(a.astype(BF16), b.astype(BF16), (((1,), (1,)), ((), ())),
                           preferred_element_type=F32)


def _layer_norm_rows(v, g, b):
    mu = jnp.mean(v, axis=-1, keepdims=True)
    d = v - mu
    var = jnp.mean(d * d, axis=-1, keepdims=True)
    return d * lax.rsqrt(var + LN_EPS) * g + b


def _mod_kernel(c_ref, w_ref, b_ref, o_ref):
    c = c_ref[...]
    a = c * jax.nn.sigmoid(c)
    o_ref[...] = jnp.dot(a, w_ref[...], preferred_element_type=F32,
                         precision=lax.Precision.HIGHEST) + b_ref[...]


def _modulation(cc, mod_w, mod_b):
    depth, d, n = mod_w.shape
    rows = cc.shape[0]
    bn = d
    out = pl.pallas_call(
        _mod_kernel,
        grid=(depth, n // bn),
        in_specs=[
            pl.BlockSpec((rows, d), lambda l, j: (0, 0)),
            pl.BlockSpec((None, d, bn), lambda l, j: (l, 0, j)),
            pl.BlockSpec((None, 1, bn), lambda l, j: (l, 0, j)),
        ],
        out_specs=pl.BlockSpec((None, rows, bn), lambda l, j: (l, 0, j)),
        out_shape=jax.ShapeDtypeStruct((depth, rows, n), F32),
        compiler_params=_cparams("parallel", "parallel"),
        name="modulation",
    )(cc, mod_w, mod_b.reshape(depth, 1, n))
    return out.reshape(depth, rows, N_MOD, d)


def _modmm_kernel(x_ref, mod_ref, w_ref, *o_refs, splits):
    h = x_ref[...] * (1.0 + mod_ref[SC1:SC1 + 1, :]) + mod_ref[SH1:SH1 + 1, :]
    z = _bdot(h, w_ref[...])
    off = 0
    for o_ref, n in zip(o_refs, splits):
        o_ref[...] = z[:, off:off + n].astype(o_ref.dtype)
        off += n


def _modulated_matmul(x, mod_l, w, splits, out_dtypes, tm, in_tile, stream_of_tile, n_tiles):
    d = x.shape[1]
    n = w.shape[1]
    assert sum(splits) == n
    return pl.pallas_call(
        functools.partial(_modmm_kernel, splits=splits),
        grid=(n_tiles,),
        in_specs=[
            pl.BlockSpec((tm, d), lambda t: (in_tile(t), 0)),
            pl.BlockSpec((None, N_MOD, d), lambda t: (stream_of_tile(t), 0, 0)),
            pl.BlockSpec((d, n), lambda t: (0, 0)),
        ],
        out_specs=[pl.BlockSpec((tm, s), lambda t: (t, 0)) for s in splits],
        out_shape=[jax.ShapeDtypeStruct((n_tiles * tm, s), dt) for s, dt in zip(splits, out_dtypes)],
        compiler_params=_cparams("parallel"),
        name="modulated_matmul",
    )(x, mod_l, w.astype(BF16))


def _ab_tail_kernel(zp_ref, zm_ref, zn_ref, x_ref, mod_ref, pw_ref, pls_ref, cw_ref, cb_ref,
                    cg_ref, cbb_ref, wo_ref, lg_ref, lb_ref, o_ref, zpool, ubuf, ymix, ushift,
                    *, tm, d_pool, d_conv, n_lat_tiles, lat_tiles_per_seq, ctx_tiles_per_seq,
                    dn_alpha, row_chunk):
    t = pl.program_id(0)
    is_lat = t < n_lat_tiles
    per_seq = jnp.where(is_lat, lat_tiles_per_seq, ctx_tiles_per_seq)
    pos_tile = jnp.where(is_lat, t, t - n_lat_tiles) % per_seq
    first = pos_tile == 0
    last = pos_tile == per_seq - 1
    seq_len = per_seq * tm

    def glu(z):
        return z[:, d_pool:d_pool + d_conv] * jax.nn.sigmoid(z[:, d_pool + d_conv:])

    zp = jnp.where(first, 0.0, zp_ref[...])
    zn = jnp.where(last, 0.0, zn_ref[...])
    zm = zm_ref[...]
    zpool[0:HALO, :] = zp[:, :d_pool]
    zpool[HALO:HALO + tm, :] = zm[:, :d_pool]
    zpool[HALO + tm:, :] = zn[:, :d_pool]
    ubuf[0:HALO, :] = glu(zp)
    ubuf[HALO:HALO + tm, :] = glu(zm)
    ubuf[HALO + tm:2 * HALO + tm, :] = glu(zn)

    gw = d_pool // len(POOL_WINDOWS)
    tpos = pos_tile * tm + lax.broadcasted_iota(jnp.int32, (tm, gw), 0)
    for g, w in enumerate(POOL_WINDOWS):
        cols = slice(g * gw, (g + 1) * gw)
        acc = zpool[HALO - w // 2:HALO - w // 2 + tm, cols]
        for s in range(1, w):
            acc = acc + zpool[HALO - w // 2 + s:HALO - w // 2 + s + tm, cols]
        cnt = jnp.minimum(tpos + w // 2, seq_len) - jnp.maximum(tpos - w // 2, 0)
        diff = acc / cnt.astype(F32) - zpool[HALO:HALO + tm, cols]
        ymix[:, cols] = (_bdot(diff, pw_ref[g]) * pls_ref[:, cols]).astype(BF16)

    half = CONV_W // 2
    span = tm + 2 * HALO
    ubuf[span:, :] = jnp.zeros((SUBLANES, d_conv), F32)
    for s in range(SUBLANES):
        ushift[s] = ubuf[s:s + span, :]
    for r in range(0, tm, row_chunk):
        base = HALO - half + r
        acc = None
        for k in range(CONV_W):
            s = (base + k) % SUBLANES
            lo = base + k - s
            term = ushift[s, lo:lo + row_chunk, :] * cw_ref[k:k + 1, :]
            acc = term if acc is None else acc + term
        yn = _layer_norm_rows(acc + cb_ref[...], cg_ref[...], cbb_ref[...])
        ymix[r:r + row_chunk, d_pool:] = (yn * jax.nn.sigmoid(yn)).astype(BF16)

    y = jnp.dot(ymix[...], wo_ref[...], preferred_element_type=F32)
    v = dn_alpha * x_ref[...] + mod_ref[G1:G1 + 1, :] * y
    o_ref[...] = _layer_norm_rows(v, lg_ref[...], lb_ref[...])


def _ab_tail(z, x, mod_l, pool_w, pool_ls, conv_w, conv_b, cln_g, cln_b, w_out, ln_g, ln_b,
             tm, n_lat_tiles, lat_tiles_per_seq, ctx_tiles_per_seq, stream_of_tile, dn_alpha):
    ntok, d = x.shape
    d_pool = pool_ls.shape[0]
    d_conv = conv_b.shape[0]
    n_tiles = ntok // tm
    hb = tm // HALO
    n_hblocks = ntok // HALO
    row = lambda a: a.reshape(1, -1)
    kern = functools.partial(
        _ab_tail_kernel, tm=tm, d_pool=d_pool, d_conv=d_conv, n_lat_tiles=n_lat_tiles,
        lat_tiles_per_seq=lat_tiles_per_seq, ctx_tiles_per_seq=ctx_tiles_per_seq,
        dn_alpha=dn_alpha, row_chunk=32)
    const2 = lambda t: (0, 0)
    return pl.pallas_call(
        kern,
        grid=(n_tiles,),
        in_specs=[
            pl.BlockSpec((HALO, z.shape[1]), lambda t: (jnp.maximum(t * hb - 1, 0), 0)),
            pl.BlockSpec((tm, z.shape[1]), lambda t: (t, 0)),
            pl.BlockSpec((HALO, z.shape[1]), lambda t: (jnp.minimum((t + 1) * hb, n_hblocks - 1), 0)),
            pl.BlockSpec((tm, d), lambda t: (t, 0)),
            pl.BlockSpec((None, N_MOD, d), lambda t: (stream_of_tile(t), 0, 0)),
            pl.BlockSpec(pool_w.shape, lambda t: (0, 0, 0)),
            pl.BlockSpec((1, d_pool), const2),
            pl.BlockSpec(conv_w.shape, const2),
            pl.BlockSpec((1, d_conv), const2),
            pl.BlockSpec((1, d_conv), const2),
            pl.BlockSpec((1, d_conv), const2),
            pl.BlockSpec(w_out.shape, const2),
            pl.BlockSpec((1, d), const2),
            pl.BlockSpec((1, d), const2),
        ],
        out_specs=pl.BlockSpec((tm, d), lambda t: (t, 0)),
        out_shape=jax.ShapeDtypeStruct((ntok, d), F32),
        scratch_shapes=[
            pltpu.VMEM((tm + 2 * HALO, d_pool), F32),
            pltpu.VMEM((tm + 2 * HALO + SUBLANES, d_conv), F32),
            pltpu.VMEM((tm, d_pool + d_conv), BF16),
            pltpu.VMEM((SUBLANES, tm + 2 * HALO, d_conv), F32),
        ],
        compiler_params=_cparams("parallel"),
        name="ab_tail",
    )(z, z, z, x, mod_l, pool_w.astype(BF16), row(pool_ls), conv_w, row(conv_b), row(cln_g),
      row(cln_b), w_out.astype(BF16), row(ln_g), row(ln_b))


def _paired_lanes(lt, tm):
    shifted = (lt + 1) % (tm // LANES)
    return slice(shifted * LANES, (shifted + 1) * LANES)


def _top_values(s, k):
    outs = []
    cur = s
    for it in range(k):
        m = jnp.max(cur, axis=0, keepdims=True)
        outs.append(m)
        if it + 1 < k:
            cur = jnp.where(cur >= m, NEG_INF, cur)
    return outs


def _peer_route_kernel(x_ref, mod_ref, wq_ref, keys_ref, a_ref, nd_ref, c_ref, bn_ref, qt_ref,
                       *, tm, half):
    h = x_ref[...] * (1.0 + mod_ref[SC2:SC2 + 1, :]) + mod_ref[SH2:SH2 + 1, :]
    qt_ref[...] = _bdot_nt(wq_ref[...], h).astype(BF16)
    k = PEER_TOPK
    for hd in range(PEER_HEADS):
        for lt in range(tm // LANES):
            lanes = slice(lt * LANES, (lt + 1) * LANES)
            r0 = hd * 2 * half
            s1 = jnp.dot(keys_ref[0], qt_ref[r0:r0 + half, lanes], preferred_element_type=F32)
            s2 = jnp.dot(keys_ref[1], qt_ref[r0 + half:r0 + 2 * half, lanes],
                         preferred_element_type=F32)
            t1 = _top_values(s1, k)
            t2 = _top_values(s2, k)
            t2_all = jnp.concatenate(t2, axis=0)
            t1_tail = jnp.concatenate(t1[k // 2:], axis=0)
            pieces = []
            for a in range(k // 2):
                nb = min(k, (k + 1) // (a + 1))
                pieces.append(t1[a] + t2_all[:nb, :])
            pieces.append(t1_tail + t2[0])
            n_cand = sum(p.shape[0] for p in pieces)
            pad = (-n_cand) % SUBLANES
            if pad:
                pieces.append(jnp.full((pad, LANES), NEG_INF, F32))
            cand = jnp.concatenate(pieces, axis=0)
            top = _top_values(cand, k + 1)
            kth = top[k - 1]
            thr = 0.5 * (kth + jnp.maximum(top[k], kth - 1.0))
            m1, m2 = t1[0], t2[0]
            z = jnp.sum(jnp.where(cand >= thr, jnp.exp(cand - (m1 + m2)), 0.0), axis=0, keepdims=True)
            a_ref[hd, :, lanes] = jnp.where(s1 >= t1[k - 1], jnp.exp(s1 - m1), 0.0)
            nd_ref[hd, :, lanes] = -s1
            c_ref[hd, :, lanes] = s2 - thr
            bn_ref[hd, :, _paired_lanes(lt, tm)] = jnp.where(s2 >= t2[k - 1], jnp.exp(s2 - m2), 0.0) / (2.0 * z)


def _peer_route(x, mod_l, w_q, keys, tm, stream_of_tile):
    ntok, d = x.shape
    nq = w_q.shape[1]
    half = keys.shape[2]
    nk = keys.shape[1]
    n_tiles = ntok // tm
    sel_spec = pl.BlockSpec((PEER_HEADS, nk, tm), lambda t: (0, 0, t))
    return pl.pallas_call(
        functools.partial(_peer_route_kernel, tm=tm, half=half),
        grid=(n_tiles,),
        in_specs=[
            pl.BlockSpec((tm, d), lambda t: (t, 0)),
            pl.BlockSpec((None, N_MOD, d), lambda t: (stream_of_tile(t), 0, 0)),
            pl.BlockSpec((nq, d), lambda t: (0, 0)),
            pl.BlockSpec(keys.shape, lambda t: (0, 0, 0)),
        ],
        out_specs=[sel_spec] * 4,
        out_shape=[jax.ShapeDtypeStruct((PEER_HEADS, nk, ntok), F32)] * 4,
        scratch_shapes=[pltpu.VMEM((nq, tm), BF16)],
        compiler_params=_cparams("parallel"),
        name="peer_route",
    )(x, mod_l, w_q.T.astype(BF16), keys.astype(BF16))


def _gelu_twice(a):
    return a * (1.0 + lax.erf(a * (1.0 / math.sqrt(2.0))))


def _peer_dense_kernel(x_ref, mod_ref, a_ref, nd_ref, c_ref, bn_ref, u_next_ref, vt_prev_ref, u_first_ref,
                       vt_last_ref, lg_ref, lb_ref, o_ref, hbf, acc, wt, at,
                       *, tm, groups, fk, nk, dn_alpha, row_chunk, rc_block):
    e = pl.program_id(1)
    assert groups % 2 == 0

    @pl.when(e == 0)
    def _():
        h = x_ref[...] * (1.0 + mod_ref[SC2:SC2 + 1, :]) + mod_ref[SH2:SH2 + 1, :]
        hbf[...] = h.T.astype(BF16)
        acc[...] = jnp.zeros_like(acc)
        at[0] = jnp.dot(u_first_ref[0], hbf[...], preferred_element_type=F32)
        wt[1] = jnp.zeros(wt.shape[1:], BF16)

    def group_step(p, carry):
        slot = p % 2
        at[1 - slot] = jnp.dot(u_next_ref[p], hbf[...], preferred_element_type=F32)
        acc[...] += jnp.dot(vt_prev_ref[p], wt[1 - slot], preferred_element_type=F32)
        for lt in range(tm // LANES):
            lanes = slice(lt * LANES, (lt + 1) * LANES)
            for rb in range(0, nk // row_chunk, rc_block):
                gates = [[None] * rc_block for _ in range(fk)]
                for hd in range(PEER_HEADS):
                    a_rows = [jnp.broadcast_to(a_ref[hd, p * fk + q, :, lanes], (row_chunk, LANES))
                              for q in range(fk)]
                    nd_rows = [jnp.broadcast_to(nd_ref[hd, p * fk + q, :, lanes], (row_chunk, LANES))
                               for q in range(fk)]
                    for r in range(rc_block):
                        rows = slice((rb + r) * row_chunk, (rb + r + 1) * row_chunk)
                        cv = c_ref[hd, rows, lanes]
                        bv = bn_ref[hd, rows, _paired_lanes(lt, tm)]
                        for q in range(fk):
                            g = jnp.where(cv >= nd_rows[q], bv * a_rows[q], 0.0)
                            gates[q][r] = g if gates[q][r] is None else gates[q][r] + g
                for q in range(fk):
                    for r in range(rc_block):
                        erows = slice(q * nk + (rb + r) * row_chunk, q * nk + (rb + r + 1) * row_chunk)
                        wt[slot, erows, lanes] = (_gelu_twice(at[slot, erows, lanes]) * gates[q][r]).astype(BF16)
        return carry

    lax.fori_loop(0, groups, group_step, 0)

    @pl.when(e == pl.num_programs(1) - 1)
    def _():
        acc[...] += jnp.dot(vt_last_ref[0], wt[(groups - 1) % 2], preferred_element_type=F32)
        f = acc[...].T
        v = dn_alpha * x_ref[...] + mod_ref[G2:G2 + 1, :] * f
        o_ref[...] = _layer_norm_rows(v, lg_ref[...], lb_ref[...])


PEER_FIRST_KEYS_PER_PASS = 4
PEER_GROUPS_PER_STEP = 4
PEER_ROW_CHUNKS_PER_BLOCK = 2
PEER_DENSE_FLAGS = None


def _peer_dense(x, mod_l, sel, u_tab, v_tab, ln_g, ln_b, tm, stream_of_tile, dn_alpha):
    ntok, d = x.shape
    n_exp = u_tab.shape[0]
    a_sel, nd_sel, c_sel, bn_sel = sel
    nk = c_sel.shape[1]
    n_tiles = ntok // tm
    fk, groups = PEER_FIRST_KEYS_PER_PASS, PEER_GROUPS_PER_STEP
    ge = fk * nk
    n_groups = n_exp // ge
    a_sel = a_sel.reshape(PEER_HEADS, nk, 1, ntok)
    nd_sel = nd_sel.reshape(PEER_HEADS, nk, 1, ntok)
    u3 = u_tab.astype(BF16).reshape(n_groups, ge, d)
    vt3 = v_tab.astype(BF16).reshape(n_groups, ge, d).transpose(0, 2, 1)
    u_next = jnp.roll(u3, -1, axis=0)
    vt_prev = jnp.roll(vt3, 1, axis=0)
    row_spec = pl.BlockSpec((PEER_HEADS, groups * fk, 1, tm), lambda t, e: (0, e, 0, t))
    full_spec = pl.BlockSpec((PEER_HEADS, nk, tm), lambda t, e: (0, 0, t))
    kern = functools.partial(_peer_dense_kernel, tm=tm, groups=groups, fk=fk, nk=nk, dn_alpha=dn_alpha,
                             row_chunk=2 * SUBLANES, rc_block=PEER_ROW_CHUNKS_PER_BLOCK)
    return pl.pallas_call(
        kern,
        grid=(n_tiles, n_groups // groups),
        in_specs=[
            pl.BlockSpec((tm, d), lambda t, e: (t, 0)),
            pl.BlockSpec((None, N_MOD, d), lambda t, e: (stream_of_tile(t), 0, 0)),
            row_spec, row_spec, full_spec, full_spec,
            pl.BlockSpec((groups, ge, d), lambda t, e: (e, 0, 0)),
            pl.BlockSpec((groups, d, ge), lambda t, e: (e, 0, 0)),
            pl.BlockSpec((1, ge, d), lambda t, e: (0, 0, 0)),
            pl.BlockSpec((1, d, ge), lambda t, e: (n_groups - 1, 0, 0)),
            pl.BlockSpec((1, d), lambda t, e: (0, 0)),
            pl.BlockSpec((1, d), lambda t, e: (0, 0)),
        ],
        out_specs=pl.BlockSpec((tm, d), lambda t, e: (t, 0)),
        out_shape=jax.ShapeDtypeStruct((ntok, d), F32),
        scratch_shapes=[
            pltpu.VMEM((d, tm), BF16),
            pltpu.VMEM((d, tm), F32),
            pltpu.VMEM((2, ge, tm), BF16),
            pltpu.VMEM((2, ge, tm), F32),
        ],
        compiler_params=_cparams("parallel", "arbitrary", flags=PEER_DENSE_FLAGS),
        name="peer_dense",
    )(x, mod_l, a_sel, nd_sel, c_sel, bn_sel, u_next, vt_prev, u3, vt3, ln_g.reshape(1, d), ln_b.reshape(1, d))


def _peer_layer(x, mod_l, w_q, keys, u_tab, v_tab, ln_g, ln_b, tm, stream_of_tile, dn_alpha):
    sel = _peer_route(x, mod_l, w_q, keys, tm, stream_of_tile)
    return _peer_dense(x, mod_l, sel, u_tab, v_tab, ln_g, ln_b, tm, stream_of_tile, dn_alpha)


def _rope(v, cos, sin_signed):
    n = v.shape[-1]
    lane = lax.broadcasted_iota(jnp.int32, v.shape, v.ndim - 1)
    quarter = sin_signed.shape[-1] // 8
    partner = jnp.where(lane % (2 * quarter) < quarter,
                        pltpu.roll(v, n - quarter, v.ndim - 1), pltpu.roll(v, quarter, v.ndim - 1))
    reps = n // cos.shape[-1]
    if reps > 1:
        cos = jnp.concatenate([cos] * reps, axis=-1)
        sin_signed = jnp.concatenate([sin_signed] * reps, axis=-1)
    return v * cos + partner * sin_signed


def _cd_in_kernel(x_ref, mod_ref, w_ref, cos_ref, sin_ref, qg_ref, kg_ref,
                  qm_ref, km_ref, vm_ref, om_ref, qx_ref, ka_ref, va_ref, gt_ref,
                  *, d_m, d_qx, d_kv, att_dh, k_scale):
    h = x_ref[...] * (1.0 + mod_ref[SC1:SC1 + 1, :]) + mod_ref[SH1:SH1 + 1, :]
    z = _bdot(h, w_ref[...])
    qm_ref[...] = z[:, 0:d_m].astype(BF16)
    km_ref[...] = (z[:, d_m:2 * d_m] * k_scale).astype(BF16)
    vm_ref[...] = z[:, 2 * d_m:3 * d_m].astype(BF16)
    om_ref[...] = z[:, 3 * d_m:4 * d_m]
    off = 4 * d_m
    cos = cos_ref[...]
    sin = sin_ref[...]
    qx = z[:, off:off + d_qx]
    pieces = []
    for hq in range(d_qx // LANES):
        blk = qx[:, hq * LANES:(hq + 1) * LANES]
        ms = jnp.sum(blk * blk, axis=-1, keepdims=True) * (1.0 / att_dh)
        pieces.append(blk * lax.rsqrt(ms + RMS_EPS))
    qn = jnp.concatenate(pieces, axis=-1) * qg_ref[...]
    qx_ref[...] = _rope(qn, cos, sin).astype(BF16)
    off += d_qx
    kk = z[:, off:off + d_kv]
    lane = lax.broadcasted_iota(jnp.int32, kk.shape, 1)
    sq = kk * kk
    pieces = []
    for hk in range(d_kv // att_dh):
        sel = (lane >= hk * att_dh) & (lane < (hk + 1) * att_dh)
        ms = jnp.sum(jnp.where(sel, sq, 0.0), axis=-1, keepdims=True) * (1.0 / att_dh)
        pieces.append((sel, lax.rsqrt(ms + RMS_EPS)))
    scale = jnp.zeros_like(kk)
    for sel, r in pieces:
        scale = jnp.where(sel, r, scale)
    ka_ref[...] = _rope(kk * scale * kg_ref[...], cos, sin).astype(BF16)
    off += d_kv
    va_ref[...] = z[:, off:off + d_kv].astype(BF16)
    off += d_kv
    gt_ref[...] = z[:, off:]


def _cd_in(x, mod_l, w, cos_t, sin_t, q_gain, k_gain, tm, n_tiles, in_tile, stream_of_tile, pos_tile,
           d_m, d_qx, d_kv, att_dh, k_scale):
    d = x.shape[1]
    n = w.shape[1]
    ntok = n_tiles * tm
    widths = (d_m, d_m, d_m, d_m, d_qx, d_kv, d_kv, n - 4 * d_m - d_qx - 2 * d_kv)
    dtypes = (BF16, BF16, BF16, F32, BF16, BF16, BF16, F32)
    kern = functools.partial(_cd_in_kernel, d_m=d_m, d_qx=d_qx, d_kv=d_kv, att_dh=att_dh, k_scale=k_scale)
    return pl.pallas_call(
        kern,
        grid=(n_tiles,),
        in_specs=[
            pl.BlockSpec((tm, d), lambda t: (in_tile(t), 0)),
            pl.BlockSpec((None, N_MOD, d), lambda t: (stream_of_tile(t), 0, 0)),
            pl.BlockSpec((d, n), lambda t: (0, 0)),
            pl.BlockSpec((tm, cos_t.shape[1]), lambda t: (pos_tile(t), 0)),
            pl.BlockSpec((tm, sin_t.shape[1]), lambda t: (pos_tile(t), 0)),
            pl.BlockSpec((1, d_qx), lambda t: (0, 0)),
            pl.BlockSpec((1, d_kv), lambda t: (0, 0)),
        ],
        out_specs=[pl.BlockSpec((tm, wd), lambda t: (t, 0)) for wd in widths],
        out_shape=[jax.ShapeDtypeStruct((ntok, wd), dt) for wd, dt in zip(widths, dtypes)],
        compiler_params=_cparams("parallel"),
        name="cd_in",
    )(x, mod_l, w.astype(BF16), cos_t, sin_t, q_gain, k_gain)


def _mlstm_kernel(qf_ref, kf_ref, vf_ref, gf_ref, qb_ref, kb_ref, vb_ref, gb_ref, bias_ref,
                  hf_ref, hb_ref, c_s, n_s, m_s, *, chunk, dh, heads):
    s = pl.program_id(1)

    @pl.when(s == 0)
    def _():
        c_s[...] = jnp.zeros_like(c_s)
        n_s[...] = jnp.zeros_like(n_s)
        m_s[...] = jnp.zeros_like(m_s)

    ri = lax.broadcasted_iota(jnp.int32, (chunk, chunk), 0)
    cj = lax.broadcasted_iota(jnp.int32, (chunk, chunk), 1)
    streams = ((qf_ref, kf_ref, vf_ref, gf_ref, hf_ref), (qb_ref, kb_ref, vb_ref, gb_ref, hb_ref))
    for direction, (q_ref, k_ref, v_ref, g_ref, h_ref) in enumerate(streams):
        mask = (cj <= ri) if direction == 0 else (cj >= ri)
        edge = chunk - 1 if direction == 0 else 0
        gates = g_ref[...] + bias_ref[...]
        logf = jax.nn.log_sigmoid(gates)
        bc = jnp.dot(mask.astype(F32), logf, preferred_element_type=F32, precision=lax.Precision.HIGHEST)
        br = bc.T
        gr = gates.T
        for hd in range(heads):
            ci = (2 * direction) * heads + hd
            cf = (2 * direction + 1) * heads + hd
            idx = direction * heads + hd
            b_col = bc[:, cf:cf + 1]
            b_row = br[cf:cf + 1, :]
            ig_row = gr[ci:ci + 1, :]
            ig_col = gates[:, ci:ci + 1]
            m_prev = m_s[idx][:, 0:1]
            dmat = jnp.where(mask, b_col - b_row + ig_row, NEG_INF)
            inter = b_col + m_prev
            m_t = jnp.maximum(inter, jnp.max(dmat, axis=-1, keepdims=True))
            dexp = jnp.exp(dmat - m_t)
            w_inter = jnp.exp(inter - m_t)
            cols = slice(hd * dh, (hd + 1) * dh)
            q = q_ref[:, cols]
            k = k_ref[:, cols]
            v = v_ref[:, cols]
            c_prev = c_s[idx]
            n_prev = n_s[idx]
            sm = _bdot_nt(q, k) * dexp
            num = _bdot(sm, v) + w_inter * _bdot(q, c_prev)
            den = (jnp.sum(sm, axis=-1, keepdims=True)
                   + w_inter * jnp.sum(q.astype(F32) * n_prev, axis=-1, keepdims=True))
            h_ref[:, cols] = num / jnp.maximum(jnp.abs(den), jnp.exp(-m_t))
            b_last = bc[edge:edge + 1, cf:cf + 1]
            g_log = b_last - b_col + ig_col
            m_new = jnp.maximum(b_last + m_prev, jnp.max(g_log, axis=0, keepdims=True))
            wk = jnp.exp(g_log - m_new)
            decay = jnp.exp(b_last + m_prev - m_new)
            kw = k.astype(F32) * wk
            c_s[idx] = decay * c_prev + lax.dot_general(
                kw.astype(BF16), v, (((0,), (0,)), ((), ())), preferred_element_type=F32)
            n_s[idx] = decay * n_prev + jnp.sum(kw, axis=0, keepdims=True)
            m_s[idx] = jnp.broadcast_to(m_new, (1, LANES))


def _mlstm(qm, km, vm, gt, gate_bias, n_batch, chunks_per_seq, ctx_chunks, heads, dh):
    chunk = MLSTM_CHUNK
    ntok, dm = qm.shape
    ng = gt.shape[1]

    def fwd(b, s):
        return (b * chunks_per_seq + s, 0)

    def bwd(b, s):
        r = jnp.where(s < ctx_chunks, ctx_chunks - 1 - s, ctx_chunks + chunks_per_seq - 1 - s)
        return (b * chunks_per_seq + r, 0)

    tok_f = pl.BlockSpec((chunk, dm), fwd)
    tok_b = pl.BlockSpec((chunk, dm), bwd)
    kern = functools.partial(_mlstm_kernel, chunk=chunk, dh=dh, heads=heads)
    return pl.pallas_call(
        kern,
        grid=(n_batch, chunks_per_seq),
        in_specs=[tok_f, tok_f, tok_f, pl.BlockSpec((chunk, ng), fwd),
                  tok_b, tok_b, tok_b, pl.BlockSpec((chunk, ng), bwd),
                  pl.BlockSpec((1, ng), lambda b, s: (0, 0))],
        out_specs=[pl.BlockSpec((chunk, dm), fwd), pl.BlockSpec((chunk, dm), bwd)],
        out_shape=[jax.ShapeDtypeStruct((ntok, dm), F32)] * 2,
        scratch_shapes=[
            pltpu.VMEM((2 * heads, dh, dh), F32),
            pltpu.VMEM((2 * heads, 1, dh), F32),
            pltpu.VMEM((2 * heads, 1, LANES), F32),
        ],
        compiler_params=_cparams("parallel", "arbitrary"),
        name="mlstm",
    )(qm, km, vm, gt, qm, km, vm, gt, gate_bias)


def _attn_kernel(q_ref, k_ref, v_ref, o_ref, *, n_qh):
    k = k_ref[...]
    v = v_ref[...]
    for hq in range(n_qh):
        cols = slice(hq * LANES, (hq + 1) * LANES)
        s = _bdot_nt(q_ref[:, cols], k)
        p = jnp.exp2(s - jnp.max(s, axis=-1, keepdims=True))
        l = jnp.sum(p, axis=-1, keepdims=True)
        o = jnp.dot(p.astype(BF16), v, preferred_element_type=F32)
        o_ref[:, cols] = (o / l).astype(o_ref.dtype)


def _attention(qx, ka, va, tq, n_batch, seq_tiles, q_tile_of, kv_len):
    n_qh = qx.shape[1] // LANES
    return pl.pallas_call(
        functools.partial(_attn_kernel, n_qh=n_qh),
        grid=(n_batch, seq_tiles),
        in_specs=[
            pl.BlockSpec((tq, qx.shape[1]), lambda b, t: (q_tile_of(b, t), 0)),
            pl.BlockSpec((kv_len, ka.shape[1]), lambda b, t: (b, 0)),
            pl.BlockSpec((kv_len, va.shape[1]), lambda b, t: (b, 0)),
        ],
        out_specs=pl.BlockSpec((tq, qx.shape[1]), lambda b, t: (b * seq_tiles + t, 0)),
        out_shape=jax.ShapeDtypeStruct((n_batch * seq_tiles * tq, qx.shape[1]), BF16),
        compiler_params=_cparams("parallel", "parallel"),
        name="attention",
    )(qx, ka, va)


def _cd_tail_kernel(hf_ref, hb_ref, om_ref, att_ref, x_ref, mod_ref, ng_ref, wm_ref, wa_ref, lg_ref, lb_ref,
                    o_ref, *, dh, heads, dn_alpha):
    hsum = hf_ref[...] + hb_ref[...]
    pieces = []
    for hd in range(heads):
        cols = slice(hd * dh, (hd + 1) * dh)
        blk = hsum[:, cols]
        mu = jnp.mean(blk, axis=-1, keepdims=True)
        dlt = blk - mu
        var = jnp.mean(dlt * dlt, axis=-1, keepdims=True)
        pieces.append(dlt * lax.rsqrt(var + LN_EPS))
    hn = jnp.concatenate(pieces, axis=-1) * ng_ref[...] * jax.nn.sigmoid(om_ref[...])
    y = _bdot(hn, wm_ref[...]) + jnp.dot(att_ref[...], wa_ref[...], preferred_element_type=F32)
    v = dn_alpha * x_ref[...] + mod_ref[G1:G1 + 1, :] * y
    o_ref[...] = _layer_norm_rows(v, lg_ref[...], lb_ref[...])


def _cd_tail(hf, hb, om, att, x, mod_l, norm_g, w_m, w_a, ln_g, ln_b, tm, n_tiles, seq_tile_of,
             stream_of_tile, heads, dh, dn_alpha):
    d = x.shape[1]
    dm = hf.shape[1]
    const2 = lambda t: (0, 0)
    kern = functools.partial(_cd_tail_kernel, dh=dh, heads=heads, dn_alpha=dn_alpha)
    return pl.pallas_call(
        kern,
        grid=(n_tiles,),
        in_specs=[
            pl.BlockSpec((tm, dm), lambda t: (seq_tile_of(t), 0)),
            pl.BlockSpec((tm, dm), lambda t: (seq_tile_of(t), 0)),
            pl.BlockSpec((tm, dm), lambda t: (seq_tile_of(t), 0)),
            pl.BlockSpec((tm, att.shape[1]), lambda t: (t, 0)),
            pl.BlockSpec((tm, d), lambda t: (t, 0)),
            pl.BlockSpec((None, N_MOD, d), lambda t: (stream_of_tile(t), 0, 0)),
            pl.BlockSpec((1, dm), const2),
            pl.BlockSpec(w_m.shape, const2),
            pl.BlockSpec(w_a.shape, const2),
            pl.BlockSpec((1, d), const2),
            pl.BlockSpec((1, d), const2),
        ],
        out_specs=pl.BlockSpec((tm, d), lambda t: (t, 0)),
        out_shape=jax.ShapeDtypeStruct((n_tiles * tm, d), F32),
        compiler_params=_cparams("parallel"),
        name="cd_tail",
    )(hf, hb, om, att, x, mod_l, norm_g.reshape(1, dm), w_m.astype(BF16), w_a.astype(BF16),
      ln_g.reshape(1, d), ln_b.reshape(1, d))


def _layer0(xf, mod_l, p, geom):
    tm = geom["tm_mix"]
    n_tiles = xf.shape[0] // tm
    n_lat_tiles = geom["n_lat"] // tm
    lat_per_seq = geom["seq"] // tm
    ctx_per_seq = geom["ctx"] // tm
    n_batch = geom["batch"]

    def stream(t):
        return jnp.where(t < n_lat_tiles, t // lat_per_seq, n_batch)

    (z,) = _modulated_matmul(xf, mod_l, p["ab_w_in"], (p["ab_w_in"].shape[1],), (F32,), tm,
                             lambda t: t, stream, n_tiles)
    x1 = _ab_tail(z, xf, mod_l, p["pool_w"], p["pool_ls"], p["conv_w"], p["conv_b"], p["conv_ln_g"],
                  p["conv_ln_b"], p["ab_w_out"], p["ln_g"][0], p["ln_b"][0], tm, n_lat_tiles,
                  lat_per_seq, ctx_per_seq, stream, geom["dn_alpha"])
    tp = geom["tm_peer"]
    n_lat_p = geom["n_lat"] // tp
    lat_per_seq_p = geom["seq"] // tp

    def stream_p(t):
        return jnp.where(t < n_lat_p, t // lat_per_seq_p, n_batch)

    return _peer_layer(x1, mod_l, p["peer_w_q"], p["peer_keys"], p["peer_u"], p["peer_v"],
                       p["ln_g"][1], p["ln_b"][1], tp, stream_p, geom["dn_alpha"])


def _rope_tables(seq, ctx_len, att_dh):
    n_freq = att_dh // 4
    t = jnp.arange(seq)
    freqs = ROPE_THETA ** (-jnp.arange(n_freq, dtype=F32) / n_freq)
    ar = (t // GRID_W).astype(F32)[:, None] * freqs
    ac = (t % GRID_W).astype(F32)[:, None] * freqs
    cos = jnp.concatenate([jnp.cos(ar), jnp.cos(ar), jnp.cos(ac), jnp.cos(ac)], axis=-1)
    sin = jnp.concatenate([-jnp.sin(ar), jnp.sin(ar), -jnp.sin(ac), jnp.sin(ac)], axis=-1)
    cos = jnp.concatenate([jnp.ones((ctx_len, att_dh), F32), cos], axis=0)
    sin = jnp.concatenate([jnp.zeros((ctx_len, att_dh), F32), sin], axis=0)
    reps = LANES // att_dh
    return jnp.tile(cos, (1, reps)), jnp.tile(sin, (1, reps))


def _layer1(xf, mod_l, p, geom):
    tm = geom["tm_mix"]
    n_batch, seq, ctx_len = geom["batch"], geom["seq"], geom["ctx"]
    d = xf.shape[1]
    heads = MLSTM_HEADS
    d_m = p["mlstm_norm_g"].shape[0]
    dh = d_m // heads
    att_dh = p["q_norm_g"].shape[0]
    n_qh, n_kvh = ATT_QH, ATT_KVH
    group = n_qh // n_kvh
    d_q, d_kv = n_qh * att_dh, n_kvh * att_dh
    n_gate = 4 * heads
    ctx_tiles, lat_tiles = ctx_len // tm, seq // tm
    per_b = ctx_tiles + lat_tiles
    n_lat_tiles = n_batch * lat_tiles

    w = p["cd_w_in"]
    cuts = np.cumsum([0, d_m, d_m, d_m, d_m, n_gate, d_q, d_kv, d_kv])
    w_qm, w_km, w_vm, w_om, w_gt, w_qa, w_ka, w_va = (w[:, cuts[i]:cuts[i + 1]] for i in range(8))
    w_qx = jnp.zeros((d, n_qh, LANES), F32)
    q_gain = jnp.zeros((n_qh, LANES), F32)
    w_att = jnp.zeros((n_qh, LANES, d), F32)
    for hq in range(n_qh):
        lo = (hq // group) * att_dh
        w_qx = w_qx.at[:, hq, lo:lo + att_dh].set(w_qa[:, hq * att_dh:(hq + 1) * att_dh])
        q_gain = q_gain.at[hq, lo:lo + att_dh].set(p["q_norm_g"] * (att_dh ** -0.5 * math.log2(math.e)))
        w_att = w_att.at[hq, lo:lo + att_dh, :].set(p["cd_w_out"][d_m + hq * att_dh:d_m + (hq + 1) * att_dh])
    d_qx = n_qh * LANES
    w_all = jnp.concatenate([w_qm, w_km, w_vm, w_om, w_qx.reshape(d, d_qx), w_ka, w_va,
                             jnp.pad(w_gt, ((0, 0), (0, LANES - n_gate)))], axis=1)
    gate_bias = jnp.pad(p["mlstm_gate_b"].reshape(1, n_gate), ((0, 0), (0, LANES - n_gate)))
    cos_t, sin_t = _rope_tables(seq, ctx_len, att_dh)

    def in_tile(t):
        b, r = t // per_b, t % per_b
        return jnp.where(r < ctx_tiles, n_lat_tiles + b * ctx_tiles + r, b * lat_tiles + r - ctx_tiles)

    def stream_seq(t):
        return jnp.where(t % per_b < ctx_tiles, n_batch, t // per_b)

    qm, km, vm, om, qx, ka, va, gt = _cd_in(
        xf, mod_l, w_all, cos_t, sin_t, q_gain.reshape(1, d_qx), jnp.tile(p["k_norm_g"], n_kvh).reshape(1, d_kv),
        tm, n_batch * per_b, in_tile, stream_seq, lambda t: t % per_b, d_m, d_qx, d_kv, att_dh, dh ** -0.5)

    chunks_per_seq = (ctx_len + seq) // MLSTM_CHUNK
    hf, hb = _mlstm(qm, km, vm, gt, gate_bias, n_batch, chunks_per_seq, ctx_len // MLSTM_CHUNK, heads, dh)
    att = _attention(qx, ka, va, tm, n_batch, lat_tiles, lambda b, t: b * per_b + ctx_tiles + t,
                     ctx_len + seq)

    def seq_tile_of(t):
        return (t // lat_tiles) * per_b + ctx_tiles + t % lat_tiles

    x1 = _cd_tail(hf, hb, om, att, xf, mod_l, p["mlstm_norm_g"], p["cd_w_out"][:d_m], w_att.reshape(d_qx, d),
                  p["ln_g"][0], p["ln_b"][0], tm, n_lat_tiles, seq_tile_of, lambda t: t // lat_tiles,
                  heads, dh, geom["dn_alpha"])
    tp = geom["tm_peer"]
    lat_per_seq_p = seq // tp
    return _peer_layer(x1, mod_l, p["peer_w_q"], p["peer_keys"], p["peer_u"], p["peer_v"],
                       p["ln_g"][1], p["ln_b"][1], tp, lambda t: t // lat_per_seq_p, geom["dn_alpha"])


def kernel(x, c, ctx, c_ctx, mod_w, mod_b, ln_g, ln_b, ab_w_in, pool_w, pool_ls, conv_w, conv_b,
           conv_ln_g, conv_ln_b, ab_w_out, cd_w_in, mlstm_gate_b, mlstm_norm_g, q_norm_g, k_norm_g,
           cd_w_out, peer_w_q, peer_keys, peer_u, peer_v):
    n_batch, seq, d = x.shape
    ctx_len = ctx.shape[1]
    depth = mod_w.shape[0]
    assert depth == 2, "one pooling/convolution layer followed by one mLSTM/attention layer"
    n_streams = SUBLANES * (-(-(n_batch + 1) // SUBLANES))
    cc = jnp.concatenate([c, c_ctx[None], jnp.zeros((n_streams - n_batch - 1, d), F32)], axis=0)
    mod = _modulation(cc, mod_w, mod_b)
    geom = dict(tm_mix=256, tm_peer=512, n_lat=n_batch * seq, seq=seq, ctx=ctx_len, batch=n_batch,
                dn_alpha=(2 * depth) ** 0.25)
    xf = jnp.concatenate([x.reshape(-1, d), ctx.reshape(-1, d)], axis=0)
    p0 = dict(ab_w_in=ab_w_in[0], pool_w=pool_w[0], pool_ls=pool_ls[0], conv_w=conv_w[0],
              conv_b=conv_b[0], conv_ln_g=conv_ln_g[0], conv_ln_b=conv_ln_b[0], ab_w_out=ab_w_out[0],
              peer_w_q=peer_w_q[0], peer_keys=peer_keys[0], peer_u=peer_u[0], peer_v=peer_v[0],
              ln_g=ln_g[0], ln_b=ln_b[0])
    xf = _layer0(xf, mod[0], p0, geom)
    p1 = dict(cd_w_in=cd_w_in[0], mlstm_gate_b=mlstm_gate_b[0], mlstm_norm_g=mlstm_norm_g[0],
              q_norm_g=q_norm_g[0], k_norm_g=k_norm_g[0], cd_w_out=cd_w_out[0],
              peer_w_q=peer_w_q[1], peer_keys=peer_keys[1], peer_u=peer_u[1], peer_v=peer_v[1],
              ln_g=ln_g[1], ln_b=ln_b[1])
    out = _layer1(xf, mod[1], p1, geom)
    return out.reshape(n_batch, seq, d)
```

```python
import functools
import math

import jax
import jax.numpy as jnp
import numpy as np
from jax import lax
from jax.experimental import pallas as pl
from jax.experimental.pallas import tpu as pltpu

F32 = jnp.float32
BF16 = jnp.bfloat16

LANES = 128
SUBLANES = 8
VMEM_LIMIT_BYTES = 56 * 1024 * 1024

GRID_W = 64
N_MOD = 6
POOL_WINDOWS = (2, 4, 8, 16)
CONV_W = 31
HALO = 16
MLSTM_HEADS = 4
MLSTM_CHUNK = 128
ATT_QH = 8
ATT_KVH = 2
ROPE_THETA = 10000.0
PEER_HEADS = 8
PEER_NKEYS = 128
PEER_TOPK = 16
LN_EPS = 1e-5
RMS_EPS = 1e-6
NEG_INF = float("-inf")

SH1, SC1, G1, SH2, SC2, G2 = range(N_MOD)


def _cparams(*sem):
    return pltpu.CompilerParams(dimension_semantics=sem, vmem_limit_bytes=VMEM_LIMIT_BYTES)


def _bdot(a, b):
    return jnp.dot(a.astype(BF16), b.astype(BF16), preferred_element_type=F32)


def _bdot_nt(a, b):
    return lax.dot_general(a.astype(BF16), b.astype(BF16), (((1,), (1,)), ((), ())),
                           preferred_element_type=F32)


def _layer_norm_rows(v, g, b):
    mu = jnp.mean(v, axis=-1, keepdims=True)
    d = v - mu
    var = jnp.mean(d * d, axis=-1, keepdims=True)
    return d * lax.rsqrt(var + LN_EPS) * g + b


def _mod_kernel(c_ref, w_ref, b_ref, o_ref):
    c = c_ref[...]
    a = c * jax.nn.sigmoid(c)
    o_ref[...] = jnp.dot(a, w_ref[...], preferred_element_type=F32,
                         precision=lax.Precision.HIGHEST) + b_ref[...]


def _modulation(cc, mod_w, mod_b):
    depth, d, n = mod_w.shape
    rows = cc.shape[0]
    bn = d
    out = pl.pallas_call(
        _mod_kernel,
        grid=(depth, n // bn),
        in_specs=[
            pl.BlockSpec((rows, d), lambda l, j: (0, 0)),
            pl.BlockSpec((None, d, bn), lambda l, j: (l, 0, j)),
            pl.BlockSpec((None, 1, bn), lambda l, j: (l, 0, j)),
        ],
        out_specs=pl.BlockSpec((None, rows, bn), lambda l, j: (l, 0, j)),
        out_shape=jax.ShapeDtypeStruct((depth, rows, n), F32),
        compiler_params=_cparams("parallel", "parallel"),
        name="modulation",
    )(cc, mod_w, mod_b.reshape(depth, 1, n))
    return out.reshape(depth, rows, N_MOD, d)


def _modmm_kernel(x_ref, mod_ref, w_ref, *o_refs, splits):
    h = x_ref[...] * (1.0 + mod_ref[SC1:SC1 + 1, :]) + mod_ref[SH1:SH1 + 1, :]
    z = _bdot(h, w_ref[...])
    off = 0
    for o_ref, n in zip(o_refs, splits):
        o_ref[...] = z[:, off:off + n].astype(o_ref.dtype)
        off += n


def _modulated_matmul(x, mod_l, w, splits, out_dtypes, tm, in_tile, stream_of_tile, n_tiles):
    d = x.shape[1]
    n = w.shape[1]
    assert sum(splits) == n
    return pl.pallas_call(
        functools.partial(_modmm_kernel, splits=splits),
        grid=(n_tiles,),
        in_specs=[
            pl.BlockSpec((tm, d), lambda t: (in_tile(t), 0)),
            pl.BlockSpec((None, N_MOD, d), lambda t: (stream_of_tile(t), 0, 0)),
            pl.BlockSpec((d, n), lambda t: (0, 0)),
        ],
        out_specs=[pl.BlockSpec((tm, s), lambda t: (t, 0)) for s in splits],
        out_shape=[jax.ShapeDtypeStruct((n_tiles * tm, s), dt) for s, dt in zip(splits, out_dtypes)],
        compiler_params=_cparams("parallel"),
        name="modulated_matmul",
    )(x, mod_l, w.astype(BF16))


def _ab_tail_kernel(zp_ref, zm_ref, zn_ref, x_ref, mod_ref, pw_ref, pls_ref, cw_ref, cb_ref,
                    cg_ref, cbb_ref, wo_ref, lg_ref, lb_ref, o_ref, zpool, ubuf, ymix, ushift,
                    *, tm, d_pool, d_conv, n_lat_tiles, lat_tiles_per_seq, ctx_tiles_per_seq,
                    dn_alpha, row_chunk):
    t = pl.program_id(0)
    is_lat = t < n_lat_tiles
    per_seq = jnp.where(is_lat, lat_tiles_per_seq, ctx_tiles_per_seq)
    pos_tile = jnp.where(is_lat, t, t - n_lat_tiles) % per_seq
    first = pos_tile == 0
    last = pos_tile == per_seq - 1
    seq_len = per_seq * tm

    def glu(z):
        return z[:, d_pool:d_pool + d_conv] * jax.nn.sigmoid(z[:, d_pool + d_conv:])

    zp = jnp.where(first, 0.0, zp_ref[...])
    zn = jnp.where(last, 0.0, zn_ref[...])
    zm = zm_ref[...]
    zpool[0:HALO, :] = zp[:, :d_pool]
    zpool[HALO:HALO + tm, :] = zm[:, :d_pool]
    zpool[HALO + tm:, :] = zn[:, :d_pool]
    ubuf[0:HALO, :] = glu(zp)
    ubuf[HALO:HALO + tm, :] = glu(zm)
    ubuf[HALO + tm:2 * HALO + tm, :] = glu(zn)

    gw = d_pool // len(POOL_WINDOWS)
    tpos = pos_tile * tm + lax.broadcasted_iota(jnp.int32, (tm, gw), 0)
    for g, w in enumerate(POOL_WINDOWS):
        cols = slice(g * gw, (g + 1) * gw)
        acc = zpool[HALO - w // 2:HALO - w // 2 + tm, cols]
        for s in range(1, w):
            acc = acc + zpool[HALO - w // 2 + s:HALO - w // 2 + s + tm, cols]
        cnt = jnp.minimum(tpos + w // 2, seq_len) - jnp.maximum(tpos - w // 2, 0)
        diff = acc / cnt.astype(F32) - zpool[HALO:HALO + tm, cols]
        ymix[:, cols] = (_bdot(diff, pw_ref[g]) * pls_ref[:, cols]).astype(BF16)

    half = CONV_W // 2
    span = tm + 2 * HALO
    ubuf[span:, :] = jnp.zeros((SUBLANES, d_conv), F32)
    for s in range(SUBLANES):
        ushift[s] = ubuf[s:s + span, :]
    for r in range(0, tm, row_chunk):
        base = HALO - half + r
        acc = None
        for k in range(CONV_W):
            s = (base + k) % SUBLANES
            lo = base + k - s
            term = ushift[s, lo:lo + row_chunk, :] * cw_ref[k:k + 1, :]
            acc = term if acc is None else acc + term
        yn = _layer_norm_rows(acc + cb_ref[...], cg_ref[...], cbb_ref[...])
        ymix[r:r + row_chunk, d_pool:] = (yn * jax.nn.sigmoid(yn)).astype(BF16)

    y = jnp.dot(ymix[...], wo_ref[...], preferred_element_type=F32)
    v = dn_alpha * x_ref[...] + mod_ref[G1:G1 + 1, :] * y
    o_ref[...] = _layer_norm_rows(v, lg_ref[...], lb_ref[...])


def _ab_tail(z, x, mod_l, pool_w, pool_ls, conv_w, conv_b, cln_g, cln_b, w_out, ln_g, ln_b,
             tm, n_lat_tiles, lat_tiles_per_seq, ctx_tiles_per_seq, stream_of_tile, dn_alpha):
    ntok, d = x.shape
    d_pool = pool_ls.shape[0]
    d_conv = conv_b.shape[0]
    n_tiles = ntok // tm
    hb = tm // HALO
    n_hblocks = ntok // HALO
    row = lambda a: a.reshape(1, -1)
    kern = functools.partial(
        _ab_tail_kernel, tm=tm, d_pool=d_pool, d_conv=d_conv, n_lat_tiles=n_lat_tiles,
        lat_tiles_per_seq=lat_tiles_per_seq, ctx_tiles_per_seq=ctx_tiles_per_seq,
        dn_alpha=dn_alpha, row_chunk=32)
    const2 = lambda t: (0, 0)
    return pl.pallas_call(
        kern,
        grid=(n_tiles,),
        in_specs=[
            pl.BlockSpec((HALO, z.shape[1]), lambda t: (jnp.maximum(t * hb - 1, 0), 0)),
            pl.BlockSpec((tm, z.shape[1]), lambda t: (t, 0)),
            pl.BlockSpec((HALO, z.shape[1]), lambda t: (jnp.minimum((t + 1) * hb, n_hblocks - 1), 0)),
            pl.BlockSpec((tm, d), lambda t: (t, 0)),
            pl.BlockSpec((None, N_MOD, d), lambda t: (stream_of_tile(t), 0, 0)),
            pl.BlockSpec(pool_w.shape, lambda t: (0, 0, 0)),
            pl.BlockSpec((1, d_pool), const2),
            pl.BlockSpec(conv_w.shape, const2),
            pl.BlockSpec((1, d_conv), const2),
            pl.BlockSpec((1, d_conv), const2),
            pl.BlockSpec((1, d_conv), const2),
            pl.BlockSpec(w_out.shape, const2),
            pl.BlockSpec((1, d), const2),
            pl.BlockSpec((1, d), const2),
        ],
        out_specs=pl.BlockSpec((tm, d), lambda t: (t, 0)),
        out_shape=jax.ShapeDtypeStruct((ntok, d), F32),
        scratch_shapes=[
            pltpu.VMEM((tm + 2 * HALO, d_pool), F32),
            pltpu.VMEM((tm + 2 * HALO + SUBLANES, d_conv), F32),
            pltpu.VMEM((tm, d_pool + d_conv), BF16),
            pltpu.VMEM((SUBLANES, tm + 2 * HALO, d_conv), F32),
        ],
        compiler_params=_cparams("parallel"),
        name="ab_tail",
    )(z, z, z, x, mod_l, pool_w.astype(BF16), row(pool_ls), conv_w, row(conv_b), row(cln_g),
      row(cln_b), w_out.astype(BF16), row(ln_g), row(ln_b))


def _paired_lanes(lt, tm):
    shifted = (lt + 1) % (tm // LANES)
    return slice(shifted * LANES, (shifted + 1) * LANES)


def _top_values(s, k):
    outs = []
    cur = s
    for it in range(k):
        m = jnp.max(cur, axis=0, keepdims=True)
        outs.append(m)
        if it + 1 < k:
            cur = jnp.where(cur >= m, NEG_INF, cur)
    return outs


def _peer_route_kernel(x_ref, mod_ref, wq_ref, keys_ref, a_ref, nd_ref, c_ref, bn_ref, qt_ref,
                       *, tm, half):
    h = x_ref[...] * (1.0 + mod_ref[SC2:SC2 + 1, :]) + mod_ref[SH2:SH2 + 1, :]
    qt_ref[...] = _bdot_nt(wq_ref[...], h).astype(BF16)
    k = PEER_TOPK
    for hd in range(PEER_HEADS):
        for lt in range(tm // LANES):
            lanes = slice(lt * LANES, (lt + 1) * LANES)
            r0 = hd * 2 * half
            s1 = jnp.dot(keys_ref[0], qt_ref[r0:r0 + half, lanes], preferred_element_type=F32)
            s2 = jnp.dot(keys_ref[1], qt_ref[r0 + half:r0 + 2 * half, lanes],
                         preferred_element_type=F32)
            t1 = _top_values(s1, k)
            t2 = _top_values(s2, k)
            t2_all = jnp.concatenate(t2, axis=0)
            t1_tail = jnp.concatenate(t1[k // 2:], axis=0)
            pieces = []
            for a in range(k // 2):
                nb = min(k, (k + 1) // (a + 1))
                pieces.append(t1[a] + t2_all[:nb, :])
            pieces.append(t1_tail + t2[0])
            n_cand = sum(p.shape[0] for p in pieces)
            pad = (-n_cand) % SUBLANES
            if pad:
                pieces.append(jnp.full((pad, LANES), NEG_INF, F32))
            cand = jnp.concatenate(pieces, axis=0)
            top = _top_values(cand, k + 1)
            kth = top[k - 1]
            thr = 0.5 * (kth + jnp.maximum(top[k], kth - 1.0))
            m1, m2 = t1[0], t2[0]
            z = jnp.sum(jnp.where(cand >= thr, jnp.exp(cand - (m1 + m2)), 0.0), axis=0, keepdims=True)
            a_ref[hd, :, 0, lanes] = jnp.where(s1 >= t1[k - 1], jnp.exp(s1 - m1), 0.0)
            nd_ref[hd, :, 0, lanes] = -s1
            c_ref[hd, :, lanes] = s2 - thr
            bn_ref[hd, :, _paired_lanes(lt, tm)] = jnp.where(s2 >= t2[k - 1], jnp.exp(s2 - m2), 0.0) / (2.0 * z)


def _peer_route(x, mod_l, w_q, keys, tm, stream_of_tile):
    ntok, d = x.shape
    nq = w_q.shape[1]
    half = keys.shape[2]
    nk = keys.shape[1]
    n_tiles = ntok // tm
    sel_spec = pl.BlockSpec((PEER_HEADS, nk, tm), lambda t: (0, 0, t))
    row_spec = pl.BlockSpec((PEER_HEADS, nk, 1, tm), lambda t: (0, 0, 0, t))
    return pl.pallas_call(
        functools.partial(_peer_route_kernel, tm=tm, half=half),
        grid=(n_tiles,),
        in_specs=[
            pl.BlockSpec((tm, d), lambda t: (t, 0)),
            pl.BlockSpec((None, N_MOD, d), lambda t: (stream_of_tile(t), 0, 0)),
            pl.BlockSpec((nq, d), lambda t: (0, 0)),
            pl.BlockSpec(keys.shape, lambda t: (0, 0, 0)),
        ],
        out_specs=[row_spec, row_spec, sel_spec, sel_spec],
        out_shape=[jax.ShapeDtypeStruct((PEER_HEADS, nk, 1, ntok), F32)] * 2
        + [jax.ShapeDtypeStruct((PEER_HEADS, nk, ntok), F32)] * 2,
        scratch_shapes=[pltpu.VMEM((nq, tm), BF16)],
        compiler_params=_cparams("parallel"),
        name="peer_route",
    )(x, mod_l, w_q.T.astype(BF16), keys.astype(BF16))


def _gelu_twice(a):
    return a * (1.0 + lax.erf(a * (1.0 / math.sqrt(2.0))))


def _peer_dense_kernel(x_ref, mod_ref, a_ref, nd_ref, c_ref, bn_ref, u_next_ref, vt_prev_ref, u_first_ref,
                       vt_last_ref, lg_ref, lb_ref, o_ref, hbf, acc, wt, at,
                       *, tm, groups, fk, nk, dn_alpha, row_chunk, rc_block):
    e = pl.program_id(1)
    assert groups % 2 == 0

    @pl.when(e == 0)
    def _():
        h = x_ref[...] * (1.0 + mod_ref[SC2:SC2 + 1, :]) + mod_ref[SH2:SH2 + 1, :]
        hbf[...] = h.T.astype(BF16)
        acc[...] = jnp.zeros_like(acc)
        at[0] = jnp.dot(u_first_ref[0], hbf[...], preferred_element_type=F32)
        wt[1] = jnp.zeros(wt.shape[1:], BF16)

    def group_step(p, carry):
        slot = p % 2
        at[1 - slot] = jnp.dot(u_next_ref[p], hbf[...], preferred_element_type=F32)
        acc[...] += jnp.dot(vt_prev_ref[p], wt[1 - slot], preferred_element_type=F32)
        for lt in range(tm // LANES):
            lanes = slice(lt * LANES, (lt + 1) * LANES)
            for rb in range(0, nk // row_chunk, rc_block):
                gates = [[None] * rc_block for _ in range(fk)]
                for hd in range(PEER_HEADS):
                    a_rows = [jnp.broadcast_to(a_ref[hd, p * fk + q, :, lanes], (row_chunk, LANES))
                              for q in range(fk)]
                    nd_rows = [jnp.broadcast_to(nd_ref[hd, p * fk + q, :, lanes], (row_chunk, LANES))
                               for q in range(fk)]
                    for r in range(rc_block):
                        rows = slice((rb + r) * row_chunk, (rb + r + 1) * row_chunk)
                        cv = c_ref[hd, rows, lanes]
                        bv = bn_ref[hd, rows, _paired_lanes(lt, tm)]
                        for q in range(fk):
                            g = jnp.where(cv >= nd_rows[q], bv * a_rows[q], 0.0)
                            gates[q][r] = g if gates[q][r] is None else gates[q][r] + g
                for q in range(fk):
                    for r in range(rc_block):
                        erows = slice(q * nk + (rb + r) * row_chunk, q * nk + (rb + r + 1) * row_chunk)
                        wt[slot, erows, lanes] = (_gelu_twice(at[slot, erows, lanes]) * gates[q][r]).astype(BF16)
        return carry

    lax.fori_loop(0, groups, group_step, 0)

    @pl.when(e == pl.num_programs(1) - 1)
    def _():
        acc[...] += jnp.dot(vt_last_ref[0], wt[(groups - 1) % 2], preferred_element_type=F32)
        f = acc[...].T
        v = dn_alpha * x_ref[...] + mod_ref[G2:G2 + 1, :] * f
        o_ref[...] = _layer_norm_rows(v, lg_ref[...], lb_ref[...])


PEER_FIRST_KEYS_PER_PASS = 4
PEER_GROUPS_PER_STEP = 4
PEER_ROW_CHUNKS_PER_BLOCK = 2


def _peer_dense(x, mod_l, sel, u_tab, v_tab, ln_g, ln_b, tm, stream_of_tile, dn_alpha):
    ntok, d = x.shape
    n_exp = u_tab.shape[0]
    a_sel, nd_sel, c_sel, bn_sel = sel
    nk = c_sel.shape[1]
    n_tiles = ntok // tm
    fk, groups = PEER_FIRST_KEYS_PER_PASS, PEER_GROUPS_PER_STEP
    ge = fk * nk
    n_groups = n_exp // ge
    u3 = u_tab.astype(BF16).reshape(n_groups, ge, d)
    vt3 = v_tab.astype(BF16).reshape(n_groups, ge, d).transpose(0, 2, 1)
    u_next = jnp.roll(u3, -1, axis=0)
    vt_prev = jnp.roll(vt3, 1, axis=0)
    row_spec = pl.BlockSpec((PEER_HEADS, groups * fk, 1, tm), lambda t, e: (0, e, 0, t))
    full_spec = pl.BlockSpec((PEER_HEADS, nk, tm), lambda t, e: (0, 0, t))
    kern = functools.partial(_peer_dense_kernel, tm=tm, groups=groups, fk=fk, nk=nk, dn_alpha=dn_alpha,
                             row_chunk=2 * SUBLANES, rc_block=PEER_ROW_CHUNKS_PER_BLOCK)
    return pl.pallas_call(
        kern,
        grid=(n_tiles, n_groups // groups),
        in_specs=[
            pl.BlockSpec((tm, d), lambda t, e: (t, 0)),
            pl.BlockSpec((None, N_MOD, d), lambda t, e: (stream_of_tile(t), 0, 0)),
            row_spec, row_spec, full_spec, full_spec,
            pl.BlockSpec((groups, ge, d), lambda t, e: (e, 0, 0)),
            pl.BlockSpec((groups, d, ge), lambda t, e: (e, 0, 0)),
            pl.BlockSpec((1, ge, d), lambda t, e: (0, 0, 0)),
            pl.BlockSpec((1, d, ge), lambda t, e: (n_groups - 1, 0, 0)),
            pl.BlockSpec((1, d), lambda t, e: (0, 0)),
            pl.BlockSpec((1, d), lambda t, e: (0, 0)),
        ],
        out_specs=pl.BlockSpec((tm, d), lambda t, e: (t, 0)),
        out_shape=jax.ShapeDtypeStruct((ntok, d), F32),
        scratch_shapes=[
            pltpu.VMEM((d, tm), BF16),
            pltpu.VMEM((d, tm), F32),
            pltpu.VMEM((2, ge, tm), BF16),
            pltpu.VMEM((2, ge, tm), F32),
        ],
        compiler_params=_cparams("parallel", "arbitrary"),
        name="peer_dense",
    )(x, mod_l, a_sel, nd_sel, c_sel, bn_sel, u_next, vt_prev, u3, vt3, ln_g.reshape(1, d), ln_b.reshape(1, d))


def _peer_layer(x, mod_l, w_q, keys, u_tab, v_tab, ln_g, ln_b, tm, stream_of_tile, dn_alpha):
    sel = _peer_route(x, mod_l, w_q, keys, tm, stream_of_tile)
    return _peer_dense(x, mod_l, sel, u_tab, v_tab, ln_g, ln_b, tm, stream_of_tile, dn_alpha)


def _rope(v, cos, sin_signed):
    n = v.shape[-1]
    lane = lax.broadcasted_iota(jnp.int32, v.shape, v.ndim - 1)
    quarter = sin_signed.shape[-1] // 8
    partner = jnp.where(lane % (2 * quarter) < quarter,
                        pltpu.roll(v, n - quarter, v.ndim - 1), pltpu.roll(v, quarter, v.ndim - 1))
    reps = n // cos.shape[-1]
    if reps > 1:
        cos = jnp.concatenate([cos] * reps, axis=-1)
        sin_signed = jnp.concatenate([sin_signed] * reps, axis=-1)
    return v * cos + partner * sin_signed


def _cd_in_kernel(x_ref, mod_ref, w_ref, cos_ref, sin_ref, qg_ref, kg_ref,
                  qm_ref, km_ref, vm_ref, om_ref, qx_ref, ka_ref, va_ref, gt_ref,
                  *, d_m, d_qx, d_kv, att_dh, k_scale):
    h = x_ref[...] * (1.0 + mod_ref[SC1:SC1 + 1, :]) + mod_ref[SH1:SH1 + 1, :]
    z = _bdot(h, w_ref[...])
    qm_ref[...] = z[:, 0:d_m].astype(BF16)
    km_ref[...] = (z[:, d_m:2 * d_m] * k_scale).astype(BF16)
    vm_ref[...] = z[:, 2 * d_m:3 * d_m].astype(BF16)
    om_ref[...] = z[:, 3 * d_m:4 * d_m]
    off = 4 * d_m
    cos = cos_ref[...]
    sin = sin_ref[...]
    qx = z[:, off:off + d_qx]
    pieces = []
    for hq in range(d_qx // LANES):
        blk = qx[:, hq * LANES:(hq + 1) * LANES]
        ms = jnp.sum(blk * blk, axis=-1, keepdims=True) * (1.0 / att_dh)
        pieces.append(blk * lax.rsqrt(ms + RMS_EPS))
    qn = jnp.concatenate(pieces, axis=-1) * qg_ref[...]
    qx_ref[...] = _rope(qn, cos, sin).astype(BF16)
    off += d_qx
    kk = z[:, off:off + d_kv]
    lane = lax.broadcasted_iota(jnp.int32, kk.shape, 1)
    sq = kk * kk
    pieces = []
    for hk in range(d_kv // att_dh):
        sel = (lane >= hk * att_dh) & (lane < (hk + 1) * att_dh)
        ms = jnp.sum(jnp.where(sel, sq, 0.0), axis=-1, keepdims=True) * (1.0 / att_dh)
        pieces.append((sel, lax.rsqrt(ms + RMS_EPS)))
    scale = jnp.zeros_like(kk)
    for sel, r in pieces:
        scale = jnp.where(sel, r, scale)
    ka_ref[...] = _rope(kk * scale * kg_ref[...], cos, sin).astype(BF16)
    off += d_kv
    va_ref[...] = z[:, off:off + d_kv].astype(BF16)
    off += d_kv
    gt_ref[...] = z[:, off:]


def _cd_in(x, mod_l, w, cos_t, sin_t, q_gain, k_gain, tm, n_tiles, in_tile, stream_of_tile, pos_tile,
           d_m, d_qx, d_kv, att_dh, k_scale):
    d = x.shape[1]
    n = w.shape[1]
    ntok = n_tiles * tm
    widths = (d_m, d_m, d_m, d_m, d_qx, d_kv, d_kv, n - 4 * d_m - d_qx - 2 * d_kv)
    dtypes = (BF16, BF16, BF16, F32, BF16, BF16, BF16, F32)
    kern = functools.partial(_cd_in_kernel, d_m=d_m, d_qx=d_qx, d_kv=d_kv, att_dh=att_dh, k_scale=k_scale)
    return pl.pallas_call(
        kern,
        grid=(n_tiles,),
        in_specs=[
            pl.BlockSpec((tm, d), lambda t: (in_tile(t), 0)),
            pl.BlockSpec((None, N_MOD, d), lambda t: (stream_of_tile(t), 0, 0)),
            pl.BlockSpec((d, n), lambda t: (0, 0)),
            pl.BlockSpec((tm, cos_t.shape[1]), lambda t: (pos_tile(t), 0)),
            pl.BlockSpec((tm, sin_t.shape[1]), lambda t: (pos_tile(t), 0)),
            pl.BlockSpec((1, d_qx), lambda t: (0, 0)),
            pl.BlockSpec((1, d_kv), lambda t: (0, 0)),
        ],
        out_specs=[pl.BlockSpec((tm, wd), lambda t: (t, 0)) for wd in widths],
        out_shape=[jax.ShapeDtypeStruct((ntok, wd), dt) for wd, dt in zip(widths, dtypes)],
        compiler_params=_cparams("parallel"),
        name="cd_in",
    )(x, mod_l, w.astype(BF16), cos_t, sin_t, q_gain, k_gain)


def _mlstm_kernel(qf_ref, kf_ref, vf_ref, gf_ref, qb_ref, kb_ref, vb_ref, gb_ref, bias_ref,
                  hf_ref, hb_ref, c_s, n_s, m_s, *, chunk, dh, heads):
    s = pl.program_id(1)

    @pl.when(s == 0)
    def _():
        c_s[...] = jnp.zeros_like(c_s)
        n_s[...] = jnp.zeros_like(n_s)
        m_s[...] = jnp.zeros_like(m_s)

    ri = lax.broadcasted_iota(jnp.int32, (chunk, chunk), 0)
    cj = lax.broadcasted_iota(jnp.int32, (chunk, chunk), 1)
    streams = ((qf_ref, kf_ref, vf_ref, gf_ref, hf_ref), (qb_ref, kb_ref, vb_ref, gb_ref, hb_ref))
    for direction, (q_ref, k_ref, v_ref, g_ref, h_ref) in enumerate(streams):
        mask = (cj <= ri) if direction == 0 else (cj >= ri)
        edge = chunk - 1 if direction == 0 else 0
        gates = g_ref[...] + bias_ref[...]
        logf = jax.nn.log_sigmoid(gates)
        bc = jnp.dot(mask.astype(F32), logf, preferred_element_type=F32, precision=lax.Precision.HIGHEST)
        br = bc.T
        gr = gates.T
        for hd in range(heads):
            ci = (2 * direction) * heads + hd
            cf = (2 * direction + 1) * heads + hd
            idx = direction * heads + hd
            b_col = bc[:, cf:cf + 1]
            b_row = br[cf:cf + 1, :]
            ig_row = gr[ci:ci + 1, :]
            ig_col = gates[:, ci:ci + 1]
            m_prev = m_s[idx][:, 0:1]
            dmat = jnp.where(mask, b_col - b_row + ig_row, NEG_INF)
            inter = b_col + m_prev
            m_t = jnp.maximum(inter, jnp.max(dmat, axis=-1, keepdims=True))
            dexp = jnp.exp(dmat - m_t)
            w_inter = jnp.exp(inter - m_t)
            cols = slice(hd * dh, (hd + 1) * dh)
            q = q_ref[:, cols]
            k = k_ref[:, cols]
            v = v_ref[:, cols]
            c_prev = c_s[idx]
            n_prev = n_s[idx]
            sm = _bdot_nt(q, k) * dexp
            num = _bdot(sm, v) + w_inter * _bdot(q, c_prev)
            den = (jnp.sum(sm, axis=-1, keepdims=True)
                   + w_inter * jnp.sum(q.astype(F32) * n_prev, axis=-1, keepdims=True))
            h_ref[:, cols] = num / jnp.maximum(jnp.abs(den), jnp.exp(-m_t))
            b_last = bc[edge:edge + 1, cf:cf + 1]
            g_log = b_last - b_col + ig_col
            m_new = jnp.maximum(b_last + m_prev, jnp.max(g_log, axis=0, keepdims=True))
            wk = jnp.exp(g_log - m_new)
            decay = jnp.exp(b_last + m_prev - m_new)
            kw = k.astype(F32) * wk
            c_s[idx] = decay * c_prev + lax.dot_general(
                kw.astype(BF16), v, (((0,), (0,)), ((), ())), preferred_element_type=F32)
            n_s[idx] = decay * n_prev + jnp.sum(kw, axis=0, keepdims=True)
            m_s[idx] = jnp.broadcast_to(m_new, (1, LANES))


def _mlstm(qm, km, vm, gt, gate_bias, n_batch, chunks_per_seq, ctx_chunks, heads, dh):
    chunk = MLSTM_CHUNK
    ntok, dm = qm.shape
    ng = gt.shape[1]

    def fwd(b, s):
        return (b * chunks_per_seq + s, 0)

    def bwd(b, s):
        r = jnp.where(s < ctx_chunks, ctx_chunks - 1 - s, ctx_chunks + chunks_per_seq - 1 - s)
        return (b * chunks_per_seq + r, 0)

    tok_f = pl.BlockSpec((chunk, dm), fwd)
    tok_b = pl.BlockSpec((chunk, dm), bwd)
    kern = functools.partial(_mlstm_kernel, chunk=chunk, dh=dh, heads=heads)
    return pl.pallas_call(
        kern,
        grid=(n_batch, chunks_per_seq),
        in_specs=[tok_f, tok_f, tok_f, pl.BlockSpec((chunk, ng), fwd),
                  tok_b, tok_b, tok_b, pl.BlockSpec((chunk, ng), bwd),
                  pl.BlockSpec((1, ng), lambda b, s: (0, 0))],
        out_specs=[pl.BlockSpec((chunk, dm), fwd), pl.BlockSpec((chunk, dm), bwd)],
        out_shape=[jax.ShapeDtypeStruct((ntok, dm), F32)] * 2,
        scratch_shapes=[
            pltpu.VMEM((2 * heads, dh, dh), F32),
            pltpu.VMEM((2 * heads, 1, dh), F32),
            pltpu.VMEM((2 * heads, 1, LANES), F32),
        ],
        compiler_params=_cparams("parallel", "arbitrary"),
        name="mlstm",
    )(qm, km, vm, gt, qm, km, vm, gt, gate_bias)


def _attn_kernel(q_ref, k_ref, v_ref, o_ref, *, n_qh):
    k = k_ref[...]
    v = v_ref[...]
    for hq in range(n_qh):
        cols = slice(hq * LANES, (hq + 1) * LANES)
        s = _bdot_nt(q_ref[:, cols], k)
        p = jnp.exp2(s - jnp.max(s, axis=-1, keepdims=True))
        l = jnp.sum(p, axis=-1, keepdims=True)
        o = jnp.dot(p.astype(BF16), v, preferred_element_type=F32)
        o_ref[:, cols] = (o / l).astype(o_ref.dtype)


def _attention(qx, ka, va, tq, n_batch, seq_tiles, q_tile_of, kv_len):
    n_qh = qx.shape[1] // LANES
    return pl.pallas_call(
        functools.partial(_attn_kernel, n_qh=n_qh),
        grid=(n_batch, seq_tiles),
        in_specs=[
            pl.BlockSpec((tq, qx.shape[1]), lambda b, t: (q_tile_of(b, t), 0)),
            pl.BlockSpec((kv_len, ka.shape[1]), lambda b, t: (b, 0)),
            pl.BlockSpec((kv_len, va.shape[1]), lambda b, t: (b, 0)),
        ],
        out_specs=pl.BlockSpec((tq, qx.shape[1]), lambda b, t: (b * seq_tiles + t, 0)),
        out_shape=jax.ShapeDtypeStruct((n_batch * seq_tiles * tq, qx.shape[1]), BF16),
        compiler_params=_cparams("parallel", "parallel"),
        name="attention",
    )(qx, ka, va)


def _cd_tail_kernel(hf_ref, hb_ref, om_ref, att_ref, x_ref, mod_ref, ng_ref, wm_ref, wa_ref, lg_ref, lb_ref,
                    o_ref, *, dh, heads, dn_alpha):
    hsum = hf_ref[...] + hb_ref[...]
    pieces = []
    for hd in range(heads):
        cols = slice(hd * dh, (hd + 1) * dh)
        blk = hsum[:, cols]
        mu = jnp.mean(blk, axis=-1, keepdims=True)
        dlt = blk - mu
        var = jnp.mean(dlt * dlt, axis=-1, keepdims=True)
        pieces.append(dlt * lax.rsqrt(var + LN_EPS))
    hn = jnp.concatenate(pieces, axis=-1) * ng_ref[...] * jax.nn.sigmoid(om_ref[...])
    y = _bdot(hn, wm_ref[...]) + jnp.dot(att_ref[...], wa_ref[...], preferred_element_type=F32)
    v = dn_alpha * x_ref[...] + mod_ref[G1:G1 + 1, :] * y
    o_ref[...] = _layer_norm_rows(v, lg_ref[...], lb_ref[...])


def _cd_tail(hf, hb, om, att, x, mod_l, norm_g, w_m, w_a, ln_g, ln_b, tm, n_tiles, seq_tile_of,
             stream_of_tile, heads, dh, dn_alpha):
    d = x.shape[1]
    dm = hf.shape[1]
    const2 = lambda t: (0, 0)
    kern = functools.partial(_cd_tail_kernel, dh=dh, heads=heads, dn_alpha=dn_alpha)
    return pl.pallas_call(
        kern,
        grid=(n_tiles,),
        in_specs=[
            pl.BlockSpec((tm, dm), lambda t: (seq_tile_of(t), 0)),
            pl.BlockSpec((tm, dm), lambda t: (seq_tile_of(t), 0)),
            pl.BlockSpec((tm, dm), lambda t: (seq_tile_of(t), 0)),
            pl.BlockSpec((tm, att.shape[1]), lambda t: (t, 0)),
            pl.BlockSpec((tm, d), lambda t: (t, 0)),
            pl.BlockSpec((None, N_MOD, d), lambda t: (stream_of_tile(t), 0, 0)),
            pl.BlockSpec((1, dm), const2),
            pl.BlockSpec(w_m.shape, const2),
            pl.BlockSpec(w_a.shape, const2),
            pl.BlockSpec((1, d), const2),
            pl.BlockSpec((1, d), const2),
        ],
        out_specs=pl.BlockSpec((tm, d), lambda t: (t, 0)),
        out_shape=jax.ShapeDtypeStruct((n_tiles * tm, d), F32),
        compiler_params=_cparams("parallel"),
        name="cd_tail",
    )(hf, hb, om, att, x, mod_l, norm_g.reshape(1, dm), w_m.astype(BF16), w_a.astype(BF16),
      ln_g.reshape(1, d), ln_b.reshape(1, d))


def _layer0(xf, mod_l, p, geom):
    tm = geom["tm_mix"]
    n_tiles = xf.shape[0] // tm
    n_lat_tiles = geom["n_lat"] // tm
    lat_per_seq = geom["seq"] // tm
    ctx_per_seq = geom["ctx"] // tm
    n_batch = geom["batch"]

    def stream(t):
        return jnp.where(t < n_lat_tiles, t // lat_per_seq, n_batch)

    (z,) = _modulated_matmul(xf, mod_l, p["ab_w_in"], (p["ab_w_in"].shape[1],), (F32,), tm,
                             lambda t: t, stream, n_tiles)
    x1 = _ab_tail(z, xf, mod_l, p["pool_w"], p["pool_ls"], p["conv_w"], p["conv_b"], p["conv_ln_g"],
                  p["conv_ln_b"], p["ab_w_out"], p["ln_g"][0], p["ln_b"][0], tm, n_lat_tiles,
                  lat_per_seq, ctx_per_seq, stream, geom["dn_alpha"])
    tp = geom["tm_peer"]
    n_lat_p = geom["n_lat"] // tp
    lat_per_seq_p = geom["seq"] // tp

    def stream_p(t):
        return jnp.where(t < n_lat_p, t // lat_per_seq_p, n_batch)

    return _peer_layer(x1, mod_l, p["peer_w_q"], p["peer_keys"], p["peer_u"], p["peer_v"],
                       p["ln_g"][1], p["ln_b"][1], tp, stream_p, geom["dn_alpha"])


def _rope_tables(seq, ctx_len, att_dh):
    n_freq = att_dh // 4
    t = jnp.arange(seq)
    freqs = ROPE_THETA ** (-jnp.arange(n_freq, dtype=F32) / n_freq)
    ar = (t // GRID_W).astype(F32)[:, None] * freqs
    ac = (t % GRID_W).astype(F32)[:, None] * freqs
    cos = jnp.concatenate([jnp.cos(ar), jnp.cos(ar), jnp.cos(ac), jnp.cos(ac)], axis=-1)
    sin = jnp.concatenate([-jnp.sin(ar), jnp.sin(ar), -jnp.sin(ac), jnp.sin(ac)], axis=-1)
    cos = jnp.concatenate([jnp.ones((ctx_len, att_dh), F32), cos], axis=0)
    sin = jnp.concatenate([jnp.zeros((ctx_len, att_dh), F32), sin], axis=0)
    reps = LANES // att_dh
    return jnp.tile(cos, (1, reps)), jnp.tile(sin, (1, reps))


def _layer1(xf, mod_l, p, geom):
    tm = geom["tm_mix"]
    n_batch, seq, ctx_len = geom["batch"], geom["seq"], geom["ctx"]
    d = xf.shape[1]
    heads = MLSTM_HEADS
    d_m = p["mlstm_norm_g"].shape[0]
    dh = d_m // heads
    att_dh = p["q_norm_g"].shape[0]
    n_qh, n_kvh = ATT_QH, ATT_KVH
    group = n_qh // n_kvh
    d_q, d_kv = n_qh * att_dh, n_kvh * att_dh
    n_gate = 4 * heads
    ctx_tiles, lat_tiles = ctx_len // tm, seq // tm
    per_b = ctx_tiles + lat_tiles
    n_lat_tiles = n_batch * lat_tiles

    w = p["cd_w_in"]
    cuts = np.cumsum([0, d_m, d_m, d_m, d_m, n_gate, d_q, d_kv, d_kv])
    w_qm, w_km, w_vm, w_om, w_gt, w_qa, w_ka, w_va = (w[:, cuts[i]:cuts[i + 1]] for i in range(8))
    w_qx = jnp.zeros((d, n_qh, LANES), F32)
    q_gain = jnp.zeros((n_qh, LANES), F32)
    w_att = jnp.zeros((n_qh, LANES, d), F32)
    for hq in range(n_qh):
        lo = (hq // group) * att_dh
        w_qx = w_qx.at[:, hq, lo:lo + att_dh].set(w_qa[:, hq * att_dh:(hq + 1) * att_dh])
        q_gain = q_gain.at[hq, lo:lo + att_dh].set(p["q_norm_g"] * (att_dh ** -0.5 * math.log2(math.e)))
        w_att = w_att.at[hq, lo:lo + att_dh, :].set(p["cd_w_out"][d_m + hq * att_dh:d_m + (hq + 1) * att_dh])
    d_qx = n_qh * LANES
    w_all = jnp.concatenate([w_qm, w_km, w_vm, w_om, w_qx.reshape(d, d_qx), w_ka, w_va,
                             jnp.pad(w_gt, ((0, 0), (0, LANES - n_gate)))], axis=1)
    gate_bias = jnp.pad(p["mlstm_gate_b"].reshape(1, n_gate), ((0, 0), (0, LANES - n_gate)))
    cos_t, sin_t = _rope_tables(seq, ctx_len, att_dh)

    def in_tile(t):
        b, r = t // per_b, t % per_b
        return jnp.where(r < ctx_tiles, n_lat_tiles + b * ctx_tiles + r, b * lat_tiles + r - ctx_tiles)

    def stream_seq(t):
        return jnp.where(t % per_b < ctx_tiles, n_batch, t // per_b)

    qm, km, vm, om, qx, ka, va, gt = _cd_in(
        xf, mod_l, w_all, cos_t, sin_t, q_gain.reshape(1, d_qx), jnp.tile(p["k_norm_g"], n_kvh).reshape(1, d_kv),
        tm, n_batch * per_b, in_tile, stream_seq, lambda t: t % per_b, d_m, d_qx, d_kv, att_dh, dh ** -0.5)

    chunks_per_seq = (ctx_len + seq) // MLSTM_CHUNK
    hf, hb = _mlstm(qm, km, vm, gt, gate_bias, n_batch, chunks_per_seq, ctx_len // MLSTM_CHUNK, heads, dh)
    att = _attention(qx, ka, va, tm, n_batch, lat_tiles, lambda b, t: b * per_b + ctx_tiles + t,
                     ctx_len + seq)

    def seq_tile_of(t):
        return (t // lat_tiles) * per_b + ctx_tiles + t % lat_tiles

    x1 = _cd_tail(hf, hb, om, att, xf, mod_l, p["mlstm_norm_g"], p["cd_w_out"][:d_m], w_att.reshape(d_qx, d),
                  p["ln_g"][0], p["ln_b"][0], tm, n_lat_tiles, seq_tile_of, lambda t: t // lat_tiles,
                  heads, dh, geom["dn_alpha"])
    tp = geom["tm_peer"]
    lat_per_seq_p = seq // tp
    return _peer_layer(x1, mod_l, p["peer_w_q"], p["peer_keys"], p["peer_u"], p["peer_v"],
                       p["ln_g"][1], p["ln_b"][1], tp, lambda t: t // lat_per_seq_p, geom["dn_alpha"])


def kernel(x, c, ctx, c_ctx, mod_w, mod_b, ln_g, ln_b, ab_w_in, pool_w, pool_ls, conv_w, conv_b,
           conv_ln_g, conv_ln_b, ab_w_out, cd_w_in, mlstm_gate_b, mlstm_norm_g, q_norm_g, k_norm_g,
           cd_w_out, peer_w_q, peer_keys, peer_u, peer_v):
    n_batch, seq, d = x.shape
    ctx_len = ctx.shape[1]
    depth = mod_w.shape[0]
    assert depth == 2, "one pooling/convolution layer followed by one mLSTM/attention layer"
    n_streams = SUBLANES * (-(-(n_batch + 1) // SUBLANES))
    cc = jnp.concatenate([c, c_ctx[None], jnp.zeros((n_streams - n_batch - 1, d), F32)], axis=0)
    mod = _modulation(cc, mod_w, mod_b)
    geom = dict(tm_mix=256, tm_peer=512, n_lat=n_batch * seq, seq=seq, ctx=ctx_len, batch=n_batch,
                dn_alpha=(2 * depth) ** 0.25)
    xf = jnp.concatenate([x.reshape(-1, d), ctx.reshape(-1, d)], axis=0)
    p0 = dict(ab_w_in=ab_w_in[0], pool_w=pool_w[0], pool_ls=pool_ls[0], conv_w=conv_w[0],
              conv_b=conv_b[0], conv_ln_g=conv_ln_g[0], conv_ln_b=conv_ln_b[0], ab_w_out=ab_w_out[0],
              peer_w_q=peer_w_q[0], peer_keys=peer_keys[0], peer_u=peer_u[0], peer_v=peer_v[0],
              ln_g=ln_g[0], ln_b=ln_b[0])
    xf = _layer0(xf, mod[0], p0, geom)
    p1 = dict(cd_w_in=cd_w_in[0], mlstm_gate_b=mlstm_gate_b[0], mlstm_norm_g=mlstm_norm_g[0],
              q_norm_g=q_norm_g[0], k_norm_g=k_norm_g[0], cd_w_out=cd_w_out[0],
              peer_w_q=peer_w_q[1], peer_keys=peer_keys[1], peer_u=peer_u[1], peer_v=peer_v[1],
              ln_g=ln_g[1], ln_b=ln_b[1])
    out = _layer1(xf, mod[1], p1, geom)
    return out.reshape(n_batch, seq, d)
```

```python
import functools
import math

import jax
import jax.numpy as jnp
import numpy as np
from jax import lax
from jax.experimental import pallas as pl
from jax.experimental.pallas import tpu as pltpu

F32 = jnp.float32
BF16 = jnp.bfloat16

LANES = 128
SUBLANES = 8
VMEM_LIMIT_BYTES = 56 * 1024 * 1024

GRID_W = 64
N_MOD = 6
POOL_WINDOWS = (2, 4, 8, 16)
CONV_W = 31
HALO = 16
MLSTM_HEADS = 4
MLSTM_CHUNK = 128
ATT_QH = 8
ATT_KVH = 2
ROPE_THETA = 10000.0
PEER_HEADS = 8
PEER_NKEYS = 128
PEER_TOPK = 16
LN_EPS = 1e-5
RMS_EPS = 1e-6
NEG_INF = float("-inf")

SH1, SC1, G1, SH2, SC2, G2 = range(N_MOD)


def _cparams(*sem):
    return pltpu.CompilerParams(dimension_semantics=sem, vmem_limit_bytes=VMEM_LIMIT_BYTES)


def _bdot(a, b):
    return jnp.dot(a.astype(BF16), b.astype(BF16), preferred_element_type=F32)


def _bdot_nt(a, b):
    return lax.dot_general(a.astype(BF16), b.astype(BF16), (((1,), (1,)), ((), ())),
                           preferred_element_type=F32)


def _layer_norm_rows(v, g, b):
    mu = jnp.mean(v, axis=-1, keepdims=True)
    d = v - mu
    var = jnp.mean(d * d, axis=-1, keepdims=True)
    return d * lax.rsqrt(var + LN_EPS) * g + b


def _mod_kernel(c_ref, w_ref, b_ref, o_ref):
    c = c_ref[...]
    a = c * jax.nn.sigmoid(c)
    o_ref[...] = jnp.dot(a, w_ref[...], preferred_element_type=F32,
                         precision=lax.Precision.HIGHEST) + b_ref[...]


def _modulation(cc, mod_w, mod_b):
    depth, d, n = mod_w.shape
    rows = cc.shape[0]
    bn = d
    out = pl.pallas_call(
        _mod_kernel,
        grid=(depth, n // bn),
        in_specs=[
            pl.BlockSpec((rows, d), lambda l, j: (0, 0)),
            pl.BlockSpec((None, d, bn), lambda l, j: (l, 0, j)),
            pl.BlockSpec((None, 1, bn), lambda l, j: (l, 0, j)),
        ],
        out_specs=pl.BlockSpec((None, rows, bn), lambda l, j: (l, 0, j)),
        out_shape=jax.ShapeDtypeStruct((depth, rows, n), F32),
        compiler_params=_cparams("parallel", "parallel"),
        name="modulation",
    )(cc, mod_w, mod_b.reshape(depth, 1, n))
    return out.reshape(depth, rows, N_MOD, d)


def _modmm_kernel(x_ref, mod_ref, w_ref, *o_refs, splits):
    h = x_ref[...] * (1.0 + mod_ref[SC1:SC1 + 1, :]) + mod_ref[SH1:SH1 + 1, :]
    z = _bdot(h, w_ref[...])
    off = 0
    for o_ref, n in zip(o_refs, splits):
        o_ref[...] = z[:, off:off + n].astype(o_ref.dtype)
        off += n


def _modulated_matmul(x, mod_l, w, splits, out_dtypes, tm, in_tile, stream_of_tile, n_tiles):
    d = x.shape[1]
    n = w.shape[1]
    assert sum(splits) == n
    return pl.pallas_call(
        functools.partial(_modmm_kernel, splits=splits),
        grid=(n_tiles,),
        in_specs=[
            pl.BlockSpec((tm, d), lambda t: (in_tile(t), 0)),
            pl.BlockSpec((None, N_MOD, d), lambda t: (stream_of_tile(t), 0, 0)),
            pl.BlockSpec((d, n), lambda t: (0, 0)),
        ],
        out_specs=[pl.BlockSpec((tm, s), lambda t: (t, 0)) for s in splits],
        out_shape=[jax.ShapeDtypeStruct((n_tiles * tm, s), dt) for s, dt in zip(splits, out_dtypes)],
        compiler_params=_cparams("parallel"),
        name="modulated_matmul",
    )(x, mod_l, w.astype(BF16))


def _ab_tail_kernel(zp_ref, zm_ref, zn_ref, x_ref, mod_ref, pw_ref, pls_ref, cw_ref, cb_ref,
                    cg_ref, cbb_ref, wo_ref, lg_ref, lb_ref, o_ref, zpool, ubuf, ymix, ushift,
                    *, tm, d_pool, d_conv, n_lat_tiles, lat_tiles_per_seq, ctx_tiles_per_seq,
                    dn_alpha, row_chunk):
    t = pl.program_id(0)
    is_lat = t < n_lat_tiles
    per_seq = jnp.where(is_lat, lat_tiles_per_seq, ctx_tiles_per_seq)
    pos_tile = jnp.where(is_lat, t, t - n_lat_tiles) % per_seq
    first = pos_tile == 0
    last = pos_tile == per_seq - 1
    seq_len = per_seq * tm

    def glu(z):
        return z[:, d_pool:d_pool + d_conv] * jax.nn.sigmoid(z[:, d_pool + d_conv:])

    zp = jnp.where(first, 0.0, zp_ref[...])
    zn = jnp.where(last, 0.0, zn_ref[...])
    zm = zm_ref[...]
    zpool[0:HALO, :] = zp[:, :d_pool]
    zpool[HALO:HALO + tm, :] = zm[:, :d_pool]
    zpool[HALO + tm:, :] = zn[:, :d_pool]
    ubuf[0:HALO, :] = glu(zp)
    ubuf[HALO:HALO + tm, :] = glu(zm)
    ubuf[HALO + tm:2 * HALO + tm, :] = glu(zn)

    gw = d_pool // len(POOL_WINDOWS)
    tpos = pos_tile * tm + lax.broadcasted_iota(jnp.int32, (tm, gw), 0)
    for g, w in enumerate(POOL_WINDOWS):
        cols = slice(g * gw, (g + 1) * gw)
        acc = zpool[HALO - w // 2:HALO - w // 2 + tm, cols]
        for s in range(1, w):
            acc = acc + zpool[HALO - w // 2 + s:HALO - w // 2 + s + tm, cols]
        cnt = jnp.minimum(tpos + w // 2, seq_len) - jnp.maximum(tpos - w // 2, 0)
        diff = acc / cnt.astype(F32) - zpool[HALO:HALO + tm, cols]
        ymix[:, cols] = (_bdot(diff, pw_ref[g]) * pls_ref[:, cols]).astype(BF16)

    half = CONV_W // 2
    span = tm + 2 * HALO
    ubuf[span:, :] = jnp.zeros((SUBLANES, d_conv), F32)
    for s in range(SUBLANES):
        ushift[s] = ubuf[s:s + span, :]
    for r in range(0, tm, row_chunk):
        base = HALO - half + r
        acc = None
        for k in range(CONV_W):
            s = (base + k) % SUBLANES
            lo = base + k - s
            term = ushift[s, lo:lo + row_chunk, :] * cw_ref[k:k + 1, :]
            acc = term if acc is None else acc + term
        yn = _layer_norm_rows(acc + cb_ref[...], cg_ref[...], cbb_ref[...])
        ymix[r:r + row_chunk, d_pool:] = (yn * jax.nn.sigmoid(yn)).astype(BF16)

    y = jnp.dot(ymix[...], wo_ref[...], preferred_element_type=F32)
    v = dn_alpha * x_ref[...] + mod_ref[G1:G1 + 1, :] * y
    o_ref[...] = _layer_norm_rows(v, lg_ref[...], lb_ref[...])


def _ab_tail(z, x, mod_l, pool_w, pool_ls, conv_w, conv_b, cln_g, cln_b, w_out, ln_g, ln_b,
             tm, n_lat_tiles, lat_tiles_per_seq, ctx_tiles_per_seq, stream_of_tile, dn_alpha):
    ntok, d = x.shape
    d_pool = pool_ls.shape[0]
    d_conv = conv_b.shape[0]
    n_tiles = ntok // tm
    hb = tm // HALO
    n_hblocks = ntok // HALO
    row = lambda a: a.reshape(1, -1)
    kern = functools.partial(
        _ab_tail_kernel, tm=tm, d_pool=d_pool, d_conv=d_conv, n_lat_tiles=n_lat_tiles,
        lat_tiles_per_seq=lat_tiles_per_seq, ctx_tiles_per_seq=ctx_tiles_per_seq,
        dn_alpha=dn_alpha, row_chunk=32)
    const2 = lambda t: (0, 0)
    return pl.pallas_call(
        kern,
        grid=(n_tiles,),
        in_specs=[
            pl.BlockSpec((HALO, z.shape[1]), lambda t: (jnp.maximum(t * hb - 1, 0), 0)),
            pl.BlockSpec((tm, z.shape[1]), lambda t: (t, 0)),
            pl.BlockSpec((HALO, z.shape[1]), lambda t: (jnp.minimum((t + 1) * hb, n_hblocks - 1), 0)),
            pl.BlockSpec((tm, d), lambda t: (t, 0)),
            pl.BlockSpec((None, N_MOD, d), lambda t: (stream_of_tile(t), 0, 0)),
            pl.BlockSpec(pool_w.shape, lambda t: (0, 0, 0)),
            pl.BlockSpec((1, d_pool), const2),
            pl.BlockSpec(conv_w.shape, const2),
            pl.BlockSpec((1, d_conv), const2),
            pl.BlockSpec((1, d_conv), const2),
            pl.BlockSpec((1, d_conv), const2),
            pl.BlockSpec(w_out.shape, const2),
            pl.BlockSpec((1, d), const2),
            pl.BlockSpec((1, d), const2),
        ],
        out_specs=pl.BlockSpec((tm, d), lambda t: (t, 0)),
        out_shape=jax.ShapeDtypeStruct((ntok, d), F32),
        scratch_shapes=[
            pltpu.VMEM((tm + 2 * HALO, d_pool), F32),
            pltpu.VMEM((tm + 2 * HALO + SUBLANES, d_conv), F32),
            pltpu.VMEM((tm, d_pool + d_conv), BF16),
            pltpu.VMEM((SUBLANES, tm + 2 * HALO, d_conv), F32),
        ],
        compiler_params=_cparams("parallel"),
        name="ab_tail",
    )(z, z, z, x, mod_l, pool_w.astype(BF16), row(pool_ls), conv_w, row(conv_b), row(cln_g),
      row(cln_b), w_out.astype(BF16), row(ln_g), row(ln_b))


def _paired_lanes(lt, tm):
    shifted = (lt + 1) % (tm // LANES)
    return slice(shifted * LANES, (shifted + 1) * LANES)


def _top_values(s, k):
    outs = []
    cur = s
    for it in range(k):
        m = jnp.max(cur, axis=0, keepdims=True)
        outs.append(m)
        if it + 1 < k:
            cur = jnp.where(cur >= m, NEG_INF, cur)
    return outs


def _peer_route_kernel(x_ref, mod_ref, wq_ref, keys_ref, a_ref, nd_ref, c_ref, bn_ref, qt_ref,
                       *, tm, half):
    h = x_ref[...] * (1.0 + mod_ref[SC2:SC2 + 1, :]) + mod_ref[SH2:SH2 + 1, :]
    qt_ref[...] = _bdot_nt(wq_ref[...], h).astype(BF16)
    k = PEER_TOPK
    for hd in range(PEER_HEADS):
        for lt in range(tm // LANES):
            lanes = slice(lt * LANES, (lt + 1) * LANES)
            r0 = hd * 2 * half
            s1 = jnp.dot(keys_ref[0], qt_ref[r0:r0 + half, lanes], preferred_element_type=F32)
            s2 = jnp.dot(keys_ref[1], qt_ref[r0 + half:r0 + 2 * half, lanes],
                         preferred_element_type=F32)
            t1 = _top_values(s1, k)
            t2 = _top_values(s2, k)
            t2_all = jnp.concatenate(t2, axis=0)
            t1_tail = jnp.concatenate(t1[k // 2:], axis=0)
            pieces = []
            for a in range(k // 2):
                nb = min(k, (k + 1) // (a + 1))
                pieces.append(t1[a] + t2_all[:nb, :])
            pieces.append(t1_tail + t2[0])
            n_cand = sum(p.shape[0] for p in pieces)
            pad = (-n_cand) % SUBLANES
            if pad:
                pieces.append(jnp.full((pad, LANES), NEG_INF, F32))
            cand = jnp.concatenate(pieces, axis=0)
            top = _top_values(cand, k + 1)
            kth = top[k - 1]
            thr = 0.5 * (kth + jnp.maximum(top[k], kth - 1.0))
            m1, m2 = t1[0], t2[0]
            z = jnp.sum(jnp.where(cand >= thr, jnp.exp(cand - (m1 + m2)), 0.0), axis=0, keepdims=True)
            a_ref[hd, :, 0, lanes] = jnp.where(s1 >= t1[k - 1], jnp.exp(s1 - m1), 0.0)
            nd_ref[hd, :, 0, lanes] = -s1
            c_ref[hd, :, lanes] = s2 - thr
            bn_ref[hd, :, _paired_lanes(lt, tm)] = jnp.where(s2 >= t2[k - 1], jnp.exp(s2 - m2), 0.0) / (2.0 * z)


def _peer_route(x, mod_l, w_q, keys, tm, stream_of_tile):
    ntok, d = x.shape
    nq = w_q.shape[1]
    half = keys.shape[2]
    nk = keys.shape[1]
    n_tiles = ntok // tm
    sel_spec = pl.BlockSpec((PEER_HEADS, nk, tm), lambda t: (0, 0, t))
    row_spec = pl.BlockSpec((PEER_HEADS, nk, 1, tm), lambda t: (0, 0, 0, t))
    return pl.pallas_call(
        functools.partial(_peer_route_kernel, tm=tm, half=half),
        grid=(n_tiles,),
        in_specs=[
            pl.BlockSpec((tm, d), lambda t: (t, 0)),
            pl.BlockSpec((None, N_MOD, d), lambda t: (stream_of_tile(t), 0, 0)),
            pl.BlockSpec((nq, d), lambda t: (0, 0)),
            pl.BlockSpec(keys.shape, lambda t: (0, 0, 0)),
        ],
        out_specs=[row_spec, row_spec, sel_spec, sel_spec],
        out_shape=[jax.ShapeDtypeStruct((PEER_HEADS, nk, 1, ntok), F32)] * 2
        + [jax.ShapeDtypeStruct((PEER_HEADS, nk, ntok), F32)] * 2,
        scratch_shapes=[pltpu.VMEM((nq, tm), BF16)],
        compiler_params=_cparams("parallel"),
        name="peer_route",
    )(x, mod_l, w_q.T.astype(BF16), keys.astype(BF16))


def _gelu_twice(a):
    return a * (1.0 + lax.erf(a * (1.0 / math.sqrt(2.0))))


def _peer_dense_kernel(x_ref, mod_ref, a_ref, nd_ref, c_ref, bn_ref, u_next_ref, vt_prev_ref, u_first_ref,
                       vt_last_ref, lg_ref, lb_ref, o_ref, hbf, acc, wt, at,
                       *, tm, groups, fk, nk, dn_alpha, row_chunk, rc_block):
    e = pl.program_id(1)
    assert groups % 2 == 0

    @pl.when(e == 0)
    def _():
        h = x_ref[...] * (1.0 + mod_ref[SC2:SC2 + 1, :]) + mod_ref[SH2:SH2 + 1, :]
        hbf[...] = h.T.astype(BF16)
        acc[...] = jnp.zeros_like(acc)
        at[0] = jnp.dot(u_first_ref[0], hbf[...], preferred_element_type=F32)
        wt[1] = jnp.zeros(wt.shape[1:], BF16)

    def group_step(p, carry):
        slot = p % 2
        at[1 - slot] = jnp.dot(u_next_ref[p], hbf[...], preferred_element_type=F32)
        acc[...] += jnp.dot(vt_prev_ref[p], wt[1 - slot], preferred_element_type=F32)
        for lt in range(tm // LANES):
            lanes = slice(lt * LANES, (lt + 1) * LANES)
            for rb in range(0, nk // row_chunk, rc_block):
                gates = [[None] * rc_block for _ in range(fk)]
                for hd in range(PEER_HEADS):
                    a_rows = [jnp.broadcast_to(a_ref[hd, p * fk + q, :, lanes], (row_chunk, LANES))
                              for q in range(fk)]
                    nd_rows = [jnp.broadcast_to(nd_ref[hd, p * fk + q, :, lanes], (row_chunk, LANES))
                               for q in range(fk)]
                    for r in range(rc_block):
                        rows = slice((rb + r) * row_chunk, (rb + r + 1) * row_chunk)
                        cv = c_ref[hd, rows, lanes]
                        bv = bn_ref[hd, rows, _paired_lanes(lt, tm)]
                        for q in range(fk):
                            g = jnp.where(cv >= nd_rows[q], bv * a_rows[q], 0.0)
                            gates[q][r] = g if gates[q][r] is None else gates[q][r] + g
                for q in range(fk):
                    for r in range(rc_block):
                        erows = slice(q * nk + (rb + r) * row_chunk, q * nk + (rb + r + 1) * row_chunk)
                        wt[slot, erows, lanes] = (_gelu_twice(at[slot, erows, lanes]) * gates[q][r]).astype(BF16)
        return carry

    lax.fori_loop(0, groups, group_step, 0)

    @pl.when(e == pl.num_programs(1) - 1)
    def _():
        acc[...] += jnp.dot(vt_last_ref[0], wt[(groups - 1) % 2], preferred_element_type=F32)
        f = acc[...].T
        v = dn_alpha * x_ref[...] + mod_ref[G2:G2 + 1, :] * f
        o_ref[...] = _layer_norm_rows(v, lg_ref[...], lb_ref[...])


PEER_FIRST_KEYS_PER_PASS = 4
PEER_GROUPS_PER_STEP = 4
PEER_ROW_CHUNKS_PER_BLOCK = 2


def _peer_dense(x, mod_l, sel, u_tab, v_tab, ln_g, ln_b, tm, stream_of_tile, dn_alpha):
    ntok, d = x.shape
    n_exp = u_tab.shape[0]
    a_sel, nd_sel, c_sel, bn_sel = sel
    nk = c_sel.shape[1]
    n_tiles = ntok // tm
    fk, groups = PEER_FIRST_KEYS_PER_PASS, PEER_GROUPS_PER_STEP
    ge = fk * nk
    n_groups = n_exp // ge
    u3 = u_tab.reshape(n_groups, ge, d)
    v3 = v_tab.reshape(n_groups, ge, d)
    u_next = jnp.concatenate([u3[1:], u3[:1]], axis=0).astype(BF16)
    vt_prev = jnp.concatenate([v3[-1:], v3[:-1]], axis=0).transpose(0, 2, 1).astype(BF16)
    u_first = u3[:1].astype(BF16)
    vt_last = v3[-1:].transpose(0, 2, 1).astype(BF16)
    row_spec = pl.BlockSpec((PEER_HEADS, groups * fk, 1, tm), lambda t, e: (0, e, 0, t))
    full_spec = pl.BlockSpec((PEER_HEADS, nk, tm), lambda t, e: (0, 0, t))
    kern = functools.partial(_peer_dense_kernel, tm=tm, groups=groups, fk=fk, nk=nk, dn_alpha=dn_alpha,
                             row_chunk=2 * SUBLANES, rc_block=PEER_ROW_CHUNKS_PER_BLOCK)
    return pl.pallas_call(
        kern,
        grid=(n_tiles, n_groups // groups),
        in_specs=[
            pl.BlockSpec((tm, d), lambda t, e: (t, 0)),
            pl.BlockSpec((None, N_MOD, d), lambda t, e: (stream_of_tile(t), 0, 0)),
            row_spec, row_spec, full_spec, full_spec,
            pl.BlockSpec((groups, ge, d), lambda t, e: (e, 0, 0)),
            pl.BlockSpec((groups, d, ge), lambda t, e: (e, 0, 0)),
            pl.BlockSpec((1, ge, d), lambda t, e: (0, 0, 0)),
            pl.BlockSpec((1, d, ge), lambda t, e: (0, 0, 0)),
            pl.BlockSpec((1, d), lambda t, e: (0, 0)),
            pl.BlockSpec((1, d), lambda t, e: (0, 0)),
        ],
        out_specs=pl.BlockSpec((tm, d), lambda t, e: (t, 0)),
        out_shape=jax.ShapeDtypeStruct((ntok, d), F32),
        scratch_shapes=[
            pltpu.VMEM((d, tm), BF16),
            pltpu.VMEM((d, tm), F32),
            pltpu.VMEM((2, ge, tm), BF16),
            pltpu.VMEM((2, ge, tm), F32),
        ],
        compiler_params=_cparams("parallel", "arbitrary"),
        name="peer_dense",
    )(x, mod_l, a_sel, nd_sel, c_sel, bn_sel, u_next, vt_prev, u_first, vt_last, ln_g.reshape(1, d), ln_b.reshape(1, d))


def _peer_layer(x, mod_l, w_q, keys, u_tab, v_tab, ln_g, ln_b, tm, stream_of_tile, dn_alpha):
    sel = _peer_route(x, mod_l, w_q, keys, tm, stream_of_tile)
    return _peer_dense(x, mod_l, sel, u_tab, v_tab, ln_g, ln_b, tm, stream_of_tile, dn_alpha)


def _rope(v, cos, sin_signed):
    n = v.shape[-1]
    lane = lax.broadcasted_iota(jnp.int32, v.shape, v.ndim - 1)
    quarter = sin_signed.shape[-1] // 8
    partner = jnp.where(lane % (2 * quarter) < quarter,
                        pltpu.roll(v, n - quarter, v.ndim - 1), pltpu.roll(v, quarter, v.ndim - 1))
    reps = n // cos.shape[-1]
    if reps > 1:
        cos = jnp.concatenate([cos] * reps, axis=-1)
        sin_signed = jnp.concatenate([sin_signed] * reps, axis=-1)
    return v * cos + partner * sin_signed


def _cd_in_kernel(x_ref, mod_ref, w_ref, cos_ref, sin_ref, qg_ref, kg_ref,
                  qm_ref, km_ref, vm_ref, om_ref, qx_ref, ka_ref, va_ref, gt_ref,
                  *, d_m, d_qx, d_kv, att_dh, k_scale):
    h = x_ref[...] * (1.0 + mod_ref[SC1:SC1 + 1, :]) + mod_ref[SH1:SH1 + 1, :]
    z = _bdot(h, w_ref[...])
    qm_ref[...] = z[:, 0:d_m].astype(BF16)
    km_ref[...] = (z[:, d_m:2 * d_m] * k_scale).astype(BF16)
    vm_ref[...] = z[:, 2 * d_m:3 * d_m].astype(BF16)
    om_ref[...] = z[:, 3 * d_m:4 * d_m]
    off = 4 * d_m
    cos = cos_ref[...]
    sin = sin_ref[...]
    qx = z[:, off:off + d_qx]
    pieces = []
    for hq in range(d_qx // LANES):
        blk = qx[:, hq * LANES:(hq + 1) * LANES]
        ms = jnp.sum(blk * blk, axis=-1, keepdims=True) * (1.0 / att_dh)
        pieces.append(blk * lax.rsqrt(ms + RMS_EPS))
    qn = jnp.concatenate(pieces, axis=-1) * qg_ref[...]
    qx_ref[...] = _rope(qn, cos, sin).astype(BF16)
    off += d_qx
    kk = z[:, off:off + d_kv]
    lane = lax.broadcasted_iota(jnp.int32, kk.shape, 1)
    sq = kk * kk
    pieces = []
    for hk in range(d_kv // att_dh):
        sel = (lane >= hk * att_dh) & (lane < (hk + 1) * att_dh)
        ms = jnp.sum(jnp.where(sel, sq, 0.0), axis=-1, keepdims=True) * (1.0 / att_dh)
        pieces.append((sel, lax.rsqrt(ms + RMS_EPS)))
    scale = jnp.zeros_like(kk)
    for sel, r in pieces:
        scale = jnp.where(sel, r, scale)
    ka_ref[...] = _rope(kk * scale * kg_ref[...], cos, sin).astype(BF16)
    off += d_kv
    va_ref[...] = z[:, off:off + d_kv].astype(BF16)
    off += d_kv
    gt_ref[...] = z[:, off:]


def _cd_in(x, mod_l, w, cos_t, sin_t, q_gain, k_gain, tm, n_tiles, in_tile, stream_of_tile, pos_tile,
           d_m, d_qx, d_kv, att_dh, k_scale):
    d = x.shape[1]
    n = w.shape[1]
    ntok = n_tiles * tm
    widths = (d_m, d_m, d_m, d_m, d_qx, d_kv, d_kv, n - 4 * d_m - d_qx - 2 * d_kv)
    dtypes = (BF16, BF16, BF16, F32, BF16, BF16, BF16, F32)
    kern = functools.partial(_cd_in_kernel, d_m=d_m, d_qx=d_qx, d_kv=d_kv, att_dh=att_dh, k_scale=k_scale)
    return pl.pallas_call(
        kern,
        grid=(n_tiles,),
        in_specs=[
            pl.BlockSpec((tm, d), lambda t: (in_tile(t), 0)),
            pl.BlockSpec((None, N_MOD, d), lambda t: (stream_of_tile(t), 0, 0)),
            pl.BlockSpec((d, n), lambda t: (0, 0)),
            pl.BlockSpec((tm, cos_t.shape[1]), lambda t: (pos_tile(t), 0)),
            pl.BlockSpec((tm, sin_t.shape[1]), lambda t: (pos_tile(t), 0)),
            pl.BlockSpec((1, d_qx), lambda t: (0, 0)),
            pl.BlockSpec((1, d_kv), lambda t: (0, 0)),
        ],
        out_specs=[pl.BlockSpec((tm, wd), lambda t: (t, 0)) for wd in widths],
        out_shape=[jax.ShapeDtypeStruct((ntok, wd), dt) for wd, dt in zip(widths, dtypes)],
        compiler_params=_cparams("parallel"),
        name="cd_in",
    )(x, mod_l, w.astype(BF16), cos_t, sin_t, q_gain, k_gain)


def _mlstm_kernel(qf_ref, kf_ref, vf_ref, gf_ref, qb_ref, kb_ref, vb_ref, gb_ref, bias_ref,
                  hf_ref, hb_ref, c_s, n_s, m_s, *, chunk, dh, heads):
    s = pl.program_id(1)

    @pl.when(s == 0)
    def _():
        c_s[...] = jnp.zeros_like(c_s)
        n_s[...] = jnp.zeros_like(n_s)
        m_s[...] = jnp.zeros_like(m_s)

    ri = lax.broadcasted_iota(jnp.int32, (chunk, chunk), 0)
    cj = lax.broadcasted_iota(jnp.int32, (chunk, chunk), 1)
    streams = ((qf_ref, kf_ref, vf_ref, gf_ref, hf_ref), (qb_ref, kb_ref, vb_ref, gb_ref, hb_ref))
    for direction, (q_ref, k_ref, v_ref, g_ref, h_ref) in enumerate(streams):
        mask = (cj <= ri) if direction == 0 else (cj >= ri)
        edge = chunk - 1 if direction == 0 else 0
        gates = g_ref[...] + bias_ref[...]
        logf = jax.nn.log_sigmoid(gates)
        bc = jnp.dot(mask.astype(F32), logf, preferred_element_type=F32, precision=lax.Precision.HIGHEST)
        br = bc.T
        gr = gates.T
        for hd in range(heads):
            ci = (2 * direction) * heads + hd
            cf = (2 * direction + 1) * heads + hd
            idx = direction * heads + hd
            b_col = bc[:, cf:cf + 1]
            b_row = br[cf:cf + 1, :]
            ig_row = gr[ci:ci + 1, :]
            ig_col = gates[:, ci:ci + 1]
            m_prev = m_s[idx][:, 0:1]
            dmat = jnp.where(mask, b_col - b_row + ig_row, NEG_INF)
            inter = b_col + m_prev
            m_t = jnp.maximum(inter, jnp.max(dmat, axis=-1, keepdims=True))
            dexp = jnp.exp(dmat - m_t)
            w_inter = jnp.exp(inter - m_t)
            cols = slice(hd * dh, (hd + 1) * dh)
            q = q_ref[:, cols]
            k = k_ref[:, cols]
            v = v_ref[:, cols]
            c_prev = c_s[idx]
            n_prev = n_s[idx]
            sm = _bdot_nt(q, k) * dexp
            num = _bdot(sm, v) + w_inter * _bdot(q, c_prev)
            den = (jnp.sum(sm, axis=-1, keepdims=True)
                   + w_inter * jnp.sum(q.astype(F32) * n_prev, axis=-1, keepdims=True))
            h_ref[:, cols] = num / jnp.maximum(jnp.abs(den), jnp.exp(-m_t))
            b_last = bc[edge:edge + 1, cf:cf + 1]
            g_log = b_last - b_col + ig_col
            m_new = jnp.maximum(b_last + m_prev, jnp.max(g_log, axis=0, keepdims=True))
            wk = jnp.exp(g_log - m_new)
            decay = jnp.exp(b_last + m_prev - m_new)
            kw = k.astype(F32) * wk
            c_s[idx] = decay * c_prev + lax.dot_general(
                kw.astype(BF16), v, (((0,), (0,)), ((), ())), preferred_element_type=F32)
            n_s[idx] = decay * n_prev + jnp.sum(kw, axis=0, keepdims=True)
            m_s[idx] = jnp.broadcast_to(m_new, (1, LANES))


def _mlstm(qm, km, vm, gt, gate_bias, n_batch, chunks_per_seq, ctx_chunks, heads, dh):
    chunk = MLSTM_CHUNK
    ntok, dm = qm.shape
    ng = gt.shape[1]

    def fwd(b, s):
        return (b * chunks_per_seq + s, 0)

    def bwd(b, s):
        r = jnp.where(s < ctx_chunks, ctx_chunks - 1 - s, ctx_chunks + chunks_per_seq - 1 - s)
        return (b * chunks_per_seq + r, 0)

    tok_f = pl.BlockSpec((chunk, dm), fwd)
    tok_b = pl.BlockSpec((chunk, dm), bwd)
    kern = functools.partial(_mlstm_kernel, chunk=chunk, dh=dh, heads=heads)
    return pl.pallas_call(
        kern,
        grid=(n_batch, chunks_per_seq),
        in_specs=[tok_f, tok_f, tok_f, pl.BlockSpec((chunk, ng), fwd),
                  tok_b, tok_b, tok_b, pl.BlockSpec((chunk, ng), bwd),
                  pl.BlockSpec((1, ng), lambda b, s: (0, 0))],
        out_specs=[pl.BlockSpec((chunk, dm), fwd), pl.BlockSpec((chunk, dm), bwd)],
        out_shape=[jax.ShapeDtypeStruct((ntok, dm), F32)] * 2,
        scratch_shapes=[
            pltpu.VMEM((2 * heads, dh, dh), F32),
            pltpu.VMEM((2 * heads, 1, dh), F32),
            pltpu.VMEM((2 * heads, 1, LANES), F32),
        ],
        compiler_params=_cparams("parallel", "arbitrary"),
        name="mlstm",
    )(qm, km, vm, gt, qm, km, vm, gt, gate_bias)


def _attn_kernel(q_ref, k_ref, v_ref, o_ref, *, n_qh):
    k = k_ref[...]
    v = v_ref[...]
    for hq in range(n_qh):
        cols = slice(hq * LANES, (hq + 1) * LANES)
        s = _bdot_nt(q_ref[:, cols], k)
        p = jnp.exp2(s - jnp.max(s, axis=-1, keepdims=True))
        l = jnp.sum(p, axis=-1, keepdims=True)
        o = jnp.dot(p.astype(BF16), v, preferred_element_type=F32)
        o_ref[:, cols] = (o / l).astype(o_ref.dtype)


def _attention(qx, ka, va, tq, n_batch, seq_tiles, q_tile_of, kv_len):
    n_qh = qx.shape[1] // LANES
    return pl.pallas_call(
        functools.partial(_attn_kernel, n_qh=n_qh),
        grid=(n_batch, seq_tiles),
        in_specs=[
            pl.BlockSpec((tq, qx.shape[1]), lambda b, t: (q_tile_of(b, t), 0)),
            pl.BlockSpec((kv_len, ka.shape[1]), lambda b, t: (b, 0)),
            pl.BlockSpec((kv_len, va.shape[1]), lambda b, t: (b, 0)),
        ],
        out_specs=pl.BlockSpec((tq, qx.shape[1]), lambda b, t: (b * seq_tiles + t, 0)),
        out_shape=jax.ShapeDtypeStruct((n_batch * seq_tiles * tq, qx.shape[1]), BF16),
        compiler_params=_cparams("parallel", "parallel"),
        name="attention",
    )(qx, ka, va)


def _cd_tail_kernel(hf_ref, hb_ref, om_ref, att_ref, x_ref, mod_ref, ng_ref, wm_ref, wa_ref, lg_ref, lb_ref,
                    o_ref, *, dh, heads, dn_alpha):
    hsum = hf_ref[...] + hb_ref[...]
    pieces = []
    for hd in range(heads):
        cols = slice(hd * dh, (hd + 1) * dh)
        blk = hsum[:, cols]
        mu = jnp.mean(blk, axis=-1, keepdims=True)
        dlt = blk - mu
        var = jnp.mean(dlt * dlt, axis=-1, keepdims=True)
        pieces.append(dlt * lax.rsqrt(var + LN_EPS))
    hn = jnp.concatenate(pieces, axis=-1) * ng_ref[...] * jax.nn.sigmoid(om_ref[...])
    y = _bdot(hn, wm_ref[...]) + jnp.dot(att_ref[...], wa_ref[...], preferred_element_type=F32)
    v = dn_alpha * x_ref[...] + mod_ref[G1:G1 + 1, :] * y
    o_ref[...] = _layer_norm_rows(v, lg_ref[...], lb_ref[...])


def _cd_tail(hf, hb, om, att, x, mod_l, norm_g, w_m, w_a, ln_g, ln_b, tm, n_tiles, seq_tile_of,
             stream_of_tile, heads, dh, dn_alpha):
    d = x.shape[1]
    dm = hf.shape[1]
    const2 = lambda t: (0, 0)
    kern = functools.partial(_cd_tail_kernel, dh=dh, heads=heads, dn_alpha=dn_alpha)
    return pl.pallas_call(
        kern,
        grid=(n_tiles,),
        in_specs=[
            pl.BlockSpec((tm, dm), lambda t: (seq_tile_of(t), 0)),
            pl.BlockSpec((tm, dm), lambda t: (seq_tile_of(t), 0)),
            pl.BlockSpec((tm, dm), lambda t: (seq_tile_of(t), 0)),
            pl.BlockSpec((tm, att.shape[1]), lambda t: (t, 0)),
            pl.BlockSpec((tm, d), lambda t: (t, 0)),
            pl.BlockSpec((None, N_MOD, d), lambda t: (stream_of_tile(t), 0, 0)),
            pl.BlockSpec((1, dm), const2),
            pl.BlockSpec(w_m.shape, const2),
            pl.BlockSpec(w_a.shape, const2),
            pl.BlockSpec((1, d), const2),
            pl.BlockSpec((1, d), const2),
        ],
        out_specs=pl.BlockSpec((tm, d), lambda t: (t, 0)),
        out_shape=jax.ShapeDtypeStruct((n_tiles * tm, d), F32),
        compiler_params=_cparams("parallel"),
        name="cd_tail",
    )(hf, hb, om, att, x, mod_l, norm_g.reshape(1, dm), w_m.astype(BF16), w_a.astype(BF16),
      ln_g.reshape(1, d), ln_b.reshape(1, d))


def _layer0(xf, mod_l, p, geom):
    tm = geom["tm_mix"]
    n_tiles = xf.shape[0] // tm
    n_lat_tiles = geom["n_lat"] // tm
    lat_per_seq = geom["seq"] // tm
    ctx_per_seq = geom["ctx"] // tm
    n_batch = geom["batch"]

    def stream(t):
        return jnp.where(t < n_lat_tiles, t // lat_per_seq, n_batch)

    (z,) = _modulated_matmul(xf, mod_l, p["ab_w_in"], (p["ab_w_in"].shape[1],), (F32,), tm,
                             lambda t: t, stream, n_tiles)
    x1 = _ab_tail(z, xf, mod_l, p["pool_w"], p["pool_ls"], p["conv_w"], p["conv_b"], p["conv_ln_g"],
                  p["conv_ln_b"], p["ab_w_out"], p["ln_g"][0], p["ln_b"][0], tm, n_lat_tiles,
                  lat_per_seq, ctx_per_seq, stream, geom["dn_alpha"])
    tp = geom["tm_peer"]
    n_lat_p = geom["n_lat"] // tp
    lat_per_seq_p = geom["seq"] // tp

    def stream_p(t):
        return jnp.where(t < n_lat_p, t // lat_per_seq_p, n_batch)

    return _peer_layer(x1, mod_l, p["peer_w_q"], p["peer_keys"], p["peer_u"], p["peer_v"],
                       p["ln_g"][1], p["ln_b"][1], tp, stream_p, geom["dn_alpha"])


def _rope_tables(seq, ctx_len, att_dh):
    n_freq = att_dh // 4
    t = jnp.arange(seq)
    freqs = ROPE_THETA ** (-jnp.arange(n_freq, dtype=F32) / n_freq)
    ar = (t // GRID_W).astype(F32)[:, None] * freqs
    ac = (t % GRID_W).astype(F32)[:, None] * freqs
    cos = jnp.concatenate([jnp.cos(ar), jnp.cos(ar), jnp.cos(ac), jnp.cos(ac)], axis=-1)
    sin = jnp.concatenate([-jnp.sin(ar), jnp.sin(ar), -jnp.sin(ac), jnp.sin(ac)], axis=-1)
    cos = jnp.concatenate([jnp.ones((ctx_len, att_dh), F32), cos], axis=0)
    sin = jnp.concatenate([jnp.zeros((ctx_len, att_dh), F32), sin], axis=0)
    reps = LANES // att_dh
    return jnp.tile(cos, (1, reps)), jnp.tile(sin, (1, reps))


def _layer1(xf, mod_l, p, geom):
    tm = geom["tm_mix"]
    n_batch, seq, ctx_len = geom["batch"], geom["seq"], geom["ctx"]
    d = xf.shape[1]
    heads = MLSTM_HEADS
    d_m = p["mlstm_norm_g"].shape[0]
    dh = d_m // heads
    att_dh = p["q_norm_g"].shape[0]
    n_qh, n_kvh = ATT_QH, ATT_KVH
    group = n_qh // n_kvh
    d_q, d_kv = n_qh * att_dh, n_kvh * att_dh
    n_gate = 4 * heads
    ctx_tiles, lat_tiles = ctx_len // tm, seq // tm
    per_b = ctx_tiles + lat_tiles
    n_lat_tiles = n_batch * lat_tiles

    w = p["cd_w_in"]
    cuts = np.cumsum([0, d_m, d_m, d_m, d_m, n_gate, d_q, d_kv, d_kv])
    w_qm, w_km, w_vm, w_om, w_gt, w_qa, w_ka, w_va = (w[:, cuts[i]:cuts[i + 1]] for i in range(8))
    def head_tiles(a, axis):
        parts = []
        for g in range(n_kvh):
            pad = [(0, 0)] * a.ndim
            pad[axis + 1] = (g * att_dh, LANES - (g + 1) * att_dh)
            parts.append(jnp.pad(lax.slice_in_dim(a, g * group, (g + 1) * group, axis=axis), pad))
        return jnp.concatenate(parts, axis=axis)

    w_qx = head_tiles(w_qa.reshape(d, n_qh, att_dh), 1)
    q_gain = head_tiles(jnp.broadcast_to(p["q_norm_g"] * (att_dh ** -0.5 * math.log2(math.e)), (n_qh, att_dh)), 0)
    w_att = head_tiles(p["cd_w_out"][d_m:].reshape(n_qh, att_dh, d), 0)
    d_qx = n_qh * LANES
    w_all = jnp.concatenate([w_qm, w_km, w_vm, w_om, w_qx.reshape(d, d_qx), w_ka, w_va,
                             jnp.pad(w_gt, ((0, 0), (0, LANES - n_gate)))], axis=1)
    gate_bias = jnp.pad(p["mlstm_gate_b"].reshape(1, n_gate), ((0, 0), (0, LANES - n_gate)))
    cos_t, sin_t = _rope_tables(seq, ctx_len, att_dh)

    def in_tile(t):
        b, r = t // per_b, t % per_b
        return jnp.where(r < ctx_tiles, n_lat_tiles + b * ctx_tiles + r, b * lat_tiles + r - ctx_tiles)

    def stream_seq(t):
        return jnp.where(t % per_b < ctx_tiles, n_batch, t // per_b)

    qm, km, vm, om, qx, ka, va, gt = _cd_in(
        xf, mod_l, w_all, cos_t, sin_t, q_gain.reshape(1, d_qx), jnp.tile(p["k_norm_g"], n_kvh).reshape(1, d_kv),
        tm, n_batch * per_b, in_tile, stream_seq, lambda t: t % per_b, d_m, d_qx, d_kv, att_dh, dh ** -0.5)

    chunks_per_seq = (ctx_len + seq) // MLSTM_CHUNK
    hf, hb = _mlstm(qm, km, vm, gt, gate_bias, n_batch, chunks_per_seq, ctx_len // MLSTM_CHUNK, heads, dh)
    att = _attention(qx, ka, va, tm, n_batch, lat_tiles, lambda b, t: b * per_b + ctx_tiles + t,
                     ctx_len + seq)

    def seq_tile_of(t):
        return (t // lat_tiles) * per_b + ctx_tiles + t % lat_tiles

    x1 = _cd_tail(hf, hb, om, att, xf, mod_l, p["mlstm_norm_g"], p["cd_w_out"][:d_m], w_att.reshape(d_qx, d),
                  p["ln_g"][0], p["ln_b"][0], tm, n_lat_tiles, seq_tile_of, lambda t: t // lat_tiles,
                  heads, dh, geom["dn_alpha"])
    tp = geom["tm_peer"]
    lat_per_seq_p = seq // tp
    return _peer_layer(x1, mod_l, p["peer_w_q"], p["peer_keys"], p["peer_u"], p["peer_v"],
                       p["ln_g"][1], p["ln_b"][1], tp, lambda t: t // lat_per_seq_p, geom["dn_alpha"])


def kernel(x, c, ctx, c_ctx, mod_w, mod_b, ln_g, ln_b, ab_w_in, pool_w, pool_ls, conv_w, conv_b,
           conv_ln_g, conv_ln_b, ab_w_out, cd_w_in, mlstm_gate_b, mlstm_norm_g, q_norm_g, k_norm_g,
           cd_w_out, peer_w_q, peer_keys, peer_u, peer_v):
    n_batch, seq, d = x.shape
    ctx_len = ctx.shape[1]
    depth = mod_w.shape[0]
    assert depth == 2, "one pooling/convolution layer followed by one mLSTM/attention layer"
    n_streams = SUBLANES * (-(-(n_batch + 1) // SUBLANES))
    cc = jnp.concatenate([c, c_ctx[None], jnp.zeros((n_streams - n_batch - 1, d), F32)], axis=0)
    mod = _modulation(cc, mod_w, mod_b)
    geom = dict(tm_mix=256, tm_peer=512, n_lat=n_batch * seq, seq=seq, ctx=ctx_len, batch=n_batch,
                dn_alpha=(2 * depth) ** 0.25)
    xf = jnp.concatenate([x.reshape(-1, d), ctx.reshape(-1, d)], axis=0)
    p0 = dict(ab_w_in=ab_w_in[0], pool_w=pool_w[0], pool_ls=pool_ls[0], conv_w=conv_w[0],
              conv_b=conv_b[0], conv_ln_g=conv_ln_g[0], conv_ln_b=conv_ln_b[0], ab_w_out=ab_w_out[0],
              peer_w_q=peer_w_q[0], peer_keys=peer_keys[0], peer_u=peer_u[0], peer_v=peer_v[0],
              ln_g=ln_g[0], ln_b=ln_b[0])
    xf = _layer0(xf, mod[0], p0, geom)
    p1 = dict(cd_w_in=cd_w_in[0], mlstm_gate_b=mlstm_gate_b[0], mlstm_norm_g=mlstm_norm_g[0],
              q_norm_g=q_norm_g[0], k_norm_g=k_norm_g[0], cd_w_out=cd_w_out[0],
              peer_w_q=peer_w_q[1], peer_keys=peer_keys[1], peer_u=peer_u[1], peer_v=peer_v[1],
              ln_g=ln_g[1], ln_b=ln_b[1])
    out = _layer1(xf, mod[1], p1, geom)
    return out.reshape(n_batch, seq, d)
```

```python
import functools
import math

import jax
import jax.numpy as jnp
import numpy as np
from jax import lax
from jax.experimental import pallas as pl
from jax.experimental.pallas import tpu as pltpu

F32 = jnp.float32
BF16 = jnp.bfloat16

LANES = 128
SUBLANES = 8
VMEM_LIMIT_BYTES = 56 * 1024 * 1024

GRID_W = 64
N_MOD = 6
POOL_WINDOWS = (2, 4, 8, 16)
CONV_W = 31
HALO = 16
MLSTM_HEADS = 4
MLSTM_CHUNK = 128
ATT_QH = 8
ATT_KVH = 2
ROPE_THETA = 10000.0
PEER_HEADS = 8
PEER_NKEYS = 128
PEER_TOPK = 16
LN_EPS = 1e-5
RMS_EPS = 1e-6
NEG_INF = float("-inf")

SH1, SC1, G1, SH2, SC2, G2 = range(N_MOD)


def _cparams(*sem):
    return pltpu.CompilerParams(dimension_semantics=sem, vmem_limit_bytes=VMEM_LIMIT_BYTES)


def _bdot(a, b):
    return jnp.dot(a.astype(BF16), b.astype(BF16), preferred_element_type=F32)


def _bdot_nt(a, b):
    return lax.dot_general(a.astype(BF16), b.astype(BF16), (((1,), (1,)), ((), ())),
                           preferred_element_type=F32)


def _layer_norm_rows(v, g, b):
    mu = jnp.mean(v, axis=-1, keepdims=True)
    d = v - mu
    var = jnp.mean(d * d, axis=-1, keepdims=True)
    return d * lax.rsqrt(var + LN_EPS) * g + b


def _mod_kernel(c_ref, w_ref, b_ref, o_ref):
    c = c_ref[...]
    a = c * jax.nn.sigmoid(c)
    o_ref[...] = jnp.dot(a, w_ref[...], preferred_element_type=F32,
                         precision=lax.Precision.HIGHEST) + b_ref[...]


def _modulation(cc, mod_w, mod_b):
    depth, d, n = mod_w.shape
    rows = cc.shape[0]
    bn = d
    out = pl.pallas_call(
        _mod_kernel,
        grid=(depth, n // bn),
        in_specs=[
            pl.BlockSpec((rows, d), lambda l, j: (0, 0)),
            pl.BlockSpec((None, d, bn), lambda l, j: (l, 0, j)),
            pl.BlockSpec((None, 1, bn), lambda l, j: (l, 0, j)),
        ],
        out_specs=pl.BlockSpec((None, rows, bn), lambda l, j: (l, 0, j)),
        out_shape=jax.ShapeDtypeStruct((depth, rows, n), F32),
        compiler_params=_cparams("parallel", "parallel"),
        name="modulation",
    )(cc, mod_w, mod_b.reshape(depth, 1, n))
    return out.reshape(depth, rows, N_MOD, d)


def _modmm_kernel(xl_ref, xc_ref, mod_ref, w_ref, *o_refs, splits, n_lat_tiles):
    x = jnp.where(pl.program_id(0) < n_lat_tiles, xl_ref[...], xc_ref[...])
    h = x * (1.0 + mod_ref[SC1:SC1 + 1, :]) + mod_ref[SH1:SH1 + 1, :]
    z = _bdot(h, w_ref[...])
    off = 0
    for o_ref, n in zip(o_refs, splits):
        o_ref[...] = z[:, off:off + n].astype(o_ref.dtype)
        off += n


def _stream_tile_specs(tm, d, n_lat_tiles, n_ctx_tiles):
    return [pl.BlockSpec((tm, d), lambda t: (jnp.minimum(t, n_lat_tiles - 1), 0)),
            pl.BlockSpec((tm, d), lambda t: (jnp.clip(t - n_lat_tiles, 0, n_ctx_tiles - 1), 0))]


def _modulated_matmul(x_lat, x_ctx, mod_l, w, splits, out_dtypes, tm, stream_of_tile):
    d = x_lat.shape[1]
    n = w.shape[1]
    assert sum(splits) == n
    n_lat_tiles, n_ctx_tiles = x_lat.shape[0] // tm, x_ctx.shape[0] // tm
    n_tiles = n_lat_tiles + n_ctx_tiles
    return pl.pallas_call(
        functools.partial(_modmm_kernel, splits=splits, n_lat_tiles=n_lat_tiles),
        grid=(n_tiles,),
        in_specs=_stream_tile_specs(tm, d, n_lat_tiles, n_ctx_tiles) + [
            pl.BlockSpec((None, N_MOD, d), lambda t: (stream_of_tile(t), 0, 0)),
            pl.BlockSpec((d, n), lambda t: (0, 0)),
        ],
        out_specs=[pl.BlockSpec((tm, s), lambda t: (t, 0)) for s in splits],
        out_shape=[jax.ShapeDtypeStruct((n_tiles * tm, s), dt) for s, dt in zip(splits, out_dtypes)],
        compiler_params=_cparams("parallel"),
        name="modulated_matmul",
    )(x_lat, x_ctx, mod_l, w.astype(BF16))


def _ab_tail_kernel(zp_ref, zm_ref, zn_ref, xl_ref, xc_ref, mod_ref, pw_ref, pls_ref, cw_ref, cb_ref,
                    cg_ref, cbb_ref, wo_ref, lg_ref, lb_ref, o_ref, zpool, ubuf, ymix, ushift,
                    *, tm, d_pool, d_conv, n_lat_tiles, lat_tiles_per_seq, ctx_tiles_per_seq,
                    dn_alpha, row_chunk):
    t = pl.program_id(0)
    is_lat = t < n_lat_tiles
    per_seq = jnp.where(is_lat, lat_tiles_per_seq, ctx_tiles_per_seq)
    pos_tile = jnp.where(is_lat, t, t - n_lat_tiles) % per_seq
    first = pos_tile == 0
    last = pos_tile == per_seq - 1
    seq_len = per_seq * tm

    def glu(z):
        return z[:, d_pool:d_pool + d_conv] * jax.nn.sigmoid(z[:, d_pool + d_conv:])

    zp = jnp.where(first, 0.0, zp_ref[...])
    zn = jnp.where(last, 0.0, zn_ref[...])
    zm = zm_ref[...]
    zpool[0:HALO, :] = zp[:, :d_pool]
    zpool[HALO:HALO + tm, :] = zm[:, :d_pool]
    zpool[HALO + tm:, :] = zn[:, :d_pool]
    ubuf[0:HALO, :] = glu(zp)
    ubuf[HALO:HALO + tm, :] = glu(zm)
    ubuf[HALO + tm:2 * HALO + tm, :] = glu(zn)

    gw = d_pool // len(POOL_WINDOWS)
    tpos = pos_tile * tm + lax.broadcasted_iota(jnp.int32, (tm, gw), 0)
    for g, w in enumerate(POOL_WINDOWS):
        cols = slice(g * gw, (g + 1) * gw)
        acc = zpool[HALO - w // 2:HALO - w // 2 + tm, cols]
        for s in range(1, w):
            acc = acc + zpool[HALO - w // 2 + s:HALO - w // 2 + s + tm, cols]
        cnt = jnp.minimum(tpos + w // 2, seq_len) - jnp.maximum(tpos - w // 2, 0)
        diff = acc / cnt.astype(F32) - zpool[HALO:HALO + tm, cols]
        ymix[:, cols] = (_bdot(diff, pw_ref[g]) * pls_ref[:, cols]).astype(BF16)

    half = CONV_W // 2
    span = tm + 2 * HALO
    ubuf[span:, :] = jnp.zeros((SUBLANES, d_conv), F32)
    for s in range(SUBLANES):
        ushift[s] = ubuf[s:s + span, :]
    for r in range(0, tm, row_chunk):
        base = HALO - half + r
        acc = None
        for k in range(CONV_W):
            s = (base + k) % SUBLANES
            lo = base + k - s
            term = ushift[s, lo:lo + row_chunk, :] * cw_ref[k:k + 1, :]
            acc = term if acc is None else acc + term
        yn = _layer_norm_rows(acc + cb_ref[...], cg_ref[...], cbb_ref[...])
        ymix[r:r + row_chunk, d_pool:] = (yn * jax.nn.sigmoid(yn)).astype(BF16)

    y = jnp.dot(ymix[...], wo_ref[...], preferred_element_type=F32)
    x = jnp.where(is_lat, xl_ref[...], xc_ref[...])
    v = dn_alpha * x + mod_ref[G1:G1 + 1, :] * y
    o_ref[...] = _layer_norm_rows(v, lg_ref[...], lb_ref[...])


def _ab_tail(z, x_lat, x_ctx, mod_l, pool_w, pool_ls, conv_w, conv_b, cln_g, cln_b, w_out, ln_g, ln_b,
             tm, n_lat_tiles, lat_tiles_per_seq, ctx_tiles_per_seq, stream_of_tile, dn_alpha):
    d = x_lat.shape[1]
    ntok = x_lat.shape[0] + x_ctx.shape[0]
    d_pool = pool_ls.shape[0]
    d_conv = conv_b.shape[0]
    n_tiles = ntok // tm
    hb = tm // HALO
    n_hblocks = ntok // HALO
    row = lambda a: a.reshape(1, -1)
    kern = functools.partial(
        _ab_tail_kernel, tm=tm, d_pool=d_pool, d_conv=d_conv, n_lat_tiles=n_lat_tiles,
        lat_tiles_per_seq=lat_tiles_per_seq, ctx_tiles_per_seq=ctx_tiles_per_seq,
        dn_alpha=dn_alpha, row_chunk=32)
    const2 = lambda t: (0, 0)
    return pl.pallas_call(
        kern,
        grid=(n_tiles,),
        in_specs=[
            pl.BlockSpec((HALO, z.shape[1]), lambda t: (jnp.maximum(t * hb - 1, 0), 0)),
            pl.BlockSpec((tm, z.shape[1]), lambda t: (t, 0)),
            pl.BlockSpec((HALO, z.shape[1]), lambda t: (jnp.minimum((t + 1) * hb, n_hblocks - 1), 0)),
        ] + _stream_tile_specs(tm, d, n_lat_tiles, n_tiles - n_lat_tiles) + [
            pl.BlockSpec((None, N_MOD, d), lambda t: (stream_of_tile(t), 0, 0)),
            pl.BlockSpec(pool_w.shape, lambda t: (0, 0, 0)),
            pl.BlockSpec((1, d_pool), const2),
            pl.BlockSpec(conv_w.shape, const2),
            pl.BlockSpec((1, d_conv), const2),
            pl.BlockSpec((1, d_conv), const2),
            pl.BlockSpec((1, d_conv), const2),
            pl.BlockSpec(w_out.shape, const2),
            pl.BlockSpec((1, d), const2),
            pl.BlockSpec((1, d), const2),
        ],
        out_specs=pl.BlockSpec((tm, d), lambda t: (t, 0)),
        out_shape=jax.ShapeDtypeStruct((ntok, d), F32),
        scratch_shapes=[
            pltpu.VMEM((tm + 2 * HALO, d_pool), F32),
            pltpu.VMEM((tm + 2 * HALO + SUBLANES, d_conv), F32),
            pltpu.VMEM((tm, d_pool + d_conv), BF16),
            pltpu.VMEM((SUBLANES, tm + 2 * HALO, d_conv), F32),
        ],
        compiler_params=_cparams("parallel"),
        name="ab_tail",
    )(z, z, z, x_lat, x_ctx, mod_l, pool_w.astype(BF16), row(pool_ls), conv_w, row(conv_b), row(cln_g),
      row(cln_b), w_out.astype(BF16), row(ln_g), row(ln_b))


def _paired_lanes(lt, tm):
    shifted = (lt + 1) % (tm // LANES)
    return slice(shifted * LANES, (shifted + 1) * LANES)


def _top_values(s, k):
    outs = []
    cur = s
    for it in range(k):
        m = jnp.max(cur, axis=0, keepdims=True)
        outs.append(m)
        if it + 1 < k:
            cur = jnp.where(cur >= m, NEG_INF, cur)
    return outs


def _peer_route_kernel(x_ref, mod_ref, wq_ref, keys_ref, a_ref, nd_ref, c_ref, bn_ref, ht_ref, qt_ref,
                       *, tm, half):
    h = x_ref[...] * (1.0 + mod_ref[SC2:SC2 + 1, :]) + mod_ref[SH2:SH2 + 1, :]
    ht = h.T.astype(BF16)
    ht_ref[...] = ht
    qt_ref[...] = jnp.dot(wq_ref[...], ht, preferred_element_type=F32).astype(BF16)
    k = PEER_TOPK
    for hd in range(PEER_HEADS):
        for lt in range(tm // LANES):
            lanes = slice(lt * LANES, (lt + 1) * LANES)
            r0 = hd * 2 * half
            s1 = jnp.dot(keys_ref[0], qt_ref[r0:r0 + half, lanes], preferred_element_type=F32)
            s2 = jnp.dot(keys_ref[1], qt_ref[r0 + half:r0 + 2 * half, lanes],
                         preferred_element_type=F32)
            t1 = _top_values(s1, k)
            t2 = _top_values(s2, k)
            t2_all = jnp.concatenate(t2, axis=0)
            t1_tail = jnp.concatenate(t1[k // 2:], axis=0)
            pieces = []
            for a in range(k // 2):
                nb = min(k, (k + 1) // (a + 1))
                pieces.append(t1[a] + t2_all[:nb, :])
            pieces.append(t1_tail + t2[0])
            n_cand = sum(p.shape[0] for p in pieces)
            pad = (-n_cand) % SUBLANES
            if pad:
                pieces.append(jnp.full((pad, LANES), NEG_INF, F32))
            cand = jnp.concatenate(pieces, axis=0)
            top = _top_values(cand, k + 1)
            kth = top[k - 1]
            thr = 0.5 * (kth + jnp.maximum(top[k], kth - 1.0))
            m1, m2 = t1[0], t2[0]
            z = jnp.sum(jnp.where(cand >= thr, jnp.exp(cand - (m1 + m2)), 0.0), axis=0, keepdims=True)
            a_ref[hd, :, 0, lanes] = jnp.where(s1 >= t1[k - 1], jnp.exp(s1 - m1), 0.0)
            nd_ref[hd, :, 0, lanes] = -s1
            c_ref[hd, :, lanes] = s2 - thr
            bn_ref[hd, :, _paired_lanes(lt, tm)] = jnp.where(s2 >= t2[k - 1], jnp.exp(s2 - m2), 0.0) / (2.0 * z)


def _peer_route(x, mod_l, w_q, keys, tm, stream_of_tile):
    ntok, d = x.shape
    nq = w_q.shape[1]
    half = keys.shape[2]
    nk = keys.shape[1]
    n_tiles = ntok // tm
    sel_spec = pl.BlockSpec((PEER_HEADS, nk, tm), lambda t: (0, 0, t))
    row_spec = pl.BlockSpec((PEER_HEADS, nk, 1, tm), lambda t: (0, 0, 0, t))
    return pl.pallas_call(
        functools.partial(_peer_route_kernel, tm=tm, half=half),
        grid=(n_tiles,),
        in_specs=[
            pl.BlockSpec((tm, d), lambda t: (t, 0)),
            pl.BlockSpec((None, N_MOD, d), lambda t: (stream_of_tile(t), 0, 0)),
            pl.BlockSpec((nq, d), lambda t: (0, 0)),
            pl.BlockSpec(keys.shape, lambda t: (0, 0, 0)),
        ],
        out_specs=[row_spec, row_spec, sel_spec, sel_spec, pl.BlockSpec((d, tm), lambda t: (0, t))],
        out_shape=[jax.ShapeDtypeStruct((PEER_HEADS, nk, 1, ntok), F32)] * 2
        + [jax.ShapeDtypeStruct((PEER_HEADS, nk, ntok), F32)] * 2 + [jax.ShapeDtypeStruct((d, ntok), BF16)],
        scratch_shapes=[pltpu.VMEM((nq, tm), BF16)],
        compiler_params=_cparams("parallel"),
        name="peer_route",
    )(x, mod_l, w_q.T.astype(BF16), keys.astype(BF16))


def _gelu_twice(a):
    return a * (1.0 + lax.erf(a * (1.0 / math.sqrt(2.0))))


def _peer_dense_kernel(x_ref, mod_ref, a_ref, nd_ref, c_ref, bn_ref, hbf, u_next_ref, vt_prev_ref, u_first_ref,
                       vt_last_ref, lg_ref, lb_ref, o_ref, acc, wt, at,
                       *, tm, groups, fk, nk, dn_alpha, row_chunk, rc_block):
    e = pl.program_id(1)
    assert groups % 2 == 0

    @pl.when(e == 0)
    def _():
        acc[...] = jnp.zeros_like(acc)
        at[0] = jnp.dot(u_first_ref[0], hbf[...], preferred_element_type=F32)
        wt[1] = jnp.zeros(wt.shape[1:], BF16)

    def group_step(p, carry):
        slot = p % 2
        at[1 - slot] = jnp.dot(u_next_ref[p], hbf[...], preferred_element_type=F32)
        acc[...] += jnp.dot(vt_prev_ref[p], wt[1 - slot], preferred_element_type=F32)
        for lt in range(tm // LANES):
            lanes = slice(lt * LANES, (lt + 1) * LANES)
            for rb in range(0, nk // row_chunk, rc_block):
                gates = [[None] * rc_block for _ in range(fk)]
                for hd in range(PEER_HEADS):
                    a_rows = [jnp.broadcast_to(a_ref[hd, p * fk + q, :, lanes], (row_chunk, LANES))
                              for q in range(fk)]
                    nd_rows = [jnp.broadcast_to(nd_ref[hd, p * fk + q, :, lanes], (row_chunk, LANES))
                               for q in range(fk)]
                    for r in range(rc_block):
                        rows = slice((rb + r) * row_chunk, (rb + r + 1) * row_chunk)
                        cv = c_ref[hd, rows, lanes]
                        bv = bn_ref[hd, rows, _paired_lanes(lt, tm)]
                        for q in range(fk):
                            g = jnp.where(cv >= nd_rows[q], bv * a_rows[q], 0.0)
                            gates[q][r] = g if gates[q][r] is None else gates[q][r] + g
                for q in range(fk):
                    for r in range(rc_block):
                        erows = slice(q * nk + (rb + r) * row_chunk, q * nk + (rb + r + 1) * row_chunk)
                        wt[slot, erows, lanes] = (_gelu_twice(at[slot, erows, lanes]) * gates[q][r]).astype(BF16)
        return carry

    lax.fori_loop(0, groups, group_step, 0)

    @pl.when(e == pl.num_programs(1) - 1)
    def _():
        acc[...] += jnp.dot(vt_last_ref[0], wt[(groups - 1) % 2], preferred_element_type=F32)
        f = acc[...].T
        v = dn_alpha * x_ref[...] + mod_ref[G2:G2 + 1, :] * f
        o_ref[...] = _layer_norm_rows(v, lg_ref[...], lb_ref[...])


PEER_FIRST_KEYS_PER_PASS = 4
PEER_GROUPS_PER_STEP = 4
PEER_ROW_CHUNKS_PER_BLOCK = 2


def _peer_dense(x, mod_l, sel, u_tab, v_tab, ln_g, ln_b, tm, stream_of_tile, dn_alpha):
    ntok, d = x.shape
    n_exp = u_tab.shape[0]
    a_sel, nd_sel, c_sel, bn_sel, h_t = sel
    nk = c_sel.shape[1]
    n_tiles = ntok // tm
    fk, groups = PEER_FIRST_KEYS_PER_PASS, PEER_GROUPS_PER_STEP
    ge = fk * nk
    n_groups = n_exp // ge
    u3 = u_tab.reshape(n_groups, ge, d)
    v3 = v_tab.reshape(n_groups, ge, d)
    u_next = jnp.concatenate([u3[1:], u3[:1]], axis=0).astype(BF16)
    vt_prev = jnp.concatenate([v3[-1:], v3[:-1]], axis=0).transpose(0, 2, 1).astype(BF16)
    u_first = u3[:1].astype(BF16)
    vt_last = v3[-1:].transpose(0, 2, 1).astype(BF16)
    row_spec = pl.BlockSpec((PEER_HEADS, groups * fk, 1, tm), lambda t, e: (0, e, 0, t))
    full_spec = pl.BlockSpec((PEER_HEADS, nk, tm), lambda t, e: (0, 0, t))
    kern = functools.partial(_peer_dense_kernel, tm=tm, groups=groups, fk=fk, nk=nk, dn_alpha=dn_alpha,
                             row_chunk=2 * SUBLANES, rc_block=PEER_ROW_CHUNKS_PER_BLOCK)
    return pl.pallas_call(
        kern,
        grid=(n_tiles, n_groups // groups),
        in_specs=[
            pl.BlockSpec((tm, d), lambda t, e: (t, 0)),
            pl.BlockSpec((None, N_MOD, d), lambda t, e: (stream_of_tile(t), 0, 0)),
            row_spec, row_spec, full_spec, full_spec,
            pl.BlockSpec((d, tm), lambda t, e: (0, t)),
            pl.BlockSpec((groups, ge, d), lambda t, e: (e, 0, 0)),
            pl.BlockSpec((groups, d, ge), lambda t, e: (e, 0, 0)),
            pl.BlockSpec((1, ge, d), lambda t, e: (0, 0, 0)),
            pl.BlockSpec((1, d, ge), lambda t, e: (0, 0, 0)),
            pl.BlockSpec((1, d), lambda t, e: (0, 0)),
            pl.BlockSpec((1, d), lambda t, e: (0, 0)),
        ],
        out_specs=pl.BlockSpec((tm, d), lambda t, e: (t, 0)),
        out_shape=jax.ShapeDtypeStruct((ntok, d), F32),
        scratch_shapes=[
            pltpu.VMEM((d, tm), F32),
            pltpu.VMEM((2, ge, tm), BF16),
            pltpu.VMEM((2, ge, tm), F32),
        ],
        compiler_params=_cparams("parallel", "arbitrary"),
        name="peer_dense",
    )(x, mod_l, a_sel, nd_sel, c_sel, bn_sel, h_t, u_next, vt_prev, u_first, vt_last, ln_g.reshape(1, d), ln_b.reshape(1, d))


def _peer_layer(x, mod_l, w_q, keys, u_tab, v_tab, ln_g, ln_b, tm, stream_of_tile, dn_alpha):
    sel = _peer_route(x, mod_l, w_q, keys, tm, stream_of_tile)
    return _peer_dense(x, mod_l, sel, u_tab, v_tab, ln_g, ln_b, tm, stream_of_tile, dn_alpha)


def _rope(v, cos, sin_signed):
    n = v.shape[-1]
    lane = lax.broadcasted_iota(jnp.int32, v.shape, v.ndim - 1)
    quarter = sin_signed.shape[-1] // 8
    partner = jnp.where(lane % (2 * quarter) < quarter,
                        pltpu.roll(v, n - quarter, v.ndim - 1), pltpu.roll(v, quarter, v.ndim - 1))
    reps = n // cos.shape[-1]
    if reps > 1:
        cos = jnp.concatenate([cos] * reps, axis=-1)
        sin_signed = jnp.concatenate([sin_signed] * reps, axis=-1)
    return v * cos + partner * sin_signed


def _cd_in_kernel(x_ref, mod_ref, w_ref, cos_ref, sin_ref, qg_ref, kg_ref,
                  qm_ref, km_ref, vm_ref, om_ref, qx_ref, ka_ref, va_ref, gt_ref,
                  *, d_m, d_qx, d_kv, att_dh, k_scale):
    h = x_ref[...] * (1.0 + mod_ref[SC1:SC1 + 1, :]) + mod_ref[SH1:SH1 + 1, :]
    z = _bdot(h, w_ref[...])
    qm_ref[...] = z[:, 0:d_m].astype(BF16)
    km_ref[...] = (z[:, d_m:2 * d_m] * k_scale).astype(BF16)
    vm_ref[...] = z[:, 2 * d_m:3 * d_m].astype(BF16)
    om_ref[...] = z[:, 3 * d_m:4 * d_m]
    off = 4 * d_m
    cos = cos_ref[...]
    sin = sin_ref[...]
    qx = z[:, off:off + d_qx]
    pieces = []
    for hq in range(d_qx // LANES):
        blk = qx[:, hq * LANES:(hq + 1) * LANES]
        ms = jnp.sum(blk * blk, axis=-1, keepdims=True) * (1.0 / att_dh)
        pieces.append(blk * lax.rsqrt(ms + RMS_EPS))
    qn = jnp.concatenate(pieces, axis=-1) * qg_ref[...]
    qx_ref[...] = _rope(qn, cos, sin).astype(BF16)
    off += d_qx
    kk = z[:, off:off + d_kv]
    lane = lax.broadcasted_iota(jnp.int32, kk.shape, 1)
    sq = kk * kk
    pieces = []
    for hk in range(d_kv // att_dh):
        sel = (lane >= hk * att_dh) & (lane < (hk + 1) * att_dh)
        ms = jnp.sum(jnp.where(sel, sq, 0.0), axis=-1, keepdims=True) * (1.0 / att_dh)
        pieces.append((sel, lax.rsqrt(ms + RMS_EPS)))
    scale = jnp.zeros_like(kk)
    for sel, r in pieces:
        scale = jnp.where(sel, r, scale)
    ka_ref[...] = _rope(kk * scale * kg_ref[...], cos, sin).astype(BF16)
    off += d_kv
    va_ref[...] = z[:, off:off + d_kv].astype(BF16)
    off += d_kv
    gt_ref[...] = z[:, off:]


def _cd_in(x, mod_l, w, cos_t, sin_t, q_gain, k_gain, tm, n_tiles, in_tile, stream_of_tile, pos_tile,
           d_m, d_qx, d_kv, att_dh, k_scale):
    d = x.shape[1]
    n = w.shape[1]
    ntok = n_tiles * tm
    widths = (d_m, d_m, d_m, d_m, d_qx, d_kv, d_kv, n - 4 * d_m - d_qx - 2 * d_kv)
    dtypes = (BF16, BF16, BF16, F32, BF16, BF16, BF16, F32)
    kern = functools.partial(_cd_in_kernel, d_m=d_m, d_qx=d_qx, d_kv=d_kv, att_dh=att_dh, k_scale=k_scale)
    return pl.pallas_call(
        kern,
        grid=(n_tiles,),
        in_specs=[
            pl.BlockSpec((tm, d), lambda t: (in_tile(t), 0)),
            pl.BlockSpec((None, N_MOD, d), lambda t: (stream_of_tile(t), 0, 0)),
            pl.BlockSpec((d, n), lambda t: (0, 0)),
            pl.BlockSpec((tm, cos_t.shape[1]), lambda t: (pos_tile(t), 0)),
            pl.BlockSpec((tm, sin_t.shape[1]), lambda t: (pos_tile(t), 0)),
            pl.BlockSpec((1, d_qx), lambda t: (0, 0)),
            pl.BlockSpec((1, d_kv), lambda t: (0, 0)),
        ],
        out_specs=[pl.BlockSpec((tm, wd), lambda t: (t, 0)) for wd in widths],
        out_shape=[jax.ShapeDtypeStruct((ntok, wd), dt) for wd, dt in zip(widths, dtypes)],
        compiler_params=_cparams("parallel"),
        name="cd_in",
    )(x, mod_l, w.astype(BF16), cos_t, sin_t, q_gain, k_gain)


def _mlstm_kernel(qf_ref, kf_ref, vf_ref, gf_ref, qb_ref, kb_ref, vb_ref, gb_ref, bias_ref,
                  hf_ref, hb_ref, c_s, n_s, m_s, *, chunk, dh, heads):
    s = pl.program_id(1)

    @pl.when(s == 0)
    def _():
        c_s[...] = jnp.zeros_like(c_s)
        n_s[...] = jnp.zeros_like(n_s)
        m_s[...] = jnp.zeros_like(m_s)

    ri = lax.broadcasted_iota(jnp.int32, (chunk, chunk), 0)
    cj = lax.broadcasted_iota(jnp.int32, (chunk, chunk), 1)
    streams = ((qf_ref, kf_ref, vf_ref, gf_ref, hf_ref), (qb_ref, kb_ref, vb_ref, gb_ref, hb_ref))
    for direction, (q_ref, k_ref, v_ref, g_ref, h_ref) in enumerate(streams):
        mask = (cj <= ri) if direction == 0 else (cj >= ri)
        edge = chunk - 1 if direction == 0 else 0
        gates = g_ref[...] + bias_ref[...]
        logf = jax.nn.log_sigmoid(gates)
        bc = jnp.dot(mask.astype(F32), logf, preferred_element_type=F32, precision=lax.Precision.HIGHEST)
        br = bc.T
        gr = gates.T
        for hd in range(heads):
            ci = (2 * direction) * heads + hd
            cf = (2 * direction + 1) * heads + hd
            idx = direction * heads + hd
            b_col = bc[:, cf:cf + 1]
            b_row = br[cf:cf + 1, :]
            ig_row = gr[ci:ci + 1, :]
            ig_col = gates[:, ci:ci + 1]
            m_prev = m_s[idx][:, 0:1]
            dmat = jnp.where(mask, b_col - b_row + ig_row, NEG_INF)
            inter = b_col + m_prev
            m_t = jnp.maximum(inter, jnp.max(dmat, axis=-1, keepdims=True))
            dexp = jnp.exp(dmat - m_t)
            w_inter = jnp.exp(inter - m_t)
            cols = slice(hd * dh, (hd + 1) * dh)
            q = q_ref[:, cols]
            k = k_ref[:, cols]
            v = v_ref[:, cols]
            c_prev = c_s[idx]
            n_prev = n_s[idx]
            sm = _bdot_nt(q, k) * dexp
            num = _bdot(sm, v) + w_inter * _bdot(q, c_prev)
            den = (jnp.sum(sm, axis=-1, keepdims=True)
                   + w_inter * jnp.sum(q.astype(F32) * n_prev, axis=-1, keepdims=True))
            h_ref[:, cols] = num / jnp.maximum(jnp.abs(den), jnp.exp(-m_t))
            b_last = bc[edge:edge + 1, cf:cf + 1]
            g_log = b_last - b_col + ig_col
            m_new = jnp.maximum(b_last + m_prev, jnp.max(g_log, axis=0, keepdims=True))
            wk = jnp.exp(g_log - m_new)
            decay = jnp.exp(b_last + m_prev - m_new)
            kw = k.astype(F32) * wk
            c_s[idx] = decay * c_prev + lax.dot_general(
                kw.astype(BF16), v, (((0,), (0,)), ((), ())), preferred_element_type=F32)
            n_s[idx] = decay * n_prev + jnp.sum(kw, axis=0, keepdims=True)
            m_s[idx] = jnp.broadcast_to(m_new, (1, LANES))


def _mlstm(qm, km, vm, gt, gate_bias, n_batch, chunks_per_seq, ctx_chunks, heads, dh):
    chunk = MLSTM_CHUNK
    ntok, dm = qm.shape
    ng = gt.shape[1]

    def fwd(b, s):
        return (b * chunks_per_seq + s, 0)

    def bwd(b, s):
        r = jnp.where(s < ctx_chunks, ctx_chunks - 1 - s, ctx_chunks + chunks_per_seq - 1 - s)
        return (b * chunks_per_seq + r, 0)

    tok_f = pl.BlockSpec((chunk, dm), fwd)
    tok_b = pl.BlockSpec((chunk, dm), bwd)
    kern = functools.partial(_mlstm_kernel, chunk=chunk, dh=dh, heads=heads)
    return pl.pallas_call(
        kern,
        grid=(n_batch, chunks_per_seq),
        in_specs=[tok_f, tok_f, tok_f, pl.BlockSpec((chunk, ng), fwd),
                  tok_b, tok_b, tok_b, pl.BlockSpec((chunk, ng), bwd),
                  pl.BlockSpec((1, ng), lambda b, s: (0, 0))],
        out_specs=[pl.BlockSpec((chunk, dm), fwd), pl.BlockSpec((chunk, dm), bwd)],
        out_shape=[jax.ShapeDtypeStruct((ntok, dm), F32)] * 2,
        scratch_shapes=[
            pltpu.VMEM((2 * heads, dh, dh), F32),
            pltpu.VMEM((2 * heads, 1, dh), F32),
            pltpu.VMEM((2 * heads, 1, LANES), F32),
        ],
        compiler_params=_cparams("parallel", "arbitrary"),
        name="mlstm",
    )(qm, km, vm, gt, qm, km, vm, gt, gate_bias)


def _attn_kernel(q_ref, k_ref, v_ref, o_ref, *, n_qh):
    k = k_ref[...]
    v = v_ref[...]
    for hq in range(n_qh):
        cols = slice(hq * LANES, (hq + 1) * LANES)
        s = _bdot_nt(q_ref[:, cols], k)
        p = jnp.exp2(s - jnp.max(s, axis=-1, keepdims=True))
        l = jnp.sum(p, axis=-1, keepdims=True)
        o = jnp.dot(p.astype(BF16), v, preferred_element_type=F32)
        o_ref[:, cols] = (o / l).astype(o_ref.dtype)


def _attention(qx, ka, va, tq, n_batch, seq_tiles, q_tile_of, kv_len):
    n_qh = qx.shape[1] // LANES
    return pl.pallas_call(
        functools.partial(_attn_kernel, n_qh=n_qh),
        grid=(n_batch, seq_tiles),
        in_specs=[
            pl.BlockSpec((tq, qx.shape[1]), lambda b, t: (q_tile_of(b, t), 0)),
            pl.BlockSpec((kv_len, ka.shape[1]), lambda b, t: (b, 0)),
            pl.BlockSpec((kv_len, va.shape[1]), lambda b, t: (b, 0)),
        ],
        out_specs=pl.BlockSpec((tq, qx.shape[1]), lambda b, t: (b * seq_tiles + t, 0)),
        out_shape=jax.ShapeDtypeStruct((n_batch * seq_tiles * tq, qx.shape[1]), BF16),
        compiler_params=_cparams("parallel", "parallel"),
        name="attention",
    )(qx, ka, va)


def _cd_tail_kernel(hf_ref, hb_ref, om_ref, att_ref, x_ref, mod_ref, ng_ref, wm_ref, wa_ref, lg_ref, lb_ref,
                    o_ref, *, dh, heads, dn_alpha):
    hsum = hf_ref[...] + hb_ref[...]
    pieces = []
    for hd in range(heads):
        cols = slice(hd * dh, (hd + 1) * dh)
        blk = hsum[:, cols]
        mu = jnp.mean(blk, axis=-1, keepdims=True)
        dlt = blk - mu
        var = jnp.mean(dlt * dlt, axis=-1, keepdims=True)
        pieces.append(dlt * lax.rsqrt(var + LN_EPS))
    hn = jnp.concatenate(pieces, axis=-1) * ng_ref[...] * jax.nn.sigmoid(om_ref[...])
    y = _bdot(hn, wm_ref[...]) + jnp.dot(att_ref[...], wa_ref[...], preferred_element_type=F32)
    v = dn_alpha * x_ref[...] + mod_ref[G1:G1 + 1, :] * y
    o_ref[...] = _layer_norm_rows(v, lg_ref[...], lb_ref[...])


def _cd_tail(hf, hb, om, att, x, mod_l, norm_g, w_m, w_a, ln_g, ln_b, tm, n_tiles, seq_tile_of,
             stream_of_tile, heads, dh, dn_alpha):
    d = x.shape[1]
    dm = hf.shape[1]
    const2 = lambda t: (0, 0)
    kern = functools.partial(_cd_tail_kernel, dh=dh, heads=heads, dn_alpha=dn_alpha)
    return pl.pallas_call(
        kern,
        grid=(n_tiles,),
        in_specs=[
            pl.BlockSpec((tm, dm), lambda t: (seq_tile_of(t), 0)),
            pl.BlockSpec((tm, dm), lambda t: (seq_tile_of(t), 0)),
            pl.BlockSpec((tm, dm), lambda t: (seq_tile_of(t), 0)),
            pl.BlockSpec((tm, att.shape[1]), lambda t: (t, 0)),
            pl.BlockSpec((tm, d), lambda t: (t, 0)),
            pl.BlockSpec((None, N_MOD, d), lambda t: (stream_of_tile(t), 0, 0)),
            pl.BlockSpec((1, dm), const2),
            pl.BlockSpec(w_m.shape, const2),
            pl.BlockSpec(w_a.shape, const2),
            pl.BlockSpec((1, d), const2),
            pl.BlockSpec((1, d), const2),
        ],
        out_specs=pl.BlockSpec((tm, d), lambda t: (t, 0)),
        out_shape=jax.ShapeDtypeStruct((n_tiles * tm, d), F32),
        compiler_params=_cparams("parallel"),
        name="cd_tail",
    )(hf, hb, om, att, x, mod_l, norm_g.reshape(1, dm), w_m.astype(BF16), w_a.astype(BF16),
      ln_g.reshape(1, d), ln_b.reshape(1, d))


def _layer0(x_lat, x_ctx, mod_l, p, geom):
    tm = geom["tm_mix"]
    n_lat_tiles = geom["n_lat"] // tm
    lat_per_seq = geom["seq"] // tm
    ctx_per_seq = geom["ctx"] // tm
    n_batch = geom["batch"]

    def stream(t):
        return jnp.where(t < n_lat_tiles, t // lat_per_seq, n_batch)

    (z,) = _modulated_matmul(x_lat, x_ctx, mod_l, p["ab_w_in"], (p["ab_w_in"].shape[1],), (F32,), tm, stream)
    x1 = _ab_tail(z, x_lat, x_ctx, mod_l, p["pool_w"], p["pool_ls"], p["conv_w"], p["conv_b"], p["conv_ln_g"],
                  p["conv_ln_b"], p["ab_w_out"], p["ln_g"][0], p["ln_b"][0], tm, n_lat_tiles,
                  lat_per_seq, ctx_per_seq, stream, geom["dn_alpha"])
    tp = geom["tm_peer"]
    n_lat_p = geom["n_lat"] // tp
    lat_per_seq_p = geom["seq"] // tp

    def stream_p(t):
        return jnp.where(t < n_lat_p, t // lat_per_seq_p, n_batch)

    return _peer_layer(x1, mod_l, p["peer_w_q"], p["peer_keys"], p["peer_u"], p["peer_v"],
                       p["ln_g"][1], p["ln_b"][1], tp, stream_p, geom["dn_alpha"])


def _rope_tables(seq, ctx_len, att_dh):
    n_freq = att_dh // 4
    t = jnp.arange(seq)
    freqs = ROPE_THETA ** (-jnp.arange(n_freq, dtype=F32) / n_freq)
    ar = (t // GRID_W).astype(F32)[:, None] * freqs
    ac = (t % GRID_W).astype(F32)[:, None] * freqs
    cos = jnp.concatenate([jnp.cos(ar), jnp.cos(ar), jnp.cos(ac), jnp.cos(ac)], axis=-1)
    sin = jnp.concatenate([-jnp.sin(ar), jnp.sin(ar), -jnp.sin(ac), jnp.sin(ac)], axis=-1)
    cos = jnp.concatenate([jnp.ones((ctx_len, att_dh), F32), cos], axis=0)
    sin = jnp.concatenate([jnp.zeros((ctx_len, att_dh), F32), sin], axis=0)
    reps = LANES // att_dh
    return jnp.tile(cos, (1, reps)), jnp.tile(sin, (1, reps))


def _layer1(xf, mod_l, p, geom):
    tm = geom["tm_mix"]
    n_batch, seq, ctx_len = geom["batch"], geom["seq"], geom["ctx"]
    d = xf.shape[1]
    heads = MLSTM_HEADS
    d_m = p["mlstm_norm_g"].shape[0]
    dh = d_m // heads
    att_dh = p["q_norm_g"].shape[0]
    n_qh, n_kvh = ATT_QH, ATT_KVH
    group = n_qh // n_kvh
    d_q, d_kv = n_qh * att_dh, n_kvh * att_dh
    n_gate = 4 * heads
    ctx_tiles, lat_tiles = ctx_len // tm, seq // tm
    per_b = ctx_tiles + lat_tiles
    n_lat_tiles = n_batch * lat_tiles

    w = p["cd_w_in"]
    cuts = np.cumsum([0, d_m, d_m, d_m, d_m, n_gate, d_q, d_kv, d_kv])
    w_qm, w_km, w_vm, w_om, w_gt, w_qa, w_ka, w_va = (w[:, cuts[i]:cuts[i + 1]] for i in range(8))
    def head_tiles(a, axis):
        parts = []
        for g in range(n_kvh):
            pad = [(0, 0)] * a.ndim
            pad[axis + 1] = (g * att_dh, LANES - (g + 1) * att_dh)
            parts.append(jnp.pad(lax.slice_in_dim(a, g * group, (g + 1) * group, axis=axis), pad))
        return jnp.concatenate(parts, axis=axis)

    w_qx = head_tiles(w_qa.reshape(d, n_qh, att_dh), 1)
    q_gain = head_tiles(jnp.broadcast_to(p["q_norm_g"] * (att_dh ** -0.5 * math.log2(math.e)), (n_qh, att_dh)), 0)
    w_att = head_tiles(p["cd_w_out"][d_m:].reshape(n_qh, att_dh, d), 0)
    d_qx = n_qh * LANES
    w_all = jnp.concatenate([w_qm, w_km, w_vm, w_om, w_qx.reshape(d, d_qx), w_ka, w_va,
                             jnp.pad(w_gt, ((0, 0), (0, LANES - n_gate)))], axis=1)
    gate_bias = jnp.pad(p["mlstm_gate_b"].reshape(1, n_gate), ((0, 0), (0, LANES - n_gate)))
    cos_t, sin_t = _rope_tables(seq, ctx_len, att_dh)

    def in_tile(t):
        b, r = t // per_b, t % per_b
        return jnp.where(r < ctx_tiles, n_lat_tiles + b * ctx_tiles + r, b * lat_tiles + r - ctx_tiles)

    def stream_seq(t):
        return jnp.where(t % per_b < ctx_tiles, n_batch, t // per_b)

    qm, km, vm, om, qx, ka, va, gt = _cd_in(
        xf, mod_l, w_all, cos_t, sin_t, q_gain.reshape(1, d_qx), jnp.tile(p["k_norm_g"], n_kvh).reshape(1, d_kv),
        tm, n_batch * per_b, in_tile, stream_seq, lambda t: t % per_b, d_m, d_qx, d_kv, att_dh, dh ** -0.5)

    chunks_per_seq = (ctx_len + seq) // MLSTM_CHUNK
    hf, hb = _mlstm(qm, km, vm, gt, gate_bias, n_batch, chunks_per_seq, ctx_len // MLSTM_CHUNK, heads, dh)
    att = _attention(qx, ka, va, tm, n_batch, lat_tiles, lambda b, t: b * per_b + ctx_tiles + t,
                     ctx_len + seq)

    def seq_tile_of(t):
        return (t // lat_tiles) * per_b + ctx_tiles + t % lat_tiles

    x1 = _cd_tail(hf, hb, om, att, xf, mod_l, p["mlstm_norm_g"], p["cd_w_out"][:d_m], w_att.reshape(d_qx, d),
                  p["ln_g"][0], p["ln_b"][0], tm, n_lat_tiles, seq_tile_of, lambda t: t // lat_tiles,
                  heads, dh, geom["dn_alpha"])
    tp = geom["tm_peer"]
    lat_per_seq_p = seq // tp
    return _peer_layer(x1, mod_l, p["peer_w_q"], p["peer_keys"], p["peer_u"], p["peer_v"],
                       p["ln_g"][1], p["ln_b"][1], tp, lambda t: t // lat_per_seq_p, geom["dn_alpha"])


def kernel(x, c, ctx, c_ctx, mod_w, mod_b, ln_g, ln_b, ab_w_in, pool_w, pool_ls, conv_w, conv_b,
           conv_ln_g, conv_ln_b, ab_w_out, cd_w_in, mlstm_gate_b, mlstm_norm_g, q_norm_g, k_norm_g,
           cd_w_out, peer_w_q, peer_keys, peer_u, peer_v):
    n_batch, seq, d = x.shape
    ctx_len = ctx.shape[1]
    depth = mod_w.shape[0]
    assert depth == 2, "one pooling/convolution layer followed by one mLSTM/attention layer"
    n_streams = SUBLANES * (-(-(n_batch + 1) // SUBLANES))
    cc = jnp.concatenate([c, c_ctx[None], jnp.zeros((n_streams - n_batch - 1, d), F32)], axis=0)
    mod = _modulation(cc, mod_w, mod_b)
    geom = dict(tm_mix=256, tm_peer=512, n_lat=n_batch * seq, seq=seq, ctx=ctx_len, batch=n_batch,
                dn_alpha=(2 * depth) ** 0.25)
    p0 = dict(ab_w_in=ab_w_in[0], pool_w=pool_w[0], pool_ls=pool_ls[0], conv_w=conv_w[0],
              conv_b=conv_b[0], conv_ln_g=conv_ln_g[0], conv_ln_b=conv_ln_b[0], ab_w_out=ab_w_out[0],
              peer_w_q=peer_w_q[0], peer_keys=peer_keys[0], peer_u=peer_u[0], peer_v=peer_v[0],
              ln_g=ln_g[0], ln_b=ln_b[0])
    xf = _layer0(x.reshape(-1, d), ctx.reshape(-1, d), mod[0], p0, geom)
    p1 = dict(cd_w_in=cd_w_in[0], mlstm_gate_b=mlstm_gate_b[0], mlstm_norm_g=mlstm_norm_g[0],
              q_norm_g=q_norm_g[0], k_norm_g=k_norm_g[0], cd_w_out=cd_w_out[0],
              peer_w_q=peer_w_q[1], peer_keys=peer_keys[1], peer_u=peer_u[1], peer_v=peer_v[1],
              ln_g=ln_g[1], ln_b=ln_b[1])
    out = _layer1(xf, mod[1], p1, geom)
    return out.reshape(n_batch, seq, d)
```

```python
import functools
import math

import jax
import jax.numpy as jnp
import numpy as np
from jax import lax
from jax.experimental import pallas as pl
from jax.experimental.pallas import tpu as pltpu

F32 = jnp.float32
BF16 = jnp.bfloat16

LANES = 128
SUBLANES = 8
VMEM_LIMIT_BYTES = 56 * 1024 * 1024

GRID_W = 64
N_MOD = 6
POOL_WINDOWS = (2, 4, 8, 16)
CONV_W = 31
HALO = 16
MLSTM_HEADS = 4
MLSTM_CHUNK = 128
ATT_QH = 8
ATT_KVH = 2
ROPE_THETA = 10000.0
PEER_HEADS = 8
PEER_NKEYS = 128
PEER_TOPK = 16
LN_EPS = 1e-5
RMS_EPS = 1e-6
NEG_INF = float("-inf")

SH1, SC1, G1, SH2, SC2, G2 = range(N_MOD)


def _cparams(*sem):
    return pltpu.CompilerParams(dimension_semantics=sem, vmem_limit_bytes=VMEM_LIMIT_BYTES)


def _bdot(a, b):
    return jnp.dot(a.astype(BF16), b.astype(BF16), preferred_element_type=F32)


def _bdot_nt(a, b):
    return lax.dot_general(a.astype(BF16), b.astype(BF16), (((1,), (1,)), ((), ())),
                           preferred_element_type=F32)


def _layer_norm_rows(v, g, b):
    mu = jnp.mean(v, axis=-1, keepdims=True)
    d = v - mu
    var = jnp.mean(d * d, axis=-1, keepdims=True)
    return d * lax.rsqrt(var + LN_EPS) * g + b


def _mod_kernel(c_ref, w_ref, b_ref, o_ref):
    c = c_ref[...]
    a = c * jax.nn.sigmoid(c)
    o_ref[...] = jnp.dot(a, w_ref[...], preferred_element_type=F32,
                         precision=lax.Precision.HIGHEST) + b_ref[...]


def _modulation(cc, mod_w, mod_b):
    depth, d, n = mod_w.shape
    rows = cc.shape[0]
    bn = d
    out = pl.pallas_call(
        _mod_kernel,
        grid=(depth, n // bn),
        in_specs=[
            pl.BlockSpec((rows, d), lambda l, j: (0, 0)),
            pl.BlockSpec((None, d, bn), lambda l, j: (l, 0, j)),
            pl.BlockSpec((None, 1, bn), lambda l, j: (l, 0, j)),
        ],
        out_specs=pl.BlockSpec((None, rows, bn), lambda l, j: (l, 0, j)),
        out_shape=jax.ShapeDtypeStruct((depth, rows, n), F32),
        compiler_params=_cparams("parallel", "parallel"),
        name="modulation",
    )(cc, mod_w, mod_b.reshape(depth, 1, n))
    return out.reshape(depth, rows, N_MOD, d)


def _modmm_kernel(xl_ref, xc_ref, mod_ref, w_ref, *o_refs, splits, n_lat_tiles):
    x = jnp.where(pl.program_id(0) < n_lat_tiles, xl_ref[...], xc_ref[...])
    h = x * (1.0 + mod_ref[SC1:SC1 + 1, :]) + mod_ref[SH1:SH1 + 1, :]
    z = _bdot(h, w_ref[...])
    off = 0
    for o_ref, n in zip(o_refs, splits):
        o_ref[...] = z[:, off:off + n].astype(o_ref.dtype)
        off += n


def _stream_tile_specs(tm, d, n_lat_tiles, n_ctx_tiles):
    return [pl.BlockSpec((tm, d), lambda t: (jnp.minimum(t, n_lat_tiles - 1), 0)),
            pl.BlockSpec((tm, d), lambda t: (jnp.clip(t - n_lat_tiles, 0, n_ctx_tiles - 1), 0))]


def _modulated_matmul(x_lat, x_ctx, mod_l, w, splits, out_dtypes, tm, stream_of_tile):
    d = x_lat.shape[1]
    n = w.shape[1]
    assert sum(splits) == n
    n_lat_tiles, n_ctx_tiles = x_lat.shape[0] // tm, x_ctx.shape[0] // tm
    n_tiles = n_lat_tiles + n_ctx_tiles
    return pl.pallas_call(
        functools.partial(_modmm_kernel, splits=splits, n_lat_tiles=n_lat_tiles),
        grid=(n_tiles,),
        in_specs=_stream_tile_specs(tm, d, n_lat_tiles, n_ctx_tiles) + [
            pl.BlockSpec((None, N_MOD, d), lambda t: (stream_of_tile(t), 0, 0)),
            pl.BlockSpec((d, n), lambda t: (0, 0)),
        ],
        out_specs=[pl.BlockSpec((tm, s), lambda t: (t, 0)) for s in splits],
        out_shape=[jax.ShapeDtypeStruct((n_tiles * tm, s), dt) for s, dt in zip(splits, out_dtypes)],
        compiler_params=_cparams("parallel"),
        name="modulated_matmul",
    )(x_lat, x_ctx, mod_l, w.astype(BF16))


def _ab_tail_kernel(zp_ref, zm_ref, zn_ref, xl_ref, xc_ref, mod_ref, pw_ref, pls_ref, cw_ref, cb_ref,
                    cg_ref, cbb_ref, wo_ref, lg_ref, lb_ref, o_ref, zpool, ubuf, ymix, ushift,
                    *, tm, d_pool, d_conv, n_lat_tiles, lat_tiles_per_seq, ctx_tiles_per_seq,
                    dn_alpha, row_chunk):
    t = pl.program_id(0)
    is_lat = t < n_lat_tiles
    per_seq = jnp.where(is_lat, lat_tiles_per_seq, ctx_tiles_per_seq)
    pos_tile = jnp.where(is_lat, t, t - n_lat_tiles) % per_seq
    first = pos_tile == 0
    last = pos_tile == per_seq - 1
    seq_len = per_seq * tm

    def glu(z):
        return z[:, d_pool:d_pool + d_conv] * jax.nn.sigmoid(z[:, d_pool + d_conv:])

    zp = jnp.where(first, 0.0, zp_ref[...])
    zn = jnp.where(last, 0.0, zn_ref[...])
    zm = zm_ref[...]
    zpool[0:HALO, :] = zp[:, :d_pool]
    zpool[HALO:HALO + tm, :] = zm[:, :d_pool]
    zpool[HALO + tm:, :] = zn[:, :d_pool]
    ubuf[0:HALO, :] = glu(zp)
    ubuf[HALO:HALO + tm, :] = glu(zm)
    ubuf[HALO + tm:2 * HALO + tm, :] = glu(zn)

    gw = d_pool // len(POOL_WINDOWS)
    tpos = pos_tile * tm + lax.broadcasted_iota(jnp.int32, (tm, gw), 0)
    for g, w in enumerate(POOL_WINDOWS):
        cols = slice(g * gw, (g + 1) * gw)
        acc = zpool[HALO - w // 2:HALO - w // 2 + tm, cols]
        for s in range(1, w):
            acc = acc + zpool[HALO - w // 2 + s:HALO - w // 2 + s + tm, cols]
        cnt = jnp.minimum(tpos + w // 2, seq_len) - jnp.maximum(tpos - w // 2, 0)
        diff = acc / cnt.astype(F32) - zpool[HALO:HALO + tm, cols]
        ymix[:, cols] = (_bdot(diff, pw_ref[g]) * pls_ref[:, cols]).astype(BF16)

    half = CONV_W // 2
    span = tm + 2 * HALO
    ubuf[span:, :] = jnp.zeros((SUBLANES, d_conv), F32)
    for s in range(SUBLANES):
        ushift[s] = ubuf[s:s + span, :]
    for r in range(0, tm, row_chunk):
        base = HALO - half + r
        acc = None
        for k in range(CONV_W):
            s = (base + k) % SUBLANES
            lo = base + k - s
            term = ushift[s, lo:lo + row_chunk, :] * cw_ref[k:k + 1, :]
            acc = term if acc is None else acc + term
        yn = _layer_norm_rows(acc + cb_ref[...], cg_ref[...], cbb_ref[...])
        ymix[r:r + row_chunk, d_pool:] = (yn * jax.nn.sigmoid(yn)).astype(BF16)

    y = jnp.dot(ymix[...], wo_ref[...], preferred_element_type=F32)
    x = jnp.where(is_lat, xl_ref[...], xc_ref[...])
    v = dn_alpha * x + mod_ref[G1:G1 + 1, :] * y
    o_ref[...] = _layer_norm_rows(v, lg_ref[...], lb_ref[...])


def _ab_tail(z, x_lat, x_ctx, mod_l, pool_w, pool_ls, conv_w, conv_b, cln_g, cln_b, w_out, ln_g, ln_b,
             tm, n_lat_tiles, lat_tiles_per_seq, ctx_tiles_per_seq, stream_of_tile, dn_alpha):
    d = x_lat.shape[1]
    ntok = x_lat.shape[0] + x_ctx.shape[0]
    d_pool = pool_ls.shape[0]
    d_conv = conv_b.shape[0]
    n_tiles = ntok // tm
    hb = tm // HALO
    n_hblocks = ntok // HALO
    row = lambda a: a.reshape(1, -1)
    kern = functools.partial(
        _ab_tail_kernel, tm=tm, d_pool=d_pool, d_conv=d_conv, n_lat_tiles=n_lat_tiles,
        lat_tiles_per_seq=lat_tiles_per_seq, ctx_tiles_per_seq=ctx_tiles_per_seq,
        dn_alpha=dn_alpha, row_chunk=32)
    const2 = lambda t: (0, 0)
    return pl.pallas_call(
        kern,
        grid=(n_tiles,),
        in_specs=[
            pl.BlockSpec((HALO, z.shape[1]), lambda t: (jnp.maximum(t * hb - 1, 0), 0)),
            pl.BlockSpec((tm, z.shape[1]), lambda t: (t, 0)),
            pl.BlockSpec((HALO, z.shape[1]), lambda t: (jnp.minimum((t + 1) * hb, n_hblocks - 1), 0)),
        ] + _stream_tile_specs(tm, d, n_lat_tiles, n_tiles - n_lat_tiles) + [
            pl.BlockSpec((None, N_MOD, d), lambda t: (stream_of_tile(t), 0, 0)),
            pl.BlockSpec(pool_w.shape, lambda t: (0, 0, 0)),
            pl.BlockSpec((1, d_pool), const2),
            pl.BlockSpec(conv_w.shape, const2),
            pl.BlockSpec((1, d_conv), const2),
            pl.BlockSpec((1, d_conv), const2),
            pl.BlockSpec((1, d_conv), const2),
            pl.BlockSpec(w_out.shape, const2),
            pl.BlockSpec((1, d), const2),
            pl.BlockSpec((1, d), const2),
        ],
        out_specs=pl.BlockSpec((tm, d), lambda t: (t, 0)),
        out_shape=jax.ShapeDtypeStruct((ntok, d), F32),
        scratch_shapes=[
            pltpu.VMEM((tm + 2 * HALO, d_pool), F32),
            pltpu.VMEM((tm + 2 * HALO + SUBLANES, d_conv), F32),
            pltpu.VMEM((tm, d_pool + d_conv), BF16),
            pltpu.VMEM((SUBLANES, tm + 2 * HALO, d_conv), F32),
        ],
        compiler_params=_cparams("parallel"),
        name="ab_tail",
    )(z, z, z, x_lat, x_ctx, mod_l, pool_w.astype(BF16), row(pool_ls), conv_w, row(conv_b), row(cln_g),
      row(cln_b), w_out.astype(BF16), row(ln_g), row(ln_b))


def _paired_lanes(lt, tm):
    shifted = (lt + 1) % (tm // LANES)
    return slice(shifted * LANES, (shifted + 1) * LANES)


def _top_values(s, k):
    outs = []
    cur = s
    for it in range(k):
        m = jnp.max(cur, axis=0, keepdims=True)
        outs.append(m)
        if it + 1 < k:
            cur = jnp.where(cur >= m, NEG_INF, cur)
    return outs


def _peer_route_kernel(x_ref, mod_ref, wq_ref, keys_ref, a_ref, nd_ref, c_ref, bn_ref, ht_ref, qt_ref,
                       *, tm, half):
    h = x_ref[...] * (1.0 + mod_ref[SC2:SC2 + 1, :]) + mod_ref[SH2:SH2 + 1, :]
    ht = h.T.astype(BF16)
    ht_ref[...] = ht
    qt_ref[...] = jnp.dot(wq_ref[...], ht, preferred_element_type=F32).astype(BF16)
    k = PEER_TOPK
    for hd in range(PEER_HEADS):
        for lt in range(tm // LANES):
            lanes = slice(lt * LANES, (lt + 1) * LANES)
            r0 = hd * 2 * half
            s1 = jnp.dot(keys_ref[0], qt_ref[r0:r0 + half, lanes], preferred_element_type=F32)
            s2 = jnp.dot(keys_ref[1], qt_ref[r0 + half:r0 + 2 * half, lanes],
                         preferred_element_type=F32)
            t1 = _top_values(s1, k)
            t2 = _top_values(s2, k)
            t2_all = jnp.concatenate(t2, axis=0)
            t1_tail = jnp.concatenate(t1[k // 2:], axis=0)
            pieces = []
            for a in range(k // 2):
                nb = min(k, (k + 1) // (a + 1))
                pieces.append(t1[a] + t2_all[:nb, :])
            pieces.append(t1_tail + t2[0])
            n_cand = sum(p.shape[0] for p in pieces)
            pad = (-n_cand) % SUBLANES
            if pad:
                pieces.append(jnp.full((pad, LANES), NEG_INF, F32))
            cand = jnp.concatenate(pieces, axis=0)
            top = _top_values(cand, k + 1)
            kth = top[k - 1]
            thr = 0.5 * (kth + jnp.maximum(top[k], kth - 1.0))
            m1, m2 = t1[0], t2[0]
            z = jnp.sum(jnp.where(cand >= thr, jnp.exp(cand - (m1 + m2)), 0.0), axis=0, keepdims=True)
            a_ref[hd, :, 0, lanes] = jnp.where(s1 >= t1[k - 1], jnp.exp(s1 - m1), 0.0)
            nd_ref[hd, :, 0, lanes] = -s1
            c_ref[hd, :, lanes] = s2 - thr
            bn_ref[hd, :, _paired_lanes(lt, tm)] = jnp.where(s2 >= t2[k - 1], jnp.exp(s2 - m2), 0.0) / (2.0 * z)


def _peer_route(x, mod_l, w_q, keys, tm, stream_of_tile):
    ntok, d = x.shape
    nq = w_q.shape[1]
    half = keys.shape[2]
    nk = keys.shape[1]
    n_tiles = ntok // tm
    sel_spec = pl.BlockSpec((PEER_HEADS, nk, tm), lambda t: (0, 0, t))
    row_spec = pl.BlockSpec((PEER_HEADS, nk, 1, tm), lambda t: (0, 0, 0, t))
    return pl.pallas_call(
        functools.partial(_peer_route_kernel, tm=tm, half=half),
        grid=(n_tiles,),
        in_specs=[
            pl.BlockSpec((tm, d), lambda t: (t, 0)),
            pl.BlockSpec((None, N_MOD, d), lambda t: (stream_of_tile(t), 0, 0)),
            pl.BlockSpec((nq, d), lambda t: (0, 0)),
            pl.BlockSpec(keys.shape, lambda t: (0, 0, 0)),
        ],
        out_specs=[row_spec, row_spec, sel_spec, sel_spec, pl.BlockSpec((d, tm), lambda t: (0, t))],
        out_shape=[jax.ShapeDtypeStruct((PEER_HEADS, nk, 1, ntok), F32)] * 2
        + [jax.ShapeDtypeStruct((PEER_HEADS, nk, ntok), F32)] * 2 + [jax.ShapeDtypeStruct((d, ntok), BF16)],
        scratch_shapes=[pltpu.VMEM((nq, tm), BF16)],
        compiler_params=_cparams("parallel"),
        name="peer_route",
    )(x, mod_l, w_q.T.astype(BF16), keys.astype(BF16))


def _gelu_twice(a):
    return a * (1.0 + lax.erf(a * (1.0 / math.sqrt(2.0))))


def _peer_dense_kernel(x_ref, mod_ref, a_ref, nd_ref, c_ref, bn_ref, hbf, u_next_ref, vt_prev_ref, u_first_ref,
                       vt_last_ref, lg_ref, lb_ref, o_ref, acc, wt, at,
                       *, tm, groups, fk, nk, dn_alpha, row_chunk, rc_block):
    e = pl.program_id(1)
    assert groups % 2 == 0

    @pl.when(e == 0)
    def _():
        acc[...] = jnp.zeros_like(acc)
        at[0] = jnp.dot(u_first_ref[0], hbf[...], preferred_element_type=F32)
        wt[1] = jnp.zeros(wt.shape[1:], BF16)

    def group_step(p, carry):
        slot = p % 2
        n_lt = tm // LANES
        lt_per_piece = n_lt // PEER_MXU_PIECES
        d_rows = acc.shape[0] // PEER_MXU_PIECES
        ge_rows = (fk * nk) // PEER_MXU_PIECES
        for lt in range(n_lt):
            if lt % lt_per_piece == 0:
                piece = lt // lt_per_piece
                krows = slice(piece * ge_rows, (piece + 1) * ge_rows)
                at[1 - slot, krows, :] = jnp.dot(u_next_ref[p, krows, :], hbf[...], preferred_element_type=F32)
                vrows = slice(piece * d_rows, (piece + 1) * d_rows)
                acc[vrows, :] += jnp.dot(vt_prev_ref[p, vrows, :], wt[1 - slot], preferred_element_type=F32)
            lanes = slice(lt * LANES, (lt + 1) * LANES)
            for rb in range(0, nk // row_chunk, rc_block):
                gates = [[None] * rc_block for _ in range(fk)]
                for hd in range(PEER_HEADS):
                    a_rows = [jnp.broadcast_to(a_ref[hd, p * fk + q, :, lanes], (row_chunk, LANES))
                              for q in range(fk)]
                    nd_rows = [jnp.broadcast_to(nd_ref[hd, p * fk + q, :, lanes], (row_chunk, LANES))
                               for q in range(fk)]
                    for r in range(rc_block):
                        rows = slice((rb + r) * row_chunk, (rb + r + 1) * row_chunk)
                        cv = c_ref[hd, rows, lanes]
                        bv = bn_ref[hd, rows, _paired_lanes(lt, tm)]
                        for q in range(fk):
                            g = jnp.where(cv >= nd_rows[q], bv * a_rows[q], 0.0)
                            gates[q][r] = g if gates[q][r] is None else gates[q][r] + g
                for q in range(fk):
                    for r in range(rc_block):
                        erows = slice(q * nk + (rb + r) * row_chunk, q * nk + (rb + r + 1) * row_chunk)
                        wt[slot, erows, lanes] = (_gelu_twice(at[slot, erows, lanes]) * gates[q][r]).astype(BF16)
        return carry

    lax.fori_loop(0, groups, group_step, 0)

    @pl.when(e == pl.num_programs(1) - 1)
    def _():
        acc[...] += jnp.dot(vt_last_ref[0], wt[(groups - 1) % 2], preferred_element_type=F32)
        f = acc[...].T
        v = dn_alpha * x_ref[...] + mod_ref[G2:G2 + 1, :] * f
        o_ref[...] = _layer_norm_rows(v, lg_ref[...], lb_ref[...])


PEER_FIRST_KEYS_PER_PASS = 4
PEER_GROUPS_PER_STEP = 4
PEER_ROW_CHUNKS_PER_BLOCK = 2
PEER_MXU_PIECES = 2


def _peer_dense(x, mod_l, sel, u_tab, v_tab, ln_g, ln_b, tm, stream_of_tile, dn_alpha):
    ntok, d = x.shape
    n_exp = u_tab.shape[0]
    a_sel, nd_sel, c_sel, bn_sel, h_t = sel
    nk = c_sel.shape[1]
    n_tiles = ntok // tm
    fk, groups = PEER_FIRST_KEYS_PER_PASS, PEER_GROUPS_PER_STEP
    ge = fk * nk
    n_groups = n_exp // ge
    u3 = u_tab.reshape(n_groups, ge, d)
    v3 = v_tab.reshape(n_groups, ge, d)
    u_next = jnp.concatenate([u3[1:], u3[:1]], axis=0).astype(BF16)
    vt_prev = jnp.concatenate([v3[-1:], v3[:-1]], axis=0).transpose(0, 2, 1).astype(BF16)
    u_first = u3[:1].astype(BF16)
    vt_last = v3[-1:].transpose(0, 2, 1).astype(BF16)
    row_spec = pl.BlockSpec((PEER_HEADS, groups * fk, 1, tm), lambda t, e: (0, e, 0, t))
    full_spec = pl.BlockSpec((PEER_HEADS, nk, tm), lambda t, e: (0, 0, t))
    kern = functools.partial(_peer_dense_kernel, tm=tm, groups=groups, fk=fk, nk=nk, dn_alpha=dn_alpha,
                             row_chunk=2 * SUBLANES, rc_block=PEER_ROW_CHUNKS_PER_BLOCK)
    return pl.pallas_call(
        kern,
        grid=(n_tiles, n_groups // groups),
        in_specs=[
            pl.BlockSpec((tm, d), lambda t, e: (t, 0)),
            pl.BlockSpec((None, N_MOD, d), lambda t, e: (stream_of_tile(t), 0, 0)),
            row_spec, row_spec, full_spec, full_spec,
            pl.BlockSpec((d, tm), lambda t, e: (0, t)),
            pl.BlockSpec((groups, ge, d), lambda t, e: (e, 0, 0)),
            pl.BlockSpec((groups, d, ge), lambda t, e: (e, 0, 0)),
            pl.BlockSpec((1, ge, d), lambda t, e: (0, 0, 0)),
            pl.BlockSpec((1, d, ge), lambda t, e: (0, 0, 0)),
            pl.BlockSpec((1, d), lambda t, e: (0, 0)),
            pl.BlockSpec((1, d), lambda t, e: (0, 0)),
        ],
        out_specs=pl.BlockSpec((tm, d), lambda t, e: (t, 0)),
        out_shape=jax.ShapeDtypeStruct((ntok, d), F32),
        scratch_shapes=[
            pltpu.VMEM((d, tm), F32),
            pltpu.VMEM((2, ge, tm), BF16),
            pltpu.VMEM((2, ge, tm), F32),
        ],
        compiler_params=_cparams("parallel", "arbitrary"),
        name="peer_dense",
    )(x, mod_l, a_sel, nd_sel, c_sel, bn_sel, h_t, u_next, vt_prev, u_first, vt_last, ln_g.reshape(1, d), ln_b.reshape(1, d))


def _peer_layer(x, mod_l, w_q, keys, u_tab, v_tab, ln_g, ln_b, tm, stream_of_tile, dn_alpha):
    sel = _peer_route(x, mod_l, w_q, keys, tm, stream_of_tile)
    return _peer_dense(x, mod_l, sel, u_tab, v_tab, ln_g, ln_b, tm, stream_of_tile, dn_alpha)


def _rope(v, cos, sin_signed):
    n = v.shape[-1]
    lane = lax.broadcasted_iota(jnp.int32, v.shape, v.ndim - 1)
    quarter = sin_signed.shape[-1] // 8
    partner = jnp.where(lane % (2 * quarter) < quarter,
                        pltpu.roll(v, n - quarter, v.ndim - 1), pltpu.roll(v, quarter, v.ndim - 1))
    reps = n // cos.shape[-1]
    if reps > 1:
        cos = jnp.concatenate([cos] * reps, axis=-1)
        sin_signed = jnp.concatenate([sin_signed] * reps, axis=-1)
    return v * cos + partner * sin_signed


def _cd_in_kernel(x_ref, mod_ref, w_ref, cos_ref, sin_ref, qg_ref, kg_ref,
                  qm_ref, km_ref, vm_ref, om_ref, qx_ref, ka_ref, va_ref, gt_ref,
                  *, d_m, d_qx, d_kv, att_dh, k_scale):
    h = x_ref[...] * (1.0 + mod_ref[SC1:SC1 + 1, :]) + mod_ref[SH1:SH1 + 1, :]
    z = _bdot(h, w_ref[...])
    qm_ref[...] = z[:, 0:d_m].astype(BF16)
    km_ref[...] = (z[:, d_m:2 * d_m] * k_scale).astype(BF16)
    vm_ref[...] = z[:, 2 * d_m:3 * d_m].astype(BF16)
    om_ref[...] = z[:, 3 * d_m:4 * d_m]
    off = 4 * d_m
    cos = cos_ref[...]
    sin = sin_ref[...]
    qx = z[:, off:off + d_qx]
    pieces = []
    for hq in range(d_qx // LANES):
        blk = qx[:, hq * LANES:(hq + 1) * LANES]
        ms = jnp.sum(blk * blk, axis=-1, keepdims=True) * (1.0 / att_dh)
        pieces.append(blk * lax.rsqrt(ms + RMS_EPS))
    qn = jnp.concatenate(pieces, axis=-1) * qg_ref[...]
    qx_ref[...] = _rope(qn, cos, sin).astype(BF16)
    off += d_qx
    kk = z[:, off:off + d_kv]
    lane = lax.broadcasted_iota(jnp.int32, kk.shape, 1)
    sq = kk * kk
    pieces = []
    for hk in range(d_kv // att_dh):
        sel = (lane >= hk * att_dh) & (lane < (hk + 1) * att_dh)
        ms = jnp.sum(jnp.where(sel, sq, 0.0), axis=-1, keepdims=True) * (1.0 / att_dh)
        pieces.append((sel, lax.rsqrt(ms + RMS_EPS)))
    scale = jnp.zeros_like(kk)
    for sel, r in pieces:
        scale = jnp.where(sel, r, scale)
    ka_ref[...] = _rope(kk * scale * kg_ref[...], cos, sin).astype(BF16)
    off += d_kv
    va_ref[...] = z[:, off:off + d_kv].astype(BF16)
    off += d_kv
    gt_ref[...] = z[:, off:]


def _cd_in(x, mod_l, w, cos_t, sin_t, q_gain, k_gain, tm, n_tiles, in_tile, stream_of_tile, pos_tile,
           d_m, d_qx, d_kv, att_dh, k_scale):
    d = x.shape[1]
    n = w.shape[1]
    ntok = n_tiles * tm
    widths = (d_m, d_m, d_m, d_m, d_qx, d_kv, d_kv, n - 4 * d_m - d_qx - 2 * d_kv)
    dtypes = (BF16, BF16, BF16, F32, BF16, BF16, BF16, F32)
    kern = functools.partial(_cd_in_kernel, d_m=d_m, d_qx=d_qx, d_kv=d_kv, att_dh=att_dh, k_scale=k_scale)
    return pl.pallas_call(
        kern,
        grid=(n_tiles,),
        in_specs=[
            pl.BlockSpec((tm, d), lambda t: (in_tile(t), 0)),
            pl.BlockSpec((None, N_MOD, d), lambda t: (stream_of_tile(t), 0, 0)),
            pl.BlockSpec((d, n), lambda t: (0, 0)),
            pl.BlockSpec((tm, cos_t.shape[1]), lambda t: (pos_tile(t), 0)),
            pl.BlockSpec((tm, sin_t.shape[1]), lambda t: (pos_tile(t), 0)),
            pl.BlockSpec((1, d_qx), lambda t: (0, 0)),
            pl.BlockSpec((1, d_kv), lambda t: (0, 0)),
        ],
        out_specs=[pl.BlockSpec((tm, wd), lambda t: (t, 0)) for wd in widths],
        out_shape=[jax.ShapeDtypeStruct((ntok, wd), dt) for wd, dt in zip(widths, dtypes)],
        compiler_params=_cparams("parallel"),
        name="cd_in",
    )(x, mod_l, w.astype(BF16), cos_t, sin_t, q_gain, k_gain)


def _mlstm_kernel(qf_ref, kf_ref, vf_ref, gf_ref, qb_ref, kb_ref, vb_ref, gb_ref, bias_ref,
                  hf_ref, hb_ref, c_s, n_s, m_s, *, chunk, dh, heads):
    s = pl.program_id(1)

    @pl.when(s == 0)
    def _():
        c_s[...] = jnp.zeros_like(c_s)
        n_s[...] = jnp.zeros_like(n_s)
        m_s[...] = jnp.zeros_like(m_s)

    ri = lax.broadcasted_iota(jnp.int32, (chunk, chunk), 0)
    cj = lax.broadcasted_iota(jnp.int32, (chunk, chunk), 1)
    streams = ((qf_ref, kf_ref, vf_ref, gf_ref, hf_ref), (qb_ref, kb_ref, vb_ref, gb_ref, hb_ref))
    for direction, (q_ref, k_ref, v_ref, g_ref, h_ref) in enumerate(streams):
        mask = (cj <= ri) if direction == 0 else (cj >= ri)
        edge = chunk - 1 if direction == 0 else 0
        gates = g_ref[...] + bias_ref[...]
        logf = jax.nn.log_sigmoid(gates)
        bc = jnp.dot(mask.astype(F32), logf, preferred_element_type=F32, precision=lax.Precision.HIGHEST)
        br = bc.T
        gr = gates.T
        for hd in range(heads):
            ci = (2 * direction) * heads + hd
            cf = (2 * direction + 1) * heads + hd
            idx = direction * heads + hd
            b_col = bc[:, cf:cf + 1]
            b_row = br[cf:cf + 1, :]
            ig_row = gr[ci:ci + 1, :]
            ig_col = gates[:, ci:ci + 1]
            m_prev = m_s[idx][:, 0:1]
            dmat = jnp.where(mask, b_col - b_row + ig_row, NEG_INF)
            inter = b_col + m_prev
            m_t = jnp.maximum(inter, jnp.max(dmat, axis=-1, keepdims=True))
            dexp = jnp.exp(dmat - m_t)
            w_inter = jnp.exp(inter - m_t)
            cols = slice(hd * dh, (hd + 1) * dh)
            q = q_ref[:, cols]
            k = k_ref[:, cols]
            v = v_ref[:, cols]
            c_prev = c_s[idx]
            n_prev = n_s[idx]
            sm = _bdot_nt(q, k) * dexp
            num = _bdot(sm, v) + w_inter * _bdot(q, c_prev)
            den = (jnp.sum(sm, axis=-1, keepdims=True)
                   + w_inter * jnp.sum(q.astype(F32) * n_prev, axis=-1, keepdims=True))
            h_ref[:, cols] = num / jnp.maximum(jnp.abs(den), jnp.exp(-m_t))
            b_last = bc[edge:edge + 1, cf:cf + 1]
            g_log = b_last - b_col + ig_col
            m_new = jnp.maximum(b_last + m_prev, jnp.max(g_log, axis=0, keepdims=True))
            wk = jnp.exp(g_log - m_new)
            decay = jnp.exp(b_last + m_prev - m_new)
            kw = k.astype(F32) * wk
            c_s[idx] = decay * c_prev + lax.dot_general(
                kw.astype(BF16), v, (((0,), (0,)), ((), ())), preferred_element_type=F32)
            n_s[idx] = decay * n_prev + jnp.sum(kw, axis=0, keepdims=True)
            m_s[idx] = jnp.broadcast_to(m_new, (1, LANES))


def _mlstm(qm, km, vm, gt, gate_bias, n_batch, chunks_per_seq, ctx_chunks, heads, dh):
    chunk = MLSTM_CHUNK
    ntok, dm = qm.shape
    ng = gt.shape[1]

    def fwd(b, s):
        return (b * chunks_per_seq + s, 0)

    def bwd(b, s):
        r = jnp.where(s < ctx_chunks, ctx_chunks - 1 - s, ctx_chunks + chunks_per_seq - 1 - s)
        return (b * chunks_per_seq + r, 0)

    tok_f = pl.BlockSpec((chunk, dm), fwd)
    tok_b = pl.BlockSpec((chunk, dm), bwd)
    kern = functools.partial(_mlstm_kernel, chunk=chunk, dh=dh, heads=heads)
    return pl.pallas_call(
        kern,
        grid=(n_batch, chunks_per_seq),
        in_specs=[tok_f, tok_f, tok_f, pl.BlockSpec((chunk, ng), fwd),
                  tok_b, tok_b, tok_b, pl.BlockSpec((chunk, ng), bwd),
                  pl.BlockSpec((1, ng), lambda b, s: (0, 0))],
        out_specs=[pl.BlockSpec((chunk, dm), fwd), pl.BlockSpec((chunk, dm), bwd)],
        out_shape=[jax.ShapeDtypeStruct((ntok, dm), F32)] * 2,
        scratch_shapes=[
            pltpu.VMEM((2 * heads, dh, dh), F32),
            pltpu.VMEM((2 * heads, 1, dh), F32),
            pltpu.VMEM((2 * heads, 1, LANES), F32),
        ],
        compiler_params=_cparams("parallel", "arbitrary"),
        name="mlstm",
    )(qm, km, vm, gt, qm, km, vm, gt, gate_bias)


def _attn_kernel(q_ref, k_ref, v_ref, o_ref, *, n_qh):
    k = k_ref[...]
    v = v_ref[...]
    for hq in range(n_qh):
        cols = slice(hq * LANES, (hq + 1) * LANES)
        s = _bdot_nt(q_ref[:, cols], k)
        p = jnp.exp2(s - jnp.max(s, axis=-1, keepdims=True))
        l = jnp.sum(p, axis=-1, keepdims=True)
        o = jnp.dot(p.astype(BF16), v, preferred_element_type=F32)
        o_ref[:, cols] = (o / l).astype(o_ref.dtype)


def _attention(qx, ka, va, tq, n_batch, seq_tiles, q_tile_of, kv_len):
    n_qh = qx.shape[1] // LANES
    return pl.pallas_call(
        functools.partial(_attn_kernel, n_qh=n_qh),
        grid=(n_batch, seq_tiles),
        in_specs=[
            pl.BlockSpec((tq, qx.shape[1]), lambda b, t: (q_tile_of(b, t), 0)),
            pl.BlockSpec((kv_len, ka.shape[1]), lambda b, t: (b, 0)),
            pl.BlockSpec((kv_len, va.shape[1]), lambda b, t: (b, 0)),
        ],
        out_specs=pl.BlockSpec((tq, qx.shape[1]), lambda b, t: (b * seq_tiles + t, 0)),
        out_shape=jax.ShapeDtypeStruct((n_batch * seq_tiles * tq, qx.shape[1]), BF16),
        compiler_params=_cparams("parallel", "parallel"),
        name="attention",
    )(qx, ka, va)


def _cd_tail_kernel(hf_ref, hb_ref, om_ref, att_ref, x_ref, mod_ref, ng_ref, wm_ref, wa_ref, lg_ref, lb_ref,
                    o_ref, *, dh, heads, dn_alpha):
    hsum = hf_ref[...] + hb_ref[...]
    pieces = []
    for hd in range(heads):
        cols = slice(hd * dh, (hd + 1) * dh)
        blk = hsum[:, cols]
        mu = jnp.mean(blk, axis=-1, keepdims=True)
        dlt = blk - mu
        var = jnp.mean(dlt * dlt, axis=-1, keepdims=True)
        pieces.append(dlt * lax.rsqrt(var + LN_EPS))
    hn = jnp.concatenate(pieces, axis=-1) * ng_ref[...] * jax.nn.sigmoid(om_ref[...])
    y = _bdot(hn, wm_ref[...]) + jnp.dot(att_ref[...], wa_ref[...], preferred_element_type=F32)
    v = dn_alpha * x_ref[...] + mod_ref[G1:G1 + 1, :] * y
    o_ref[...] = _layer_norm_rows(v, lg_ref[...], lb_ref[...])


def _cd_tail(hf, hb, om, att, x, mod_l, norm_g, w_m, w_a, ln_g, ln_b, tm, n_tiles, seq_tile_of,
             stream_of_tile, heads, dh, dn_alpha):
    d = x.shape[1]
    dm = hf.shape[1]
    const2 = lambda t: (0, 0)
    kern = functools.partial(_cd_tail_kernel, dh=dh, heads=heads, dn_alpha=dn_alpha)
    return pl.pallas_call(
        kern,
        grid=(n_tiles,),
        in_specs=[
            pl.BlockSpec((tm, dm), lambda t: (seq_tile_of(t), 0)),
            pl.BlockSpec((tm, dm), lambda t: (seq_tile_of(t), 0)),
            pl.BlockSpec((tm, dm), lambda t: (seq_tile_of(t), 0)),
            pl.BlockSpec((tm, att.shape[1]), lambda t: (t, 0)),
            pl.BlockSpec((tm, d), lambda t: (t, 0)),
            pl.BlockSpec((None, N_MOD, d), lambda t: (stream_of_tile(t), 0, 0)),
            pl.BlockSpec((1, dm), const2),
            pl.BlockSpec(w_m.shape, const2),
            pl.BlockSpec(w_a.shape, const2),
            pl.BlockSpec((1, d), const2),
            pl.BlockSpec((1, d), const2),
        ],
        out_specs=pl.BlockSpec((tm, d), lambda t: (t, 0)),
        out_shape=jax.ShapeDtypeStruct((n_tiles * tm, d), F32),
        compiler_params=_cparams("parallel"),
        name="cd_tail",
    )(hf, hb, om, att, x, mod_l, norm_g.reshape(1, dm), w_m.astype(BF16), w_a.astype(BF16),
      ln_g.reshape(1, d), ln_b.reshape(1, d))


def _layer0(x_lat, x_ctx, mod_l, p, geom):
    tm = geom["tm_mix"]
    n_lat_tiles = geom["n_lat"] // tm
    lat_per_seq = geom["seq"] // tm
    ctx_per_seq = geom["ctx"] // tm
    n_batch = geom["batch"]

    def stream(t):
        return jnp.where(t < n_lat_tiles, t // lat_per_seq, n_batch)

    (z,) = _modulated_matmul(x_lat, x_ctx, mod_l, p["ab_w_in"], (p["ab_w_in"].shape[1],), (F32,), tm, stream)
    x1 = _ab_tail(z, x_lat, x_ctx, mod_l, p["pool_w"], p["pool_ls"], p["conv_w"], p["conv_b"], p["conv_ln_g"],
                  p["conv_ln_b"], p["ab_w_out"], p["ln_g"][0], p["ln_b"][0], tm, n_lat_tiles,
                  lat_per_seq, ctx_per_seq, stream, geom["dn_alpha"])
    tp = geom["tm_peer"]
    n_lat_p = geom["n_lat"] // tp
    lat_per_seq_p = geom["seq"] // tp

    def stream_p(t):
        return jnp.where(t < n_lat_p, t // lat_per_seq_p, n_batch)

    return _peer_layer(x1, mod_l, p["peer_w_q"], p["peer_keys"], p["peer_u"], p["peer_v"],
                       p["ln_g"][1], p["ln_b"][1], tp, stream_p, geom["dn_alpha"])


def _rope_tables(seq, ctx_len, att_dh):
    n_freq = att_dh // 4
    t = jnp.arange(seq)
    freqs = ROPE_THETA ** (-jnp.arange(n_freq, dtype=F32) / n_freq)
    ar = (t // GRID_W).astype(F32)[:, None] * freqs
    ac = (t % GRID_W).astype(F32)[:, None] * freqs
    cos = jnp.concatenate([jnp.cos(ar), jnp.cos(ar), jnp.cos(ac), jnp.cos(ac)], axis=-1)
    sin = jnp.concatenate([-jnp.sin(ar), jnp.sin(ar), -jnp.sin(ac), jnp.sin(ac)], axis=-1)
    cos = jnp.concatenate([jnp.ones((ctx_len, att_dh), F32), cos], axis=0)
    sin = jnp.concatenate([jnp.zeros((ctx_len, att_dh), F32), sin], axis=0)
    reps = LANES // att_dh
    return jnp.tile(cos, (1, reps)), jnp.tile(sin, (1, reps))


def _layer1(xf, mod_l, p, geom):
    tm = geom["tm_mix"]
    n_batch, seq, ctx_len = geom["batch"], geom["seq"], geom["ctx"]
    d = xf.shape[1]
    heads = MLSTM_HEADS
    d_m = p["mlstm_norm_g"].shape[0]
    dh = d_m // heads
    att_dh = p["q_norm_g"].shape[0]
    n_qh, n_kvh = ATT_QH, ATT_KVH
    group = n_qh // n_kvh
    d_q, d_kv = n_qh * att_dh, n_kvh * att_dh
    n_gate = 4 * heads
    ctx_tiles, lat_tiles = ctx_len // tm, seq // tm
    per_b = ctx_tiles + lat_tiles
    n_lat_tiles = n_batch * lat_tiles

    w = p["cd_w_in"]
    cuts = np.cumsum([0, d_m, d_m, d_m, d_m, n_gate, d_q, d_kv, d_kv])
    w_qm, w_km, w_vm, w_om, w_gt, w_qa, w_ka, w_va = (w[:, cuts[i]:cuts[i + 1]] for i in range(8))
    def head_tiles(a, axis):
        parts = []
        for g in range(n_kvh):
            pad = [(0, 0)] * a.ndim
            pad[axis + 1] = (g * att_dh, LANES - (g + 1) * att_dh)
            parts.append(jnp.pad(lax.slice_in_dim(a, g * group, (g + 1) * group, axis=axis), pad))
        return jnp.concatenate(parts, axis=axis)

    w_qx = head_tiles(w_qa.reshape(d, n_qh, att_dh), 1)
    q_gain = head_tiles(jnp.broadcast_to(p["q_norm_g"] * (att_dh ** -0.5 * math.log2(math.e)), (n_qh, att_dh)), 0)
    w_att = head_tiles(p["cd_w_out"][d_m:].reshape(n_qh, att_dh, d), 0)
    d_qx = n_qh * LANES
    w_all = jnp.concatenate([w_qm, w_km, w_vm, w_om, w_qx.reshape(d, d_qx), w_ka, w_va,
                             jnp.pad(w_gt, ((0, 0), (0, LANES - n_gate)))], axis=1)
    gate_bias = jnp.pad(p["mlstm_gate_b"].reshape(1, n_gate), ((0, 0), (0, LANES - n_gate)))
    cos_t, sin_t = _rope_tables(seq, ctx_len, att_dh)

    def in_tile(t):
        b, r = t // per_b, t % per_b
        return jnp.where(r < ctx_tiles, n_lat_tiles + b * ctx_tiles + r, b * lat_tiles + r - ctx_tiles)

    def stream_seq(t):
        return jnp.where(t % per_b < ctx_tiles, n_batch, t // per_b)

    qm, km, vm, om, qx, ka, va, gt = _cd_in(
        xf, mod_l, w_all, cos_t, sin_t, q_gain.reshape(1, d_qx), jnp.tile(p["k_norm_g"], n_kvh).reshape(1, d_kv),
        tm, n_batch * per_b, in_tile, stream_seq, lambda t: t % per_b, d_m, d_qx, d_kv, att_dh, dh ** -0.5)

    chunks_per_seq = (ctx_len + seq) // MLSTM_CHUNK
    hf, hb = _mlstm(qm, km, vm, gt, gate_bias, n_batch, chunks_per_seq, ctx_len // MLSTM_CHUNK, heads, dh)
    att = _attention(qx, ka, va, tm, n_batch, lat_tiles, lambda b, t: b * per_b + ctx_tiles + t,
                     ctx_len + seq)

    def seq_tile_of(t):
        return (t // lat_tiles) * per_b + ctx_tiles + t % lat_tiles

    x1 = _cd_tail(hf, hb, om, att, xf, mod_l, p["mlstm_norm_g"], p["cd_w_out"][:d_m], w_att.reshape(d_qx, d),
                  p["ln_g"][0], p["ln_b"][0], tm, n_lat_tiles, seq_tile_of, lambda t: t // lat_tiles,
                  heads, dh, geom["dn_alpha"])
    tp = geom["tm_peer"]
    lat_per_seq_p = seq // tp
    return _peer_layer(x1, mod_l, p["peer_w_q"], p["peer_keys"], p["peer_u"], p["peer_v"],
                       p["ln_g"][1], p["ln_b"][1], tp, lambda t: t // lat_per_seq_p, geom["dn_alpha"])


def kernel(x, c, ctx, c_ctx, mod_w, mod_b, ln_g, ln_b, ab_w_in, pool_w, pool_ls, conv_w, conv_b,
           conv_ln_g, conv_ln_b, ab_w_out, cd_w_in, mlstm_gate_b, mlstm_norm_g, q_norm_g, k_norm_g,
           cd_w_out, peer_w_q, peer_keys, peer_u, peer_v):
    n_batch, seq, d = x.shape
    ctx_len = ctx.shape[1]
    depth = mod_w.shape[0]
    assert depth == 2, "one pooling/convolution layer followed by one mLSTM/attention layer"
    n_streams = SUBLANES * (-(-(n_batch + 1) // SUBLANES))
    cc = jnp.concatenate([c, c_ctx[None], jnp.zeros((n_streams - n_batch - 1, d), F32)], axis=0)
    mod = _modulation(cc, mod_w, mod_b)
    geom = dict(tm_mix=256, tm_peer=512, n_lat=n_batch * seq, seq=seq, ctx=ctx_len, batch=n_batch,
                dn_alpha=(2 * depth) ** 0.25)
    p0 = dict(ab_w_in=ab_w_in[0], pool_w=pool_w[0], pool_ls=pool_ls[0], conv_w=conv_w[0],
              conv_b=conv_b[0], conv_ln_g=conv_ln_g[0], conv_ln_b=conv_ln_b[0], ab_w_out=ab_w_out[0],
              peer_w_q=peer_w_q[0], peer_keys=peer_keys[0], peer_u=peer_u[0], peer_v=peer_v[0],
              ln_g=ln_g[0], ln_b=ln_b[0])
    xf = _layer0(x.reshape(-1, d), ctx.reshape(-1, d), mod[0], p0, geom)
    p1 = dict(cd_w_in=cd_w_in[0], mlstm_gate_b=mlstm_gate_b[0], mlstm_norm_g=mlstm_norm_g[0],
              q_norm_g=q_norm_g[0], k_norm_g=k_norm_g[0], cd_w_out=cd_w_out[0],
              peer_w_q=peer_w_q[1], peer_keys=peer_keys[1], peer_u=peer_u[1], peer_v=peer_v[1],
              ln_g=ln_g[1], ln_b=ln_b[1])
    out = _layer1(xf, mod[1], p1, geom)
    return out.reshape(n_batch, seq, d)
```
